```python
import math
import jax
import jax.numpy as jnp
from jax import lax
import numpy as np

D_MODEL = 1024
BATCH = 8
SEQ = 2048
DEPTH = 4
DEC_BATCH = 32
DEC_SEQ = 4
PAST_LEN = 8192
PAGE_SIZE = 128

HEAD_DIM = 64
D_MIX = D_MODEL
N_GROUPS = 4
GROUP_WIDTH = D_MIX // N_GROUPS
GDN_HEADS = GROUP_WIDTH // HEAD_DIM
GDN_CHUNK = 64
SHORT_CONV = 4
LRU_WIDTH = GROUP_WIDTH
LRU_BLOCKS = 4
LRU_BLOCK = LRU_WIDTH // LRU_BLOCKS
LRU_C = 8.0
NSA_HEADS = GROUP_WIDTH // HEAD_DIM
NSA_KV_HEADS = 2
NSA_GROUP = NSA_HEADS // NSA_KV_HEADS
CMP_STRIDE = 16
CMP_LEN = 2 * CMP_STRIDE
SLC_BLOCK = 64
SLC_TOPK = 16
N_LOCAL = 2
WINDOW = 512
FORCE_BONUS = 1.0e3
SLC_Q_BLOCK = 32
SB_HEADS = GROUP_WIDTH // HEAD_DIM
Q_BLOCK = 128
D_FF = 2816
FFN_CONV = 3
ROPE_THETA = 10000.0
EPS = 1e-6
NEG = -1e30
IN_SIZES = (3 * GROUP_WIDTH, GROUP_WIDTH, GDN_HEADS, GDN_HEADS,
            LRU_WIDTH, LRU_WIDTH,
            NSA_HEADS * HEAD_DIM, 6 * NSA_KV_HEADS * HEAD_DIM, 3 * NSA_HEADS,
            3 * SB_HEADS * HEAD_DIM)
D_IN = sum(IN_SIZES)

kernel_name = 'hybrid_parallel_groups_decode_step'


def rmsnorm(x, g):
    xf = x.astype(jnp.float32)
    y = xf * lax.rsqrt(jnp.mean(xf * xf, axis=-1, keepdims=True) + EPS)
    return (y * g.astype(jnp.float32)).astype(x.dtype)


def l2norm(x):
    return x * lax.rsqrt(jnp.sum(x * x, axis=-1, keepdims=True) + EPS)


def masked_softmax(s, mask):
    p = jax.nn.softmax(jnp.where(mask, s, NEG), axis=-1)
    return jnp.where(mask, p, 0.0)


def causal_conv(x, buf, w):
    L = x.shape[1]
    xp = jnp.concatenate([buf.astype(x.dtype), x], axis=1)
    y = xp[:, 0:L] * w[0]
    for i in range(1, w.shape[0]):
        y = y + xp[:, i:i + L] * w[i]
    return y, xp[:, L:]


def rope(x, pos):
    half = x.shape[-1] // 2
    inv = ROPE_THETA ** (-jnp.arange(half, dtype=jnp.float32) / half)
    ang = pos.astype(jnp.float32)[:, None] * inv[None, :]
    cos, sin = jnp.cos(ang)[:, None, :], jnp.sin(ang)[:, None, :]
    xf = x.astype(jnp.float32)
    x1, x2 = xf[..., :half], xf[..., half:]
    return jnp.concatenate([x1 * cos - x2 * sin, x2 * cos + x1 * sin], axis=-1).astype(x.dtype)


def split_cols(z):
    out, o = [], 0
    for s in IN_SIZES:
        out.append(z[..., o:o + s])
        o += s
    return out


def gather_pages(pool, l, page_table):
    pages = pool[l, page_table]
    return pages.reshape((page_table.shape[0], page_table.shape[1] * pool.shape[2]) + pool.shape[3:])


def _qblock(n, pref):
    return pref if n % pref == 0 else n


def _to_chunks(t, n, c):
    b, _, h = t.shape[:3]
    t = t.reshape((b, n, c, h) + t.shape[3:])
    return jnp.moveaxis(t, (1, 3), (0, 2))


def gdn_chunked(q, k, v, g, beta, S0):
    f32 = jnp.float32
    B, L, H, Dk = q.shape
    C = min(GDN_CHUNK, L)
    pad = (-L) % C
    q, k, v, g, beta = [t.astype(f32) for t in (q, k, v, g, beta)]
    if pad:
        q, k, v, g, beta = [jnp.pad(t, [(0, 0), (0, pad)] + [(0, 0)] * (t.ndim - 2)) for t in (q, k, v, g, beta)]
    N = (L + pad) // C
    q, k, v = [_to_chunks(t, N, C) for t in (q, k, v)]
    g, beta = _to_chunks(g, N, C), _to_chunks(beta, N, C)
    q = q * Dk ** -0.5
    kb = k * beta[..., None]
    gc = jnp.cumsum(g, axis=-1)
    causal = jnp.tril(jnp.ones((C, C), bool))
    strict = jnp.tril(jnp.ones((C, C), bool), -1)
    decay = jnp.exp(jnp.where(causal, gc[..., :, None] - gc[..., None, :], -jnp.inf))
    A = jnp.where(strict, jnp.einsum('nbhcd,nbhsd->nbhcs', kb, k) * decay, 0.0)
    eye = jnp.eye(C, dtype=f32)
    Tm = lax.linalg.triangular_solve(A + eye, jnp.broadcast_to(eye, A.shape), left_side=True, lower=True)
    u = Tm @ (v * beta[..., None])
    w = Tm @ (kb * jnp.exp(gc)[..., None])

    def step(S, xs):
        qi, ki, ui, wi, gi, di = xs
        v_new = ui - jnp.einsum('bhcd,bhde->bhce', wi, S)
        attn = jnp.einsum('bhcd,bhsd->bhcs', qi, ki) * di
        o = (jnp.einsum('bhcd,bhde->bhce', qi * jnp.exp(gi)[..., None], S)
             + jnp.einsum('bhcs,bhse->bhce', attn, v_new))
        gl = gi[..., -1:]
        S = S * jnp.exp(gl)[..., None] + jnp.einsum('bhcd,bhce->bhde', ki * jnp.exp(gl - gi)[..., None], v_new)
        return S, o

    S, o = lax.scan(step, S0.astype(f32), (q, k, u, w, gc, decay))
    o = jnp.moveaxis(o, (0, 2), (1, 3)).reshape(B, N * C, H, -1)[:, :L]
    return o, S


def gdn_mixer(qkv, gate, b, a, conv_buf, S0, p):
    f32 = jnp.float32
    B, L, _ = qkv.shape
    y, conv_new = causal_conv(qkv, conv_buf, p['gdn_conv_w'])
    y = jax.nn.silu(y.astype(f32)).reshape(B, L, 3, GDN_HEADS, HEAD_DIM)
    q, k, v = l2norm(y[:, :, 0]), l2norm(y[:, :, 1]), y[:, :, 2]
    beta = jax.nn.sigmoid(b.astype(f32))
    g = -jnp.exp(p['gdn_a_log'].astype(f32)) * jax.nn.softplus(a.astype(f32) + p['gdn_dt_bias'].astype(f32))
    o, S = gdn_chunked(q, k, v, g, beta, S0)
    o = rmsnorm(o, p['gdn_norm']) * jax.nn.silu(gate.astype(f32).reshape(B, L, GDN_HEADS, HEAD_DIM))
    return o.reshape(B, L, GROUP_WIDTH).astype(qkv.dtype), conv_new, S


def rglru_mixer(xb, gate, conv_buf, h0, p):
    f32 = jnp.float32
    B, L, _ = xb.shape
    xc, conv_new = causal_conv(xb, conv_buf, p['lru_conv_w'])
    xc = (xc + p['lru_conv_b']).astype(f32)
    xblk = xc.reshape(B, L, LRU_BLOCKS, LRU_BLOCK)
    r = jax.nn.sigmoid(jnp.einsum('blnc,ncd->blnd', xblk, p['lru_wa'].astype(f32)).reshape(B, L, LRU_WIDTH) + p['lru_ba'])
    i = jax.nn.sigmoid(jnp.einsum('blnc,ncd->blnd', xblk, p['lru_wx'].astype(f32)).reshape(B, L, LRU_WIDTH) + p['lru_bx'])
    log_a = -LRU_C * r * jax.nn.softplus(-p['lru_lambda'].astype(f32))
    a = jnp.exp(log_a)
    u = jnp.sqrt(-jnp.expm1(2.0 * log_a)) * (i * xc)
    u = u.at[:, 0].add(a[:, 0] * h0.astype(f32))
    _, h = lax.associative_scan(lambda e1, e2: (e1[0] * e2[0], e2[0] * e1[1] + e2[1]), (a, u), axis=1)
    y = h * jax.nn.gelu(gate.astype(f32))
    return y.astype(xb.dtype), conv_new, h[:, -1]


def nsa_compress(kv, pe, w1, w2):
    B, T = kv.shape[:2]
    m = T // CMP_STRIDE
    ch = kv[:, :m * CMP_STRIDE].reshape(B, m, CMP_STRIDE, NSA_KV_HEADS, HEAD_DIM)
    blk = jnp.concatenate([ch[:, :-1], ch[:, 1:]], axis=2)
    hid = jax.nn.silu(jnp.einsum('bnlgd,lde->bnge', blk + pe[:, None, :], w1))
    return jnp.einsum('bnge,ef->bngf', hid, w2)


def nsa_selected(qg, ctx, idx, pos):
    B, L = qg.shape[:2]
    T = ctx.shape[1]
    ns = -(-T // SLC_BLOCK)
    K = idx.shape[-1]
    ctx = jnp.pad(ctx, ((0, 0), (0, ns * SLC_BLOCK - T), (0, 0), (0, 0), (0, 0)))
    blocks = jnp.transpose(ctx.reshape(B, ns, SLC_BLOCK, 2, NSA_KV_HEADS, HEAD_DIM), (0, 4, 1, 2, 3, 5))
    qb = _qblock(L, SLC_Q_BLOCK)
    nq = L // qb
    q_b = jnp.moveaxis(qg.reshape(B, nq, qb, NSA_KV_HEADS, NSA_GROUP, HEAD_DIM), 1, 0)
    i_b = jnp.moveaxis(idx.reshape(B, NSA_KV_HEADS, nq, qb, K), 2, 0)
    p_b = pos.reshape(nq, qb)
    take = jax.vmap(jax.vmap(lambda blk, ix: blk[ix]))
    scale = HEAD_DIM ** -0.5

    def one(xs):
        qi, ii, pi = xs
        kv = take(blocks, ii).reshape(B, NSA_KV_HEADS, qb, K * SLC_BLOCK, 2, HEAD_DIM)
        kpos = (ii[..., None] * SLC_BLOCK + jnp.arange(SLC_BLOCK)).reshape(B, NSA_KV_HEADS, qb, K * SLC_BLOCK)
        mask = kpos <= pi[None, None, :, None]
        s = jnp.einsum('bqghd,bgqnd->bghqn', qi, kv[..., 0, :]).astype(jnp.float32) * scale
        pr = masked_softmax(s, mask[:, :, None])
        return jnp.einsum('bghqn,bgqnd->bqghd', pr.astype(kv.dtype), kv[..., 1, :])

    o = lax.map(one, (q_b, i_b, p_b))
    return jnp.moveaxis(o, 0, 1).reshape(B, L, NSA_KV_HEADS, NSA_GROUP, HEAD_DIM)


def nsa_window(qg, rows, past, pos0):
    B, L = qg.shape[:2]
    P = past.shape[1]
    scale = HEAD_DIM ** -0.5
    if P == 0 and L % Q_BLOCK == 0:
        nb, nprev = L // Q_BLOCK, WINDOW // Q_BLOCK
        kp = jnp.pad(rows, ((0, 0), (nprev * Q_BLOCK, 0), (0, 0), (0, 0), (0, 0)))
        kp = kp.reshape(B, nb + nprev, Q_BLOCK, 2, NSA_KV_HEADS, HEAD_DIM)
        kw = jnp.concatenate([kp[:, j:j + nb] for j in range(nprev + 1)], axis=2)
        qpos = pos0 + jnp.arange(L).reshape(nb, Q_BLOCK)
        kpos = pos0 + (jnp.arange(nb)[:, None] - nprev) * Q_BLOCK + jnp.arange((nprev + 1) * Q_BLOCK)[None, :]
        d = qpos[:, :, None] - kpos[:, None, :]
        mask = (d >= 0) & (d < WINDOW) & (kpos[:, None, :] >= pos0)
        qb = qg.reshape(B, nb, Q_BLOCK, NSA_KV_HEADS, NSA_GROUP, HEAD_DIM)
        s = jnp.einsum('bnqghd,bnkgd->bnghqk', qb, kw[..., 0, :, :]).astype(jnp.float32) * scale
        pr = masked_softmax(s, mask[None, :, None, None])
        o = jnp.einsum('bnghqk,bnkgd->bnqghd', pr.astype(kw.dtype), kw[..., 1, :, :])
        return o.reshape(B, L, NSA_KV_HEADS, NSA_GROUP, HEAD_DIM)
    ctx = jnp.concatenate([past.astype(rows.dtype), rows], axis=1)
    qpos = pos0 + jnp.arange(L)
    kpos = pos0 - P + jnp.arange(P + L)
    d = qpos[:, None] - kpos[None, :]
    mask = (d >= 0) & (d < WINDOW)
    s = jnp.einsum('bqghd,bkgd->bghqk', qg, ctx[:, :, 0]).astype(jnp.float32) * scale
    pr = masked_softmax(s, mask[None, None, None])
    return jnp.einsum('bghqk,bkgd->bqghd', pr.astype(ctx.dtype), ctx[:, :, 1])


def nsa_mixer(q_raw, kv_raw, gate_raw, pos0, past_cmp, past_slc, past_win, p):
    f32 = jnp.float32
    B, L, _ = q_raw.shape
    dt = q_raw.dtype
    pos = pos0 + jnp.arange(L)
    q = rope(q_raw.reshape(B, L, NSA_HEADS, HEAD_DIM), pos)
    qg = q.reshape(B, L, NSA_KV_HEADS, NSA_GROUP, HEAD_DIM)
    kv = kv_raw.reshape(B, L, 3, 2, NSA_KV_HEADS, HEAD_DIM)
    k_rot = rope(kv[:, :, :, 0].reshape(B, L, 3 * NSA_KV_HEADS, HEAD_DIM), pos).reshape(B, L, 3, NSA_KV_HEADS, HEAD_DIM)
    kv = jnp.stack([k_rot, kv[:, :, :, 1]], axis=3)
    cmp_rows, slc_rows, win_rows = kv[:, :, 0], kv[:, :, 1], kv[:, :, 2]
    ctx_cmp = jnp.concatenate([past_cmp.astype(dt), cmp_rows], axis=1)
    ctx_slc = jnp.concatenate([past_slc.astype(dt), slc_rows], axis=1)
    T = ctx_cmp.shape[1]
    scale = HEAD_DIM ** -0.5
    kc = nsa_compress(ctx_cmp[:, :, 0], p['nsa_cmp_pe'][0], p['nsa_cmp_w1'][0], p['nsa_cmp_w2'][0])
    vc = nsa_compress(ctx_cmp[:, :, 1], p['nsa_cmp_pe'][1], p['nsa_cmp_w1'][1], p['nsa_cmp_w2'][1])
    nc = kc.shape[1]
    c_start = jnp.arange(nc) * CMP_STRIDE
    c_end = c_start + CMP_LEN - 1
    s = jnp.einsum('bqghd,bngd->bghqn', qg, kc).astype(f32) * scale
    pc = masked_softmax(s, (c_end[None, :] <= pos[:, None])[None, None, None])
    o_cmp = jnp.einsum('bghqn,bngd->bqghd', pc.astype(dt), vc)
    ns = -(-T // SLC_BLOCK)
    s_start = jnp.arange(ns) * SLC_BLOCK
    ov = (jnp.minimum(c_end[:, None], s_start[None, :] + SLC_BLOCK - 1)
          - jnp.maximum(c_start[:, None], s_start[None, :]) + 1)
    M = jnp.maximum(ov, 0).astype(f32) / CMP_LEN
    imp = jnp.einsum('bghqn,nm->bgqm', pc, M)
    cur = pos // SLC_BLOCK
    j = jnp.arange(ns)
    valid = j[None, :] <= cur[:, None]
    forced = valid & ((j[None, :] == 0) | (j[None, :] > cur[:, None] - N_LOCAL))
    score = jnp.where(valid, imp + jnp.where(forced, FORCE_BONUS, 0.0), NEG)
    _, idx = lax.top_k(score, min(SLC_TOPK, ns))
    o_slc = nsa_selected(qg, ctx_slc, idx, pos)
    o_win = nsa_window(qg, win_rows, past_win, pos0)
    gt = jax.nn.sigmoid(gate_raw.astype(f32)).reshape(B, L, NSA_KV_HEADS, NSA_GROUP, 3)
    o = gt[..., 0:1] * o_cmp + gt[..., 1:2] * o_slc + gt[..., 2:3] * o_win
    return o.reshape(B, L, GROUP_WIDTH).astype(dt), cmp_rows, slc_rows, win_rows[:, L - min(WINDOW, L):]


def sb_mixer(qkv, pos0, past):
    B, L, _ = qkv.shape
    qkv = qkv.reshape(B, L, 3, SB_HEADS, HEAD_DIM)
    q, rows = qkv[:, :, 0], qkv[:, :, 1:]
    ctx = jnp.concatenate([past.astype(qkv.dtype), rows], axis=1)
    k, v = ctx[:, :, 0], ctx[:, :, 1]
    T = ctx.shape[1]
    kpos = pos0 - past.shape[1] + jnp.arange(T)
    qb = _qblock(L, Q_BLOCK)
    nb = L // qb
    q_b = jnp.moveaxis(q.reshape(B, nb, qb, SB_HEADS, HEAD_DIM), 1, 0)
    p_b = (pos0 + jnp.arange(L)).reshape(nb, qb)
    scale = HEAD_DIM ** -0.5

    def one(xs):
        qi, pi = xs
        z = jnp.einsum('bqhd,bkhd->bhqk', qi, k).astype(jnp.float32) * scale
        mask = kpos[None, :] < pi[:, None]
        log_fail = jnp.where(mask, jax.nn.log_sigmoid(-z), 0.0)
        after = lax.cumsum(log_fail, axis=3, reverse=True) - log_fail
        A = jnp.where(mask, jnp.exp(jax.nn.log_sigmoid(z) + after), 0.0)
        return jnp.einsum('bhqk,bkhd->bqhd', A.astype(v.dtype), v)

    o = lax.map(one, (q_b, p_b))
    return jnp.moveaxis(o, 0, 1).reshape(B, L, GROUP_WIDTH), rows


def conv_ffn(x, buf, p):
    a = x @ p['ffn_w_gate']
    u = x @ p['ffn_w_up']
    ac, buf_new = causal_conv(a, buf, p['ffn_conv_w'])
    return (jax.nn.silu(ac) * u) @ p['ffn_w_down'], buf_new


def trunk_layer(x, p, pos0, gdn_conv, gdn_S, lru_conv, lru_h, past_cmp, past_slc, past_sb, past_win, ffn_buf):
    xn = rmsnorm(x, p['norm1'])
    (gdn_qkv, gdn_gate, gdn_b, gdn_a, lru_x, lru_gate,
     nsa_q, nsa_kv, nsa_gate, sb_qkv) = split_cols(xn @ p['w_in'])
    o_gdn, gdn_conv_new, gdn_S_new = gdn_mixer(gdn_qkv, gdn_gate, gdn_b, gdn_a, gdn_conv, gdn_S, p)
    o_lru, lru_conv_new, lru_h_new = rglru_mixer(lru_x, lru_gate, lru_conv, lru_h, p)
    o_nsa, cmp_rows, slc_rows, win_rows = nsa_mixer(nsa_q, nsa_kv, nsa_gate, pos0, past_cmp, past_slc, past_win, p)
    o_sb, sb_rows = sb_mixer(sb_qkv, pos0, past_sb)
    mix = jnp.concatenate([o_gdn, o_lru, o_nsa, o_sb], axis=-1)
    x = x + mix @ p['w_out']
    f, ffn_new = conv_ffn(rmsnorm(x, p['norm2']), ffn_buf, p)
    x = x + f
    return x, (cmp_rows, slc_rows, sb_rows, win_rows, gdn_conv_new, gdn_S_new, lru_conv_new, lru_h_new, ffn_new)


def setup_inputs(seed: int = 0) -> dict:
    key = jax.random.key(seed)
    keys = iter(jax.random.split(key, 48))
    f32 = jnp.float32

    def nrm(shape, scale=1.0):
        return scale * jax.random.normal(next(keys), shape, f32)

    def gain(shape):
        return 1.0 + 0.01 * jax.random.normal(next(keys), shape, f32)

    n_pages = PAST_LEN // PAGE_SIZE
    n_phys = (5 * DEC_BATCH * n_pages + 3) // 4
    win_buf = min(WINDOW, PAST_LEN)
    perm = jax.random.permutation(next(keys), n_phys)
    page_table = perm[:DEC_BATCH * n_pages].reshape(DEC_BATCH, n_pages).astype(jnp.int32)
    dt0 = jnp.exp(jax.random.uniform(next(keys), (DEPTH, GDN_HEADS), f32, math.log(1e-3), math.log(1e-1)))
    a_log = jnp.log(jax.random.uniform(next(keys), (DEPTH, GDN_HEADS), f32, 1.0, 16.0))
    a0 = jax.random.uniform(next(keys), (DEPTH, LRU_WIDTH), f32, 0.9, 0.999)
    sa = a0 ** (1.0 / LRU_C)
    return {
        'x_prompt': nrm((BATCH, SEQ, D_MODEL)),
        'x_sample': nrm((DEC_BATCH, DEC_SEQ, D_MODEL)),
        'cache_cmp_kv': nrm((DEPTH, n_phys, PAGE_SIZE, 2, NSA_KV_HEADS, HEAD_DIM)),
        'cache_slc_kv': nrm((DEPTH, n_phys, PAGE_SIZE, 2, NSA_KV_HEADS, HEAD_DIM)),
        'cache_sb_kv': nrm((DEPTH, n_phys, PAGE_SIZE, 2, SB_HEADS, HEAD_DIM)),
        'cache_win_kv': nrm((DEPTH, DEC_BATCH, win_buf, 2, NSA_KV_HEADS, HEAD_DIM)),
        'state_gdn_conv': nrm((DEPTH, DEC_BATCH, SHORT_CONV - 1, 3 * GROUP_WIDTH)),
        'state_gdn': nrm((DEPTH, DEC_BATCH, GDN_HEADS, HEAD_DIM, HEAD_DIM), 0.5),
        'state_lru_conv': nrm((DEPTH, DEC_BATCH, SHORT_CONV - 1, LRU_WIDTH)),
        'state_lru': nrm((DEPTH, DEC_BATCH, LRU_WIDTH), 0.5),
        'state_ffn_conv': nrm((DEPTH, DEC_BATCH, FFN_CONV - 1, D_FF)),
        'page_table': page_table,
        'norm1': gain((DEPTH, D_MODEL)),
        'w_in': nrm((DEPTH, D_MODEL, D_IN), D_MODEL ** -0.5),
        'gdn_conv_w': nrm((DEPTH, SHORT_CONV, 3 * GROUP_WIDTH), SHORT_CONV ** -0.5),
        'gdn_a_log': a_log,
        'gdn_dt_bias': dt0 + jnp.log(-jnp.expm1(-dt0)),
        'gdn_norm': gain((DEPTH, HEAD_DIM)),
        'lru_conv_w': nrm((DEPTH, SHORT_CONV, LRU_WIDTH), SHORT_CONV ** -0.5),
        'lru_conv_b': nrm((DEPTH, LRU_WIDTH), 0.01),
        'lru_wa': nrm((DEPTH, LRU_BLOCKS, LRU_BLOCK, LRU_BLOCK), LRU_BLOCK ** -0.5),
        'lru_ba': nrm((DEPTH, LRU_WIDTH), 0.01),
        'lru_wx': nrm((DEPTH, LRU_BLOCKS, LRU_BLOCK, LRU_BLOCK), LRU_BLOCK ** -0.5),
        'lru_bx': nrm((DEPTH, LRU_WIDTH), 0.01),
        'lru_lambda': jnp.log(sa) - jnp.log1p(-sa),
        'nsa_cmp_pe': nrm((DEPTH, 2, CMP_LEN, HEAD_DIM), 0.1),
        'nsa_cmp_w1': nrm((DEPTH, 2, CMP_LEN, HEAD_DIM, HEAD_DIM), (CMP_LEN * HEAD_DIM) ** -0.5),
        'nsa_cmp_w2': nrm((DEPTH, 2, HEAD_DIM, HEAD_DIM), HEAD_DIM ** -0.5),
        'w_out': nrm((DEPTH, D_MIX, D_MODEL), D_MIX ** -0.5),
        'norm2': gain((DEPTH, D_MODEL)),
        'ffn_w_gate': nrm((DEPTH, D_MODEL, D_FF), D_MODEL ** -0.5),
        'ffn_w_up': nrm((DEPTH, D_MODEL, D_FF), D_MODEL ** -0.5),
        'ffn_conv_w': nrm((DEPTH, FFN_CONV, D_FF), FFN_CONV ** -0.5),
        'ffn_w_down': nrm((DEPTH, D_FF, D_MODEL), D_FF ** -0.5),
        'norm_f': gain((D_MODEL,)),
    }


def reference(x_prompt, x_sample, cache_cmp_kv, cache_slc_kv, cache_sb_kv, cache_win_kv,
              state_gdn_conv, state_gdn, state_lru_conv, state_lru, state_ffn_conv, page_table,
              norm1, w_in, gdn_conv_w, gdn_a_log, gdn_dt_bias, gdn_norm,
              lru_conv_w, lru_conv_b, lru_wa, lru_ba, lru_wx, lru_bx, lru_lambda,
              nsa_cmp_pe, nsa_cmp_w1, nsa_cmp_w2, w_out, norm2,
              ffn_w_gate, ffn_w_up, ffn_conv_w, ffn_w_down, norm_f):
    layers = [dict(norm1=norm1[l], w_in=w_in[l], gdn_conv_w=gdn_conv_w[l], gdn_a_log=gdn_a_log[l],
                   gdn_dt_bias=gdn_dt_bias[l], gdn_norm=gdn_norm[l], lru_conv_w=lru_conv_w[l],
                   lru_conv_b=lru_conv_b[l], lru_wa=lru_wa[l], lru_ba=lru_ba[l], lru_wx=lru_wx[l],
                   lru_bx=lru_bx[l], lru_lambda=lru_lambda[l], nsa_cmp_pe=nsa_cmp_pe[l],
                   nsa_cmp_w1=nsa_cmp_w1[l], nsa_cmp_w2=nsa_cmp_w2[l], w_out=w_out[l], norm2=norm2[l],
                   ffn_w_gate=ffn_w_gate[l], ffn_w_up=ffn_w_up[l], ffn_conv_w=ffn_conv_w[l],
                   ffn_w_down=ffn_w_down[l]) for l in range(DEPTH)]
    G, H, D = NSA_KV_HEADS, SB_HEADS, HEAD_DIM

    B = x_prompt.shape[0]
    dt = x_prompt.dtype
    h = x_prompt
    p_new = []
    for l in range(DEPTH):
        h, st = trunk_layer(
            h, layers[l], 0,
            jnp.zeros((B, SHORT_CONV - 1, 3 * GROUP_WIDTH), dt),
            jnp.zeros((B, GDN_HEADS, HEAD_DIM, HEAD_DIM), jnp.float32),
            jnp.zeros((B, SHORT_CONV - 1, LRU_WIDTH), dt),
            jnp.zeros((B, LRU_WIDTH), jnp.float32),
            jnp.zeros((B, 0, 2, G, D), dt), jnp.zeros((B, 0, 2, G, D), dt),
            jnp.zeros((B, 0, 2, H, D), dt), jnp.zeros((B, 0, 2, G, D), dt),
            jnp.zeros((B, FFN_CONV - 1, D_FF), dt))
        p_new.append(st)
    y_prompt = rmsnorm(h, norm_f)

    pos0 = page_table.shape[1] * cache_sb_kv.shape[2]
    h = x_sample
    s_new = []
    for l in range(DEPTH):
        h, st = trunk_layer(
            h, layers[l], pos0,
            state_gdn_conv[l], state_gdn[l], state_lru_conv[l], state_lru[l],
            gather_pages(cache_cmp_kv, l, page_table), gather_pages(cache_slc_kv, l, page_table),
            gather_pages(cache_sb_kv, l, page_table), cache_win_kv[l], state_ffn_conv[l])
        s_new.append(st)
    y_sample = rmsnorm(h, norm_f)

    def stk(outs, i):
        return jnp.stack([o[i] for o in outs])

    p_cmp_kv, p_slc_kv, p_sb_kv, p_win_kv = stk(p_new, 0), stk(p_new, 1), stk(p_new, 2), stk(p_new, 3)
    p_gdn_conv, p_gdn, p_lru_conv, p_lru, p_ffn_conv = stk(p_new, 4), stk(p_new, 5), stk(p_new, 6), stk(p_new, 7), stk(p_new, 8)
    s_cmp_kv, s_slc_kv, s_sb_kv, s_win_kv = stk(s_new, 0), stk(s_new, 1), stk(s_new, 2), stk(s_new, 3)
    s_gdn_conv, s_gdn, s_lru_conv, s_lru, s_ffn_conv = stk(s_new, 4), stk(s_new, 5), stk(s_new, 6), stk(s_new, 7), stk(s_new, 8)
    return (y_prompt, y_sample,
            p_cmp_kv, p_slc_kv, p_sb_kv, p_win_kv, p_gdn_conv, p_gdn, p_lru_conv, p_lru, p_ffn_conv,
            s_cmp_kv, s_slc_kv, s_sb_kv, s_win_kv, s_gdn_conv, s_gdn, s_lru_conv, s_lru, s_ffn_conv)
```

```python
import functools
import math

import jax
import jax.numpy as jnp
from jax import lax
from jax.experimental import pallas as pl
from jax.experimental.pallas import tpu as pltpu

F32, BF16 = jnp.float32, jnp.bfloat16

D_MODEL = 1024
HEAD_DIM = 64
GROUP_WIDTH = 256
N_HEADS = 4
NSA_KV_HEADS = 2
NSA_GROUP = 2
D_FF = 2816
SHORT_CONV = 4
FFN_CONV = 3
LRU_C = 8.0
CMP_STRIDE = 16
CMP_LEN = 32
SLC_BLOCK = 64
SLC_TOPK = 16
N_LOCAL = 2
WINDOW = 512
FORCE_BONUS = 1.0e3
ROPE_THETA = 10000.0
EPS = 1e-6
NEG = -1e30
SCALE = HEAD_DIM ** -0.5
PAGE_SIZE = 128

VMEM_LIMIT_BYTES = 56 * 1024 * 1024
LANES = 128
SUBLANES = 8

IN_WIDTHS = (1024, 512, 256, 256, 512, 256, 256, 256, 128)
D_IN_PAD = sum(IN_WIDTHS)


def _params(*sem):
    return pltpu.CompilerParams(dimension_semantics=sem, vmem_limit_bytes=VMEM_LIMIT_BYTES)


def _bdot(a, b):
    return jnp.dot(a.astype(BF16), b.astype(BF16), preferred_element_type=F32)


def _bdot_nt(a, b):
    return lax.dot_general(a.astype(BF16), b.astype(BF16), (((1,), (1,)), ((), ())), preferred_element_type=F32)


def _bdot_tn(a, b):
    return lax.dot_general(a.astype(BF16), b.astype(BF16), (((0,), (0,)), ((), ())), preferred_element_type=F32)


def _fdot(a, b):
    return jnp.dot(a, b, preferred_element_type=F32, precision=lax.Precision.HIGHEST)


def _split_dot(a, b_exact):
    hi = a.astype(BF16)
    lo = (a - hi.astype(F32)).astype(BF16)
    return (jnp.dot(hi, b_exact, preferred_element_type=F32) + jnp.dot(lo, b_exact, preferred_element_type=F32))


def _sigmoid(x):
    return jax.nn.sigmoid(x)


def _silu(x):
    return x * jax.nn.sigmoid(x)


def _softplus(x):
    return jnp.maximum(x, 0.0) + jnp.log1p(jnp.exp(-jnp.abs(x)))


def _gelu_tanh(x):
    return 0.5 * x * (1.0 + jnp.tanh(math.sqrt(2.0 / math.pi) * (x + 0.044715 * (x * x * x))))


def _rms(x, g):
    return x * lax.rsqrt(jnp.mean(x * x, axis=-1, keepdims=True) + EPS) * g


def _rope_pairs(v, cos, sin):
    lane = lax.broadcasted_iota(jnp.int32, v.shape, 1)
    first_half = (lane % HEAD_DIM) < (HEAD_DIM // 2)
    partner = jnp.where(first_half, pltpu.roll(v, LANES - HEAD_DIM // 2, 1), pltpu.roll(v, HEAD_DIM // 2, 1))
    return v * cos + partner * sin


def _masked_softmax(s, mask):
    sm = jnp.where(mask, s, NEG)
    m = jnp.max(sm, axis=-1, keepdims=True)
    e = jnp.where(mask, jnp.exp(sm - m), 0.0)
    den = jnp.sum(e, axis=-1, keepdims=True)
    return e * (1.0 / jnp.where(den > 0.0, den, 1.0))


def _in_proj_kernel(x_ref, g_ref, w_ref, cos_ref, sin_ref,
                    zg_ref, zl_ref, nq_ref, sq_ref, skv_ref, cmp_ref, slc_ref, win_ref, sm_ref):
    xb = _rms(x_ref[...], g_ref[...]).astype(BF16)
    cos, sin = cos_ref[...], sin_ref[...]

    def mm(lo, width):
        return jnp.dot(xb, w_ref[:, lo:lo + width], preferred_element_type=F32)

    zg_ref[...] = mm(0, 1024)
    zl_ref[...] = mm(1024, 512)
    q = mm(1536, 256)
    nq_ref[:, 0:128] = _rope_pairs(q[:, 0:128], cos, sin)
    nq_ref[:, 128:256] = _rope_pairs(q[:, 128:256], cos, sin)
    sq_ref[...] = mm(1792, 256)
    skv_ref[...] = mm(2048, 512)
    for ref, lo in ((cmp_ref, 2560), (slc_ref, 2816), (win_ref, 3072)):
        kv = mm(lo, 256)
        ref[:, 0:128] = _rope_pairs(kv[:, 0:128], cos, sin)
        ref[:, 128:256] = kv[:, 128:256]
    sm_ref[...] = mm(3328, 128)


def _in_proj(x2d, g, w, cos, sin, tm, table_tiles):
    n = x2d.shape[0]
    nt = n // tm
    row = lambda wd: pl.BlockSpec((tm, wd), lambda i: (i, 0))
    tab = pl.BlockSpec((tm, LANES), lambda i: (i % table_tiles, 0))
    return pl.pallas_call(
        _in_proj_kernel,
        grid=(nt,),
        in_specs=[row(D_MODEL), pl.BlockSpec((1, D_MODEL), lambda i: (0, 0)),
                  pl.BlockSpec((D_MODEL, D_IN_PAD), lambda i: (0, 0)), tab, tab],
        out_specs=[row(wd) for wd in IN_WIDTHS],
        out_shape=[jax.ShapeDtypeStruct((n, wd), F32) for wd in IN_WIDTHS],
        compiler_params=_params("parallel"),
    )(x2d, g, w, cos, sin)


def _out_ffn_kernel(*refs, tm, tf, nf, stride, prev_rows, tiles_per_seq, final):
    x_ref, mix_refs, refs = refs[0], refs[1:5], refs[5:]
    if final:
        (wo_ref, g2_ref, wg_ref, wu_ref, cw_ref, wd_ref, buf_ref, gf_ref,
         xo_ref, st_ref, y_ref, xn_s, x1_s, acc_s, carry_s) = refs
    else:
        (wo_ref, g2_ref, wg_ref, wu_ref, cw_ref, wd_ref, buf_ref,
         xo_ref, st_ref, xn_s, x1_s, acc_s, carry_s) = refs
    i, j = pl.program_id(0), pl.program_id(1)

    @pl.when(j == 0)
    def _():
        mix = jnp.concatenate([m[...] for m in mix_refs], axis=1)
        x1 = x_ref[...] + _bdot(mix, wo_ref[...])
        x1_s[...] = x1
        xn_s[...] = _rms(x1, g2_ref[...]).astype(BF16)
        acc_s[...] = jnp.zeros_like(acc_s)

    xn = xn_s[...]
    a = jnp.dot(xn, wg_ref[...], preferred_element_type=F32)
    u = jnp.dot(xn, wu_ref[...], preferred_element_type=F32)
    seq_start = (i % tiles_per_seq) == 0
    prev = jnp.where(seq_start, buf_ref[...], carry_s[j])
    ext = jnp.concatenate([prev, a], axis=0)
    cw = cw_ref[...]
    p, s = prev_rows, stride
    ac = cw[0:1] * ext[p - 2 * s:p - 2 * s + tm] + cw[1:2] * ext[p - s:p - s + tm] + cw[2:3] * ext[p:p + tm]
    carry_s[j] = ext[tm:tm + p]
    st_ref[0] = ext[tm + p - 2 * s:tm + p]
    h = _silu(ac) * u
    acc_s[...] += _bdot(h, wd_ref[...])

    @pl.when(j == nf - 1)
    def _():
        x2 = x1_s[...] + acc_s[...]
        xo_ref[...] = x2
        if final:
            y_ref[...] = _rms(x2, gf_ref[...])


def _out_ffn(x2d, mixes, wo, g2, wg, wu, cw, wd, buf, gf, *, tm, stride, tiles_per_seq, final):
    n = x2d.shape[0]
    nt = n // tm
    nf = 2
    tf = D_FF // nf
    prev_rows = buf.shape[0]
    kern = functools.partial(_out_ffn_kernel, tm=tm, tf=tf, nf=nf, stride=stride, prev_rows=prev_rows,
                             tiles_per_seq=tiles_per_seq, final=final)
    tok = pl.BlockSpec((tm, D_MODEL), lambda i, j: (i, 0))
    part = pl.BlockSpec((tm, GROUP_WIDTH), lambda i, j: (i, 0))
    in_specs = [tok, part, part, part, part,
                pl.BlockSpec((D_MODEL, D_MODEL), lambda i, j: (0, 0)),
                pl.BlockSpec((1, D_MODEL), lambda i, j: (0, 0)),
                pl.BlockSpec((D_MODEL, tf), lambda i, j: (0, j)),
                pl.BlockSpec((D_MODEL, tf), lambda i, j: (0, j)),
                pl.BlockSpec((FFN_CONV, tf), lambda i, j: (0, j)),
                pl.BlockSpec((tf, D_MODEL), lambda i, j: (j, 0)),
                pl.BlockSpec((prev_rows, tf), lambda i, j: (0, j))]
    args = [x2d, *mixes, wo, g2, wg, wu, cw, wd, buf]
    out_specs = [tok, pl.BlockSpec((1, 2 * stride, tf), lambda i, j: (i, 0, j))]
    out_shape = [jax.ShapeDtypeStruct((n, D_MODEL), F32),
                 jax.ShapeDtypeStruct((nt, 2 * stride, D_FF), F32)]
    if final:
        in_specs.append(pl.BlockSpec((1, D_MODEL), lambda i, j: (0, 0)))
        args.append(gf)
        out_specs.append(tok)
        out_shape.append(jax.ShapeDtypeStruct((n, D_MODEL), F32))
    return pl.pallas_call(
        kern,
        grid=(nt, nf),
        in_specs=in_specs,
        out_specs=out_specs,
        out_shape=out_shape,
        scratch_shapes=[pltpu.VMEM((tm, D_MODEL), BF16), pltpu.VMEM((tm, D_MODEL), F32),
                        pltpu.VMEM((tm, D_MODEL), F32), pltpu.VMEM((nf, prev_rows, tf), F32)],
        compiler_params=_params("arbitrary", "arbitrary"),
    )(*args)


def _short_conv(x, carry_ref, cw, tl):
    ext = jnp.concatenate([carry_ref[...], x], axis=0)
    y = cw[0:1] * ext[5:5 + tl]
    for i in range(1, SHORT_CONV):
        y = y + cw[i:i + 1] * ext[5 + i:5 + i + tl]
    carry_ref[...] = ext[tl:tl + SUBLANES]
    return y, ext


def _gdn_kernel(zg_ref, sm_ref, cst_ref, s0_ref, cw_ref, par_ref, nrm_ref,
                o_ref, cout_ref, sout_ref, carry_s, state_s, *, tl, chunk, l_real):
    t = pl.program_id(1)

    @pl.when(t == 0)
    def _():
        carry_s[...] = cst_ref[0]
        state_s[...] = s0_ref[0]

    z = zg_ref[0]
    gate = z[:, 768:1024]
    y, ext = _short_conv(z[:, 0:768], carry_s, cw_ref[...], tl)
    cout_ref[0] = ext[SUBLANES + l_real - 3:SUBLANES + l_real]
    y = _silu(y)

    small = sm_ref[0]
    par = par_ref[...]
    beta_c = _sigmoid(small)
    g_c = -jnp.exp(par[0:1]) * _softplus(small + par[1:2])
    if l_real < tl:
        real = lax.broadcasted_iota(jnp.int32, small.shape, 0) < l_real
        beta_c = jnp.where(real, beta_c, 0.0)
        g_c = jnp.where(real, g_c, 0.0)
    ri = lax.broadcasted_iota(jnp.int32, (tl, tl), 0)
    ci = lax.broadcasted_iota(jnp.int32, (tl, tl), 1)
    same = (ri // chunk) == (ci // chunk)
    gc = _fdot(jnp.where(same & (ci <= ri), 1.0, 0.0), g_c)
    grest = _fdot(jnp.where(same & (ci > ri), 1.0, 0.0), g_c)
    gc_t = gc.T

    ii = lax.broadcasted_iota(jnp.int32, (chunk, chunk), 0)
    jj = lax.broadcasted_iota(jnp.int32, (chunk, chunk), 1)
    causal, strict = ii >= jj, ii > jj
    eye = jnp.where(ii == jj, 1.0, 0.0)
    gnorm = nrm_ref[...]
    n_chunks = tl // chunk
    outs = [[None] * N_HEADS for _ in range(n_chunks)]
    for h in range(N_HEADS):
        S = state_s[h]
        for c in range(n_chunks):
            r0 = c * chunk
            rows = slice(r0, r0 + chunk)
            q = y[rows, h * 64:(h + 1) * 64]
            k = y[rows, 256 + h * 64:256 + (h + 1) * 64]
            v = y[rows, 512 + h * 64:512 + (h + 1) * 64]
            q = q * lax.rsqrt(jnp.sum(q * q, axis=-1, keepdims=True) + EPS) * SCALE
            k = k * lax.rsqrt(jnp.sum(k * k, axis=-1, keepdims=True) + EPS)
            beta = beta_c[rows, h:h + 1]
            gcc = gc[rows, 4 + h:5 + h]
            gcr = gc_t[4 + h:5 + h, rows]
            decay = jnp.where(causal, jnp.exp(jnp.where(causal, gcc - gcr, 0.0)), 0.0)
            kb = k * beta
            A = jnp.where(strict, _bdot_nt(kb, k) * decay, 0.0)
            T = eye - A
            P = A
            span = 2
            while span < chunk:
                P = _fdot(P, P)
                T = T + _fdot(T, P)
                span *= 2
            u = _bdot(T, v * beta)
            w = _bdot(T, kb * jnp.exp(gcc))
            v_new = u - _bdot(w, S)
            attn = _bdot_nt(q, k) * decay
            o = _bdot(q * jnp.exp(gcc), S) + _bdot(attn, v_new)
            g_last = gc[r0 + chunk - 1:r0 + chunk, 4 + h:5 + h]
            S = S * jnp.exp(g_last) + _bdot_tn(k * jnp.exp(grest[rows, 4 + h:5 + h]), v_new)
            o = _rms(o, gnorm) * _silu(gate[rows, h * 64:(h + 1) * 64])
            outs[c][h] = o
        state_s[h] = S
        sout_ref[0, h] = S
    o_ref[0] = jnp.concatenate([jnp.concatenate(r, axis=1) for r in outs], axis=0)


def _gdn(zg, small, conv_state8, s0, cw, par, gnorm, *, tl, chunk, l_real):
    b, l, _ = zg.shape
    nt = l // tl
    kern = functools.partial(_gdn_kernel, tl=tl, chunk=chunk, l_real=l_real)
    return pl.pallas_call(
        kern,
        grid=(b, nt),
        in_specs=[pl.BlockSpec((1, tl, 1024), lambda i, t: (i, t, 0)),
                  pl.BlockSpec((1, tl, LANES), lambda i, t: (i, t, 0)),
                  pl.BlockSpec((1, SUBLANES, 768), lambda i, t: (i, 0, 0)),
                  pl.BlockSpec((1, N_HEADS, 64, 64), lambda i, t: (i, 0, 0, 0)),
                  pl.BlockSpec((SHORT_CONV, 768), lambda i, t: (0, 0)),
                  pl.BlockSpec((2, LANES), lambda i, t: (0, 0)),
                  pl.BlockSpec((1, HEAD_DIM), lambda i, t: (0, 0))],
        out_specs=[pl.BlockSpec((1, tl, GROUP_WIDTH), lambda i, t: (i, t, 0)),
                   pl.BlockSpec((1, 3, 768), lambda i, t: (i, 0, 0)),
                   pl.BlockSpec((1, N_HEADS, 64, 64), lambda i, t: (i, 0, 0, 0))],
        out_shape=[jax.ShapeDtypeStruct((b, l, GROUP_WIDTH), F32),
                   jax.ShapeDtypeStruct((b, 3, 768), F32),
                   jax.ShapeDtypeStruct((b, N_HEADS, 64, 64), F32)],
        scratch_shapes=[pltpu.VMEM((SUBLANES, 768), F32), pltpu.VMEM((N_HEADS, 64, 64), F32)],
        compiler_params=_params("parallel", "arbitrary"),
    )(zg, small, conv_state8, s0, cw, par, gnorm)


def _lru_kernel(zl_ref, cst_ref, h0_ref, cw_ref, vec_ref, wa_ref, wx_ref,
                y_ref, cout_ref, hout_ref, carry_s, h_s, *, tl, l_real):
    t = pl.program_id(1)

    @pl.when(t == 0)
    def _():
        carry_s[...] = cst_ref[0]
        h_s[...] = h0_ref[0]

    z = zl_ref[0]
    gate = z[:, 256:512]
    vec = vec_ref[...]
    xc, ext = _short_conv(z[:, 0:256], carry_s, cw_ref[...], tl)
    cout_ref[0] = ext[SUBLANES + l_real - 3:SUBLANES + l_real]
    xc = xc + vec[0:1]
    r = _sigmoid(_bdot(xc, wa_ref[...]) + vec[1:2])
    i = _sigmoid(_bdot(xc, wx_ref[...]) + vec[2:3])
    log_a = -LRU_C * r * _softplus(-vec[3:4])
    a = jnp.exp(log_a)
    u = jnp.sqrt(1.0 - a * a) * (i * xc)
    rowid = lax.broadcasted_iota(jnp.int32, u.shape, 0)
    u = u + jnp.where(rowid == 0, a * h_s[...], 0.0)
    s = 1
    while s < tl:
        a_sh = jnp.concatenate([jnp.ones((s, GROUP_WIDTH), F32), a[:tl - s]], axis=0)
        u_sh = jnp.concatenate([jnp.zeros((s, GROUP_WIDTH), F32), u[:tl - s]], axis=0)
        u = a * u_sh + u
        a = a * a_sh
        s *= 2
    h_last = u[l_real - 1:l_real]
    h_s[...] = h_last
    hout_ref[0] = h_last
    y_ref[0] = u * _gelu_tanh(gate)


def _lru(zl, conv_state8, h0, cw, vec, wa, wx, *, tl, l_real):
    b, l, _ = zl.shape
    nt = l // tl
    kern = functools.partial(_lru_kernel, tl=tl, l_real=l_real)
    return pl.pallas_call(
        kern,
        grid=(b, nt),
        in_specs=[pl.BlockSpec((1, tl, 512), lambda i, t: (i, t, 0)),
                  pl.BlockSpec((1, SUBLANES, GROUP_WIDTH), lambda i, t: (i, 0, 0)),
                  pl.BlockSpec((1, 1, GROUP_WIDTH), lambda i, t: (i, 0, 0)),
                  pl.BlockSpec((SHORT_CONV, GROUP_WIDTH), lambda i, t: (0, 0)),
                  pl.BlockSpec((4, GROUP_WIDTH), lambda i, t: (0, 0)),
                  pl.BlockSpec((GROUP_WIDTH, GROUP_WIDTH), lambda i, t: (0, 0)),
                  pl.BlockSpec((GROUP_WIDTH, GROUP_WIDTH), lambda i, t: (0, 0))],
        out_specs=[pl.BlockSpec((1, tl, GROUP_WIDTH), lambda i, t: (i, t, 0)),
                   pl.BlockSpec((1, 3, GROUP_WIDTH), lambda i, t: (i, 0, 0)),
                   pl.BlockSpec((1, 1, GROUP_WIDTH), lambda i, t: (i, 0, 0))],
        out_shape=[jax.ShapeDtypeStruct((b, l, GROUP_WIDTH), F32),
                   jax.ShapeDtypeStruct((b, 3, GROUP_WIDTH), F32),
                   jax.ShapeDtypeStruct((b, 1, GROUP_WIDTH), F32)],
        scratch_shapes=[pltpu.VMEM((SUBLANES, GROUP_WIDTH), F32), pltpu.VMEM((1, GROUP_WIDTH), F32)],
        compiler_params=_params("parallel", "arbitrary"),
    )(zl, conv_state8, h0, cw, vec, wa, wx)


def _compress_tokens(x, pea, peb, wa, wb, w2):
    pa = _bdot(x + pea, wa)
    pb = _bdot(x + peb, wb)
    return pa, pb


def _compress_finish(pa, pb, w2):
    pb_next = jnp.concatenate([pb[1:], jnp.zeros((1, pb.shape[1]), F32)], axis=0)
    return _bdot(_silu(pa + pb_next), w2)


def _cmp_prompt_kernel(x_ref, pea_ref, peb_ref, wa_ref, wb_ref, w2_ref, o_ref):
    pa, pb = _compress_tokens(x_ref[0], pea_ref[...], peb_ref[...], wa_ref[...], wb_ref[...], None)
    o_ref[0] = _compress_finish(pa, pb, w2_ref[...])


def _cmp_prompt(xc, pea, peb, wa, wb, w2):
    b, m, f = xc.shape
    full = lambda a: pl.BlockSpec(a.shape, lambda i: (0,) * a.ndim)
    return pl.pallas_call(
        _cmp_prompt_kernel,
        grid=(b,),
        in_specs=[pl.BlockSpec((1, m, f), lambda i: (i, 0, 0)), full(pea), full(peb), full(wa), full(wb), full(w2)],
        out_specs=pl.BlockSpec((1, m, 256), lambda i: (i, 0, 0)),
        out_shape=jax.ShapeDtypeStruct((b, m, 256), F32),
        compiler_params=_params("parallel"),
    )(xc, pea, peb, wa, wb, w2)


def _overlap_matrix(m, nsp, nc, ns):
    nn = lax.broadcasted_iota(jnp.int32, (m, nsp), 0)
    mm = lax.broadcasted_iota(jnp.int32, (m, nsp), 1)
    ov = (jnp.minimum(nn * CMP_STRIDE + CMP_LEN - 1, mm * SLC_BLOCK + SLC_BLOCK - 1)
          - jnp.maximum(nn * CMP_STRIDE, mm * SLC_BLOCK) + 1)
    return jnp.where((nn < nc) & (mm < ns), jnp.maximum(ov, 0).astype(F32) * (1.0 / CMP_LEN), 0.0)


def _select_blocks(imp, qpos, ns):
    rows, nsp = imp.shape
    j = lax.broadcasted_iota(jnp.int32, (rows, nsp), 1)
    cur = qpos // SLC_BLOCK
    valid = j <= cur
    forced = valid & ((j == 0) | (j > cur - N_LOCAL))
    score = jnp.where(valid, imp + jnp.where(forced, FORCE_BONUS, 0.0), NEG)
    score = jnp.where(j < ns, score, -3.0e38)
    cnt = jnp.zeros((rows, nsp), jnp.int32)
    for i in range(ns):
        si = score[:, i:i + 1]
        beats = (si > score) | ((si == score) & (i < j))
        cnt = cnt + beats.astype(jnp.int32)
    return jnp.where(cnt < min(SLC_TOPK, ns), 1.0, 0.0)


def _nsa_prompt_kernel(q_ref, sm_ref, kc_ref, slc_ref, win_ref, e_ref, o_ref, *, tq, l, nc, ns, wk, nsp):
    q0 = pl.program_id(1) * tq
    q = q_ref[0]
    gates = _sigmoid(sm_ref[0])
    kcvc = kc_ref[0]
    m = kcvc.shape[0]
    qpos = q0 + lax.broadcasted_iota(jnp.int32, (tq, 1), 0)
    qpos2 = jnp.concatenate([qpos, qpos], axis=0)
    n_id = lax.broadcasted_iota(jnp.int32, (2 * tq, m), 1)
    cmask = (n_id * CMP_STRIDE + CMP_LEN - 1 <= qpos2) & (n_id < nc)
    ovl = _overlap_matrix(m, nsp, nc, ns)
    kpos = lax.broadcasted_iota(jnp.int32, (tq, l), 1)
    start = pl.multiple_of(jnp.clip(q0 + tq - wk, 0, l - wk), SUBLANES)
    dwin = qpos2 - (start + lax.broadcasted_iota(jnp.int32, (2 * tq, wk), 1))
    wmask = (dwin >= 0) & (dwin < WINDOW)
    for g in range(NSA_KV_HEADS):
        h0 = g * NSA_GROUP
        qs = jnp.concatenate([q[:, h0 * 64:(h0 + 1) * 64], q[:, (h0 + 1) * 64:(h0 + 2) * 64]], axis=0)
        kcol = slice(g * 64, (g + 1) * 64)
        vcol = slice(128 + g * 64, 128 + (g + 1) * 64)
        pc = _masked_softmax(_bdot_nt(qs, kcvc[:, kcol]) * SCALE, cmask)
        o_cmp = _bdot(pc, kcvc[:, vcol])
        sel = _select_blocks(_fdot(pc[:tq] + pc[tq:], ovl), qpos, ns)
        selk = jnp.dot(sel.astype(BF16), e_ref[...], preferred_element_type=F32)
        smask = (selk > 0.5) & (kpos <= qpos)
        p2 = _masked_softmax(_bdot_nt(qs, slc_ref[0, :, kcol]) * SCALE, jnp.concatenate([smask, smask], axis=0))
        o_slc = _bdot(p2, slc_ref[0, :, vcol])
        p3 = _masked_softmax(_bdot_nt(qs, win_ref[0, pl.ds(start, wk), kcol]) * SCALE, wmask)
        o_win = _bdot(p3, win_ref[0, pl.ds(start, wk), vcol])
        for hg in range(NSA_GROUP):
            head = h0 + hg
            rows = slice(hg * tq, (hg + 1) * tq)
            c0 = 8 + 3 * head
            o_ref[0, :, head * 64:(head + 1) * 64] = (gates[:, c0:c0 + 1] * o_cmp[rows]
                                                      + gates[:, c0 + 1:c0 + 2] * o_slc[rows]
                                                      + gates[:, c0 + 2:c0 + 3] * o_win[rows])


def _block_expander(nsp, nkeys):
    blk = lax.broadcasted_iota(jnp.int32, (nsp, nkeys), 0)
    key = lax.broadcasted_iota(jnp.int32, (nsp, nkeys), 1)
    return jnp.where(key // SLC_BLOCK == blk, 1.0, 0.0).astype(BF16)


def _nsa_prompt(q, small, kcvc, slc, win, *, tq):
    b, l, _ = q.shape
    m = kcvc.shape[1]
    nc = m - 1
    ns = -(-l // SLC_BLOCK)
    nsp = LANES
    wk = min(WINDOW + tq, l)
    kern = functools.partial(_nsa_prompt_kernel, tq=tq, l=l, nc=nc, ns=ns, wk=wk, nsp=nsp)
    seq = lambda wd: pl.BlockSpec((1, l, wd), lambda i, t: (i, 0, 0))
    tile = lambda wd: pl.BlockSpec((1, tq, wd), lambda i, t: (i, t, 0))
    return pl.pallas_call(
        kern,
        grid=(b, l // tq),
        in_specs=[tile(256), tile(LANES), pl.BlockSpec((1, m, 256), lambda i, t: (i, 0, 0)), seq(256), seq(256),
                  pl.BlockSpec((nsp, l), lambda i, t: (0, 0))],
        out_specs=tile(256),
        out_shape=jax.ShapeDtypeStruct((b, l, 256), F32),
        compiler_params=_params("parallel", "arbitrary"),
    )(q, small, kcvc, slc, win, _block_expander(nsp, l))


def _later_keys_matrix(n):
    a = lax.broadcasted_iota(jnp.int32, (n, n), 0)
    c = lax.broadcasted_iota(jnp.int32, (n, n), 1)
    return jnp.where(a > c, 1.0, 0.0).astype(BF16)


def _sb_block(z, mask, later, aft_prev):
    sp = _softplus(z)
    lf = -sp if mask is None else jnp.where(mask, -sp, 0.0)
    aft = _split_dot(lf, later) + aft_prev
    a = jnp.exp(z - sp + aft)
    if mask is not None:
        a = jnp.where(mask, a, 0.0)
    return a, aft_prev + jnp.sum(lf, axis=-1, keepdims=True)


def _sb_prompt_kernel(q_ref, kv_ref, later_ref, o_ref, acc_s, aft_s, *, tq):
    qi = pl.program_id(1)
    q = q_ref[0]
    later = later_ref[...]
    acc_s[...] = jnp.zeros_like(acc_s)
    aft_s[...] = jnp.zeros_like(aft_s)
    qrow = qi * tq + lax.broadcasted_iota(jnp.int32, (tq, tq), 0)
    kcol = lax.broadcasted_iota(jnp.int32, (tq, tq), 1)

    def body(it, carry):
        k0 = pl.multiple_of((qi - it) * tq, tq)
        mask = (k0 + kcol) < qrow
        for h in range(N_HEADS):
            kh = kv_ref[0, pl.ds(k0, tq), h * 64:(h + 1) * 64]
            vh = kv_ref[0, pl.ds(k0, tq), 256 + h * 64:256 + (h + 1) * 64]
            z = _bdot_nt(q[:, h * 64:(h + 1) * 64], kh) * SCALE
            a, aft = _sb_block(z, mask, later, aft_s[h])
            acc_s[:, h * 64:(h + 1) * 64] += _bdot(a, vh)
            aft_s[h] = aft
        return carry

    lax.fori_loop(0, qi + 1, body, 0)
    o_ref[0] = acc_s[...]


def _sb_prompt(q, kv, *, tq):
    b, l, _ = q.shape
    kern = functools.partial(_sb_prompt_kernel, tq=tq)
    return pl.pallas_call(
        kern,
        grid=(b, l // tq),
        in_specs=[pl.BlockSpec((1, tq, 256), lambda i, t: (i, t, 0)),
                  pl.BlockSpec((1, l, 512), lambda i, t: (i, 0, 0)),
                  pl.BlockSpec((tq, tq), lambda i, t: (0, 0))],
        out_specs=pl.BlockSpec((1, tq, 256), lambda i, t: (i, t, 0)),
        out_shape=jax.ShapeDtypeStruct((b, l, 256), F32),
        scratch_shapes=[pltpu.VMEM((tq, 256), F32), pltpu.VMEM((N_HEADS, tq, 1), F32)],
        compiler_params=_params("parallel", "arbitrary"),
    )(q, kv, _later_keys_matrix(tq))


def _sb_sample_kernel(pt_ref, qbd_ref, new_ref, later_ref, *rest, pg, n_pg, pos0):
    pages = rest[:pg]
    o_ref, acc_s, aft_s = rest[pg:]
    g = pl.program_id(1)
    qbd = qbd_ref[0]
    later = later_ref[...]
    rows = qbd.shape[0]

    def block(kv, mask):
        z = _bdot_nt(qbd, kv[:, 0:256]) * SCALE
        a, aft = _sb_block(z, mask, later, aft_s[...])
        acc_s[...] += _bdot(a, kv[:, 256:512])
        aft_s[...] = aft

    @pl.when(g == 0)
    def _():
        acc_s[...] = jnp.zeros_like(acc_s)
        aft_s[...] = jnp.zeros_like(aft_s)
        t = lax.broadcasted_iota(jnp.int32, (rows, PAGE_SIZE), 0) % SUBLANES
        key = lax.broadcasted_iota(jnp.int32, (rows, PAGE_SIZE), 1)
        block(new_ref[0], key < t)

    for i in range(pg):
        block(pages[i][0], None)

    @pl.when(g == n_pg - 1)
    def _():
        acc = acc_s[...]
        lane_head = lax.broadcasted_iota(jnp.int32, (SUBLANES, 256), 1) // HEAD_DIM
        out = jnp.zeros((SUBLANES, 256), F32)
        for h in range(N_HEADS):
            out = out + jnp.where(lane_head == h, acc[h * SUBLANES:(h + 1) * SUBLANES], 0.0)
        o_ref[0] = out


def _sb_sample(page_table, qbd, new_rows, cache, layer_base, *, pg, pos0):
    b, n_pages = page_table.shape
    n_pg = n_pages // pg
    kern = functools.partial(_sb_sample_kernel, pg=pg, n_pg=n_pg, pos0=pos0)

    def page_spec(i):
        return pl.BlockSpec((1, PAGE_SIZE, 512),
                            lambda bi, g, pt: (layer_base + pt[bi, (n_pg - 1 - g) * pg + (pg - 1 - i)], 0, 0))

    grid_spec = pltpu.PrefetchScalarGridSpec(
        num_scalar_prefetch=1,
        grid=(b, n_pg),
        in_specs=[pl.BlockSpec((1, 32, 256), lambda bi, g, pt: (bi, 0, 0)),
                  pl.BlockSpec((1, PAGE_SIZE, 512), lambda bi, g, pt: (bi, 0, 0)),
                  pl.BlockSpec((PAGE_SIZE, PAGE_SIZE), lambda bi, g, pt: (0, 0))]
                 + [page_spec(i) for i in range(pg)],
        out_specs=pl.BlockSpec((1, SUBLANES, 256), lambda bi, g, pt: (bi, 0, 0)),
        scratch_shapes=[pltpu.VMEM((32, 256), F32), pltpu.VMEM((32, 1), F32)],
    )
    return pl.pallas_call(
        kern,
        grid_spec=grid_spec,
        out_shape=jax.ShapeDtypeStruct((b, SUBLANES, 256), F32),
        compiler_params=_params("parallel", "arbitrary"),
    )(page_table, qbd, new_rows, _later_keys_matrix(PAGE_SIZE), *([cache] * pg))


CMP_PAGE_GROUP = 16


def _nsa_sample_cmp_kernel(pt_ref, q_ref, pea_ref, peb_ref, wa_ref, wb_ref, w2_ref, *rest, n_pages, pos0, nsp):
    pages = rest[:n_pages]
    ocmp_ref, sel_ref = rest[n_pages:]
    pas, pbs = [], []
    for c in range(0, n_pages, CMP_PAGE_GROUP):
        x = jnp.concatenate([pages[i][0] for i in range(c, min(c + CMP_PAGE_GROUP, n_pages))], axis=0)
        pa, pb = _compress_tokens(x, pea_ref[...], peb_ref[...], wa_ref[...], wb_ref[...], None)
        pas.append(pa)
        pbs.append(pb)
    kcvc = _compress_finish(jnp.concatenate(pas, axis=0), jnp.concatenate(pbs, axis=0), w2_ref[...])
    m = kcvc.shape[0]
    nc = m - 1
    ns = 2 * n_pages + 1
    q = q_ref[0]
    tq = q.shape[0]
    qpos = pos0 + lax.broadcasted_iota(jnp.int32, (tq, 1), 0)
    qpos2 = jnp.concatenate([qpos, qpos], axis=0)
    n_id = lax.broadcasted_iota(jnp.int32, (2 * tq, m), 1)
    cmask = (n_id * CMP_STRIDE + CMP_LEN - 1 <= qpos2) & (n_id < nc)
    ovl = _overlap_matrix(m, nsp, nc, ns)
    for g in range(NSA_KV_HEADS):
        h0 = g * NSA_GROUP
        qs = jnp.concatenate([q[:, h0 * 64:(h0 + 1) * 64], q[:, (h0 + 1) * 64:(h0 + 2) * 64]], axis=0)
        pc = _masked_softmax(_bdot_nt(qs, kcvc[:, g * 64:(g + 1) * 64]) * SCALE, cmask)
        o_cmp = _bdot(pc, kcvc[:, 128 + g * 64:128 + (g + 1) * 64])
        sel_ref[0, g] = _select_blocks(_fdot(pc[:tq] + pc[tq:], ovl), qpos, ns)
        for hg in range(NSA_GROUP):
            ocmp_ref[0, :, (h0 + hg) * 64:(h0 + hg + 1) * 64] = o_cmp[hg * tq:(hg + 1) * tq]


def _nsa_sample_cmp(page_table, q, pea, peb, wa, wb, w2, cache, layer_base, *, pos0):
    b, n_pages = page_table.shape
    nsp = -(-(2 * n_pages + 1) // LANES) * LANES
    kern = functools.partial(_nsa_sample_cmp_kernel, n_pages=n_pages, pos0=pos0, nsp=nsp)
    full = lambda a: pl.BlockSpec(a.shape, lambda bi, pt: (0,) * a.ndim)

    def page_spec(i):
        return pl.BlockSpec((1, SUBLANES, 4096), lambda bi, pt: (layer_base + pt[bi, i], 0, 0))

    grid_spec = pltpu.PrefetchScalarGridSpec(
        num_scalar_prefetch=1,
        grid=(b,),
        in_specs=[pl.BlockSpec((1, SUBLANES, 256), lambda bi, pt: (bi, 0, 0)),
                  full(pea), full(peb), full(wa), full(wb), full(w2)] + [page_spec(i) for i in range(n_pages)],
        out_specs=[pl.BlockSpec((1, SUBLANES, 256), lambda bi, pt: (bi, 0, 0)),
                   pl.BlockSpec((1, NSA_KV_HEADS, SUBLANES, nsp), lambda bi, pt: (bi, 0, 0, 0))],
    )
    return pl.pallas_call(
        kern,
        grid_spec=grid_spec,
        out_shape=[jax.ShapeDtypeStruct((b, SUBLANES, 256), F32),
                   jax.ShapeDtypeStruct((b, NSA_KV_HEADS, SUBLANES, nsp), F32)],
        compiler_params=_params("parallel"),
    )(page_table, q, pea, peb, wa, wb, w2, *([cache] * n_pages))


def _nsa_sample_slc_kernel(pt_ref, q_ref, sm_ref, ocmp_ref, sel_ref, newslc_ref, pastwin_ref, newwin_ref, e_ref,
                           *rest, n_pages, pos0):
    pages = rest[:n_pages]
    o_ref, z_s = rest[n_pages:]
    q = q_ref[0]
    tq = q.shape[0]
    gates = _sigmoid(sm_ref[0])
    nk = (n_pages + 1) * PAGE_SIZE
    past = pastwin_ref.shape[1]
    qpos = pos0 + lax.broadcasted_iota(jnp.int32, (tq, 1), 0)
    qpos2 = jnp.concatenate([qpos, qpos], axis=0)
    kpos = lax.broadcasted_iota(jnp.int32, (tq, nk), 1)
    kposw = jnp.concatenate([pos0 - past + lax.broadcasted_iota(jnp.int32, (2 * tq, past), 1),
                             pos0 + lax.broadcasted_iota(jnp.int32, (2 * tq, PAGE_SIZE), 1)], axis=1)
    dwin = qpos2 - kposw
    wmask = (dwin >= 0) & (dwin < WINDOW)
    for g in range(NSA_KV_HEADS):
        h0 = g * NSA_GROUP
        kcol = slice(g * 64, (g + 1) * 64)
        vcol = slice(128 + g * 64, 128 + (g + 1) * 64)
        qs = jnp.concatenate([q[:, h0 * 64:(h0 + 1) * 64], q[:, (h0 + 1) * 64:(h0 + 2) * 64]], axis=0)
        for p in range(n_pages):
            z_s[:, p * PAGE_SIZE:(p + 1) * PAGE_SIZE] = _bdot_nt(qs, pages[p][0, :, kcol])
        z_s[:, n_pages * PAGE_SIZE:nk] = _bdot_nt(qs, newslc_ref[0, :, kcol])
        selk = jnp.dot(sel_ref[0, g].astype(BF16), e_ref[...], preferred_element_type=F32)
        smask = (selk > 0.5) & (kpos <= qpos)
        p2 = _masked_softmax(z_s[...] * SCALE, jnp.concatenate([smask, smask], axis=0))
        o_slc = _bdot(p2[:, n_pages * PAGE_SIZE:nk], newslc_ref[0, :, vcol])
        for p in range(n_pages):
            o_slc = o_slc + _bdot(p2[:, p * PAGE_SIZE:(p + 1) * PAGE_SIZE], pages[p][0, :, vcol])
        s3 = jnp.concatenate([_bdot_nt(qs, pastwin_ref[0, :, kcol]), _bdot_nt(qs, newwin_ref[0, :, kcol])], axis=1)
        p3 = _masked_softmax(s3 * SCALE, wmask)
        o_win = _bdot(p3[:, 0:past], pastwin_ref[0, :, vcol]) + _bdot(p3[:, past:], newwin_ref[0, :, vcol])
        for hg in range(NSA_GROUP):
            head = h0 + hg
            rows = slice(hg * tq, (hg + 1) * tq)
            c0 = 8 + 3 * head
            o_ref[0, :, head * 64:(head + 1) * 64] = (gates[:, c0:c0 + 1] * ocmp_ref[0, :, head * 64:(head + 1) * 64]
                                                      + gates[:, c0 + 1:c0 + 2] * o_slc[rows]
                                                      + gates[:, c0 + 2:c0 + 3] * o_win[rows])


def _nsa_sample_slc(page_table, q, small, o_cmp, sel, new_slc, past_win, past_win_base, new_win, cache, layer_base,
                    *, pos0):
    b, n_pages = page_table.shape
    nsp = sel.shape[-1]
    nk = (n_pages + 1) * PAGE_SIZE
    past = past_win.shape[1]
    kern = functools.partial(_nsa_sample_slc_kernel, n_pages=n_pages, pos0=pos0)
    per_b = lambda r, wd: pl.BlockSpec((1, r, wd), lambda bi, pt: (bi, 0, 0))

    def page_spec(i):
        return pl.BlockSpec((1, PAGE_SIZE, 256), lambda bi, pt: (layer_base + pt[bi, i], 0, 0))

    grid_spec = pltpu.PrefetchScalarGridSpec(
        num_scalar_prefetch=1,
        grid=(b,),
        in_specs=[per_b(SUBLANES, 256), per_b(SUBLANES, LANES), per_b(SUBLANES, 256),
                  pl.BlockSpec((1, NSA_KV_HEADS, SUBLANES, nsp), lambda bi, pt: (bi, 0, 0, 0)),
                  per_b(PAGE_SIZE, 256),
                  pl.BlockSpec((1, past, 256), lambda bi, pt: (past_win_base + bi, 0, 0)),
                  per_b(PAGE_SIZE, 256),
                  pl.BlockSpec((nsp, nk), lambda bi, pt: (0, 0))] + [page_spec(i) for i in range(n_pages)],
        out_specs=per_b(SUBLANES, 256),
        scratch_shapes=[pltpu.VMEM((2 * SUBLANES, nk), F32)],
    )
    return pl.pallas_call(
        kern,
        grid_spec=grid_spec,
        out_shape=jax.ShapeDtypeStruct((b, SUBLANES, 256), F32),
        compiler_params=_params("parallel"),
    )(page_table, q, small, o_cmp, sel, new_slc, past_win, new_win, _block_expander(nsp, nk), *([cache] * n_pages))


def _prep_layer(l, P):
    w_in = P["w_in"][l]
    col = lambda lo, hi: w_in[:, lo:hi]
    small = jnp.concatenate([col(1024, 1032), col(2568, 2580), jnp.zeros((D_MODEL, LANES - 20), F32)], axis=1)
    w_perm = jnp.concatenate([col(0, 1024), col(1032, 1544), col(1544, 1800), col(2580, 2836), col(2836, 3348),
                              col(1800, 2568), small], axis=1).astype(BF16)
    par = jnp.zeros((2, LANES), F32).at[0, 4:8].set(P["gdn_a_log"][l]).at[1, 4:8].set(P["gdn_dt_bias"][l])

    def block_diag(w):
        return jnp.einsum("ncd,nm->ncmd", w, jnp.eye(4, dtype=F32)).reshape(GROUP_WIDTH, GROUP_WIDTH).astype(BF16)

    eye2 = jnp.eye(2, dtype=F32)
    w1 = P["nsa_cmp_w1"][l]
    half = CMP_LEN // 2

    def expand_w1(w):
        return jnp.einsum("klde,kK,gG->lkgdKGe", w, eye2, eye2).reshape(half * 256, 256).astype(BF16)

    def expand_pe(pe):
        return jnp.broadcast_to(pe.transpose(1, 0, 2)[:, :, None, :], (half, 2, 2, HEAD_DIM)).reshape(1, half * 256)

    pe = P["nsa_cmp_pe"][l]
    w2 = jnp.einsum("kef,kK,gG->kgeKGf", P["nsa_cmp_w2"][l], eye2, eye2).reshape(256, 256).astype(BF16)
    return dict(
        norm1=P["norm1"][l][None], w_in=w_perm,
        gdn_cw=P["gdn_conv_w"][l], gdn_par=par, gdn_norm=P["gdn_norm"][l][None],
        lru_cw=P["lru_conv_w"][l],
        lru_vec=jnp.stack([P["lru_conv_b"][l], P["lru_ba"][l], P["lru_bx"][l], P["lru_lambda"][l]]),
        lru_wa=block_diag(P["lru_wa"][l]), lru_wx=block_diag(P["lru_wx"][l]),
        pea=expand_pe(pe[:, :half]), peb=expand_pe(pe[:, half:]),
        cmp_wa=expand_w1(w1[:, :half]), cmp_wb=expand_w1(w1[:, half:]), cmp_w2=w2,
        w_out=P["w_out"][l].astype(BF16), norm2=P["norm2"][l][None],
        wg=P["ffn_w_gate"][l].astype(BF16), wu=P["ffn_w_up"][l].astype(BF16),
        ffn_cw=P["ffn_conv_w"][l], wd=P["ffn_w_down"][l].astype(BF16),
    )


def _rope_tables(pos):
    half = HEAD_DIM // 2
    inv = ROPE_THETA ** (-jnp.arange(half, dtype=F32) / half)
    ang = pos.astype(F32)[:, None] * inv[None, :]
    c, s = jnp.cos(ang), jnp.sin(ang)
    return jnp.tile(jnp.concatenate([c, c], axis=-1), (1, 2)), jnp.tile(jnp.concatenate([-s, s], axis=-1), (1, 2))


PROMPT_TM = 256
PROMPT_TQ = 128
GDN_CHUNK = 64


def _layer_prompt(x2d, W, b, l, cos, sin, final, norm_f):
    outs = _in_proj(x2d, W["norm1"], W["w_in"], cos, sin, PROMPT_TM, l // PROMPT_TM)
    zg, zl, nq, sq, skv, cmp_r, slc_r, win_r, small = [o.reshape(b, l, o.shape[-1]) for o in outs]
    o_gdn, gdn_conv, gdn_s = _gdn(zg, small, jnp.zeros((b, SUBLANES, 768), F32), jnp.zeros((b, N_HEADS, 64, 64), F32),
                                  W["gdn_cw"], W["gdn_par"], W["gdn_norm"], tl=PROMPT_TM, chunk=GDN_CHUNK,
                                  l_real=PROMPT_TM)
    o_lru, lru_conv, lru_h = _lru(zl, jnp.zeros((b, SUBLANES, GROUP_WIDTH), F32), jnp.zeros((b, 1, GROUP_WIDTH), F32),
                                  W["lru_cw"], W["lru_vec"], W["lru_wa"], W["lru_wx"], tl=PROMPT_TM, l_real=PROMPT_TM)
    kcvc = _cmp_prompt(cmp_r.reshape(b, l // CMP_STRIDE, CMP_STRIDE * 256), W["pea"], W["peb"],
                       W["cmp_wa"], W["cmp_wb"], W["cmp_w2"])
    o_nsa = _nsa_prompt(nq, small, kcvc, slc_r, win_r, tq=PROMPT_TQ)
    o_sb = _sb_prompt(sq, skv, tq=PROMPT_TQ)
    mixes = [o.reshape(b * l, GROUP_WIDTH) for o in (o_gdn, o_lru, o_nsa, o_sb)]
    res = _out_ffn(x2d, mixes, W["w_out"], W["norm2"], W["wg"], W["wu"], W["ffn_cw"], W["wd"],
                   jnp.zeros((SUBLANES, D_FF), F32), norm_f, tm=PROMPT_TM, stride=1, tiles_per_seq=l // PROMPT_TM,
                   final=final)
    kv5 = lambda a, h: a.reshape(b, l, 2, h, HEAD_DIM)
    wkeep = min(WINDOW, l)
    states = (kv5(cmp_r, 2), kv5(slc_r, 2), kv5(skv, 4), kv5(win_r, 2)[:, l - wkeep:], gdn_conv, gdn_s, lru_conv,
              lru_h[:, 0], res[1][l // PROMPT_TM - 1::l // PROMPT_TM])
    return res[0], states, (res[2] if final else None)


def _layer_sample(x2d, W, layer, S, cos, sin, final, norm_f):
    page_table = S["page_table"]
    b, n_pages = page_table.shape
    t_new = x2d.shape[0] // b
    pos0 = n_pages * PAGE_SIZE
    n_phys = S["n_phys"]
    outs = _in_proj(x2d, W["norm1"], W["w_in"], cos, sin, x2d.shape[0], 1)

    def to_batch_major(a, rows=SUBLANES):
        a = a.reshape(t_new, b, a.shape[-1]).transpose(1, 0, 2)
        return jnp.pad(a, ((0, 0), (0, rows - t_new), (0, 0)))

    zg, zl, nq, sq, skv, cmp_r, slc_r, win_r, small = [to_batch_major(o) for o in outs]
    front = lambda st: jnp.pad(st, ((0, 0), (SUBLANES - st.shape[1], 0), (0, 0)))
    o_gdn, gdn_conv, gdn_s = _gdn(zg, small, front(S["gdn_conv"][layer]), S["gdn"][layer],
                                  W["gdn_cw"], W["gdn_par"], W["gdn_norm"], tl=SUBLANES, chunk=SUBLANES, l_real=t_new)
    o_lru, lru_conv, lru_h = _lru(zl, front(S["lru_conv"][layer]), S["lru"][layer][:, None, :],
                                  W["lru_cw"], W["lru_vec"], W["lru_wa"], W["lru_wx"], tl=SUBLANES, l_real=t_new)
    base = layer * n_phys
    o_cmp, sel = _nsa_sample_cmp(page_table, nq, W["pea"], W["peb"], W["cmp_wa"], W["cmp_wb"], W["cmp_w2"],
                                 S["cmp_pages"], base, pos0=pos0)
    pad_page = lambda a: jnp.pad(a, ((0, 0), (0, PAGE_SIZE - a.shape[1]), (0, 0)))
    o_nsa = _nsa_sample_slc(page_table, nq, small, o_cmp, sel, pad_page(slc_r), S["win"], layer * b, pad_page(win_r),
                            S["slc_pages"], base, pos0=pos0)
    head_of_col = jnp.arange(256) // HEAD_DIM
    head_of_row = jnp.arange(N_HEADS * SUBLANES) // SUBLANES
    qbd = jnp.where(head_of_row[:, None] == head_of_col[None, :], jnp.tile(sq, (1, N_HEADS, 1)), 0.0)
    o_sb = _sb_sample(page_table, qbd, pad_page(skv), S["sb_pages"], base, pg=min(32, n_pages), pos0=pos0)

    def to_time_major(a):
        return a[:, :t_new].transpose(1, 0, 2).reshape(t_new * b, a.shape[-1])

    mixes = [to_time_major(o) for o in (o_gdn, o_lru, o_nsa, o_sb)]
    buf = S["ffn_conv"][layer].transpose(1, 0, 2).reshape(2 * b, D_FF)
    res = _out_ffn(x2d, mixes, W["w_out"], W["norm2"], W["wg"], W["wu"], W["ffn_cw"], W["wd"], buf, norm_f,
                   tm=x2d.shape[0], stride=b, tiles_per_seq=1, final=final)
    kv5 = lambda a, h: a[:, :t_new].reshape(b, t_new, 2, h, HEAD_DIM)
    ffn_state = res[1].reshape(2, b, D_FF).transpose(1, 0, 2)
    states = (kv5(cmp_r, 2), kv5(slc_r, 2), kv5(skv, 4), kv5(win_r, 2), gdn_conv, gdn_s, lru_conv, lru_h[:, 0], ffn_state)
    return res[0], states, (res[2] if final else None)


def kernel(x_prompt, x_sample, cache_cmp_kv, cache_slc_kv, cache_sb_kv, cache_win_kv, state_gdn_conv, state_gdn, state_lru_conv, state_lru, state_ffn_conv, page_table, norm1, w_in, gdn_conv_w, gdn_a_log, gdn_dt_bias, gdn_norm, lru_conv_w, lru_conv_b, lru_wa, lru_ba, lru_wx, lru_bx, lru_lambda, nsa_cmp_pe, nsa_cmp_w1, nsa_cmp_w2, w_out, norm2, ffn_w_gate, ffn_w_up, ffn_conv_w, ffn_w_down, norm_f):
    P = dict(norm1=norm1, w_in=w_in, gdn_conv_w=gdn_conv_w, gdn_a_log=gdn_a_log, gdn_dt_bias=gdn_dt_bias,
             gdn_norm=gdn_norm, lru_conv_w=lru_conv_w, lru_conv_b=lru_conv_b, lru_wa=lru_wa, lru_ba=lru_ba,
             lru_wx=lru_wx, lru_bx=lru_bx, lru_lambda=lru_lambda, nsa_cmp_pe=nsa_cmp_pe, nsa_cmp_w1=nsa_cmp_w1,
             nsa_cmp_w2=nsa_cmp_w2, w_out=w_out, norm2=norm2, ffn_w_gate=ffn_w_gate, ffn_w_up=ffn_w_up,
             ffn_conv_w=ffn_conv_w, ffn_w_down=ffn_w_down)
    depth = w_in.shape[0]
    weights = [_prep_layer(l, P) for l in range(depth)]
    gf = norm_f[None]

    b, l, _ = x_prompt.shape
    cos, sin = _rope_tables(jnp.arange(l))
    h = x_prompt.reshape(b * l, D_MODEL)
    p_states, y_prompt = [], None
    for layer in range(depth):
        h, st, y = _layer_prompt(h, weights[layer], b, l, cos, sin, layer == depth - 1, gf)
        p_states.append(st)
        y_prompt = y
    y_prompt = y_prompt.reshape(b, l, D_MODEL)

    db, t_new, _ = x_sample.shape
    n_phys = cache_sb_kv.shape[1]
    n_pages = page_table.shape[1]
    pos0 = n_pages * PAGE_SIZE
    past_win = cache_win_kv.shape[2]
    S = dict(page_table=page_table, n_phys=n_phys,
             cmp_pages=cache_cmp_kv.reshape(depth * n_phys, PAGE_SIZE // CMP_STRIDE, CMP_STRIDE * 256),
             slc_pages=cache_slc_kv.reshape(depth * n_phys, PAGE_SIZE, 256),
             sb_pages=cache_sb_kv.reshape(depth * n_phys, PAGE_SIZE, 512),
             win=cache_win_kv.reshape(depth * db, past_win, 256),
             gdn_conv=state_gdn_conv, gdn=state_gdn, lru_conv=state_lru_conv, lru=state_lru, ffn_conv=state_ffn_conv)
    cos_s, sin_s = _rope_tables(pos0 + jnp.repeat(jnp.arange(t_new), db))
    h = x_sample.transpose(1, 0, 2).reshape(t_new * db, D_MODEL)
    s_states, y_sample = [], None
    for layer in range(depth):
        h, st, y = _layer_sample(h, weights[layer], layer, S, cos_s, sin_s, layer == depth - 1, gf)
        s_states.append(st)
        y_sample = y
    y_sample = y_sample.reshape(t_new, db, D_MODEL).transpose(1, 0, 2)

    stk = lambda states, i: jnp.stack([s[i] for s in states])
    return ((y_prompt, y_sample) + tuple(stk(p_states, i) for i in range(9))
            + tuple(stk(s_states, i) for i in range(9)))
```

```python
import functools
import math

import jax
import jax.numpy as jnp
from jax import lax
from jax.experimental import pallas as pl
from jax.experimental.pallas import tpu as pltpu

F32, BF16 = jnp.float32, jnp.bfloat16

D_MODEL = 1024
HEAD_DIM = 64
GROUP_WIDTH = 256
N_HEADS = 4
NSA_KV_HEADS = 2
NSA_GROUP = 2
D_FF = 2816
SHORT_CONV = 4
FFN_CONV = 3
LRU_C = 8.0
CMP_STRIDE = 16
CMP_LEN = 32
SLC_BLOCK = 64
SLC_TOPK = 16
N_LOCAL = 2
WINDOW = 512
FORCE_BONUS = 1.0e3
ROPE_THETA = 10000.0
EPS = 1e-6
NEG = -1e30
SCALE = HEAD_DIM ** -0.5
PAGE_SIZE = 128

VMEM_LIMIT_BYTES = 56 * 1024 * 1024
LANES = 128
SUBLANES = 8

IN_WIDTHS = (1024, 512, 256, 256, 512, 256, 256, 256, 128)
D_IN_PAD = sum(IN_WIDTHS)


def _params(*sem):
    return pltpu.CompilerParams(dimension_semantics=sem, vmem_limit_bytes=VMEM_LIMIT_BYTES)


def _bdot(a, b):
    return jnp.dot(a.astype(BF16), b.astype(BF16), preferred_element_type=F32)


def _bdot_nt(a, b):
    return lax.dot_general(a.astype(BF16), b.astype(BF16), (((1,), (1,)), ((), ())), preferred_element_type=F32)


def _bdot_tn(a, b):
    return lax.dot_general(a.astype(BF16), b.astype(BF16), (((0,), (0,)), ((), ())), preferred_element_type=F32)


def _fdot(a, b):
    return jnp.dot(a, b, preferred_element_type=F32, precision=lax.Precision.HIGHEST)


def _dot3(a, b):
    ah, bh = a.astype(BF16), b.astype(BF16)
    al, bl = (a - ah.astype(F32)).astype(BF16), (b - bh.astype(F32)).astype(BF16)
    dot = lambda x, y: jnp.dot(x, y, preferred_element_type=F32)
    return dot(ah, bh) + (dot(ah, bl) + dot(al, bh))


GDN_GROUP_CHUNKS = 2


def _split_dot(a, b_exact):
    hi = a.astype(BF16)
    lo = (a - hi.astype(F32)).astype(BF16)
    return (jnp.dot(hi, b_exact, preferred_element_type=F32) + jnp.dot(lo, b_exact, preferred_element_type=F32))


def _sigmoid(x):
    return jax.nn.sigmoid(x)


def _silu(x):
    return x * jax.nn.sigmoid(x)


def _softplus(x):
    return jnp.maximum(x, 0.0) + jnp.log1p(jnp.exp(-jnp.abs(x)))


def _gelu_tanh(x):
    return 0.5 * x * (1.0 + jnp.tanh(math.sqrt(2.0 / math.pi) * (x + 0.044715 * (x * x * x))))


def _rms(x, g):
    return x * lax.rsqrt(jnp.mean(x * x, axis=-1, keepdims=True) + EPS) * g


def _rope_pairs(v, cos, sin):
    lane = lax.broadcasted_iota(jnp.int32, v.shape, 1)
    first_half = (lane % HEAD_DIM) < (HEAD_DIM // 2)
    partner = jnp.where(first_half, pltpu.roll(v, LANES - HEAD_DIM // 2, 1), pltpu.roll(v, HEAD_DIM // 2, 1))
    return v * cos + partner * sin


def _masked_softmax(s, mask):
    sm = jnp.where(mask, s, NEG)
    m = jnp.max(sm, axis=-1, keepdims=True)
    e = jnp.where(mask, jnp.exp(sm - m), 0.0)
    den = jnp.sum(e, axis=-1, keepdims=True)
    return e * (1.0 / jnp.where(den > 0.0, den, 1.0))


def _in_proj_kernel(x_ref, g_ref, w_ref, cos_ref, sin_ref,
                    zg_ref, zl_ref, nq_ref, sq_ref, skv_ref, cmp_ref, slc_ref, win_ref, sm_ref):
    xb = _rms(x_ref[...], g_ref[...]).astype(BF16)
    cos, sin = cos_ref[...], sin_ref[...]

    def mm(lo, width):
        return jnp.dot(xb, w_ref[:, lo:lo + width], preferred_element_type=F32)

    zg_ref[...] = mm(0, 1024)
    zl_ref[...] = mm(1024, 512)
    q = mm(1536, 256)
    nq_ref[:, 0:128] = _rope_pairs(q[:, 0:128], cos, sin)
    nq_ref[:, 128:256] = _rope_pairs(q[:, 128:256], cos, sin)
    sq_ref[...] = mm(1792, 256)
    skv_ref[...] = mm(2048, 512)
    for ref, lo in ((cmp_ref, 2560), (slc_ref, 2816), (win_ref, 3072)):
        kv = mm(lo, 256)
        ref[:, 0:128] = _rope_pairs(kv[:, 0:128], cos, sin)
        ref[:, 128:256] = kv[:, 128:256]
    sm_ref[...] = mm(3328, 128)


def _in_proj(x2d, g, w, cos, sin, tm, table_tiles):
    n = x2d.shape[0]
    nt = n // tm
    row = lambda wd: pl.BlockSpec((tm, wd), lambda i: (i, 0))
    tab = pl.BlockSpec((tm, LANES), lambda i: (i % table_tiles, 0))
    return pl.pallas_call(
        _in_proj_kernel,
        grid=(nt,),
        in_specs=[row(D_MODEL), pl.BlockSpec((1, D_MODEL), lambda i: (0, 0)),
                  pl.BlockSpec((D_MODEL, D_IN_PAD), lambda i: (0, 0)), tab, tab],
        out_specs=[row(wd) for wd in IN_WIDTHS],
        out_shape=[jax.ShapeDtypeStruct((n, wd), F32) for wd in IN_WIDTHS],
        compiler_params=_params("parallel"),
    )(x2d, g, w, cos, sin)


def _rope_rows(v, cos_t, sin_t):
    q = HEAD_DIM // 2
    partner = jnp.concatenate([v[q:2 * q], v[0:q], v[3 * q:4 * q], v[2 * q:3 * q]], axis=0)
    return v * cos_t + partner * sin_t


KV_T_ROWS = 1280


def _in_proj_t_kernel(x_ref, g_ref, w_ref, wt_ref, cos_ref, sin_ref, cost_ref, sint_ref,
                      zg_ref, zl_ref, nq_ref, sq_ref, cmp_ref, sm_ref, skvt_ref, cmpt_ref, slct_ref, wint_ref):
    xb = _rms(x_ref[...], g_ref[...]).astype(BF16)
    cos, sin = cos_ref[...], sin_ref[...]
    cos_t, sin_t = cost_ref[...], sint_ref[...]

    def mm(lo, width):
        return jnp.dot(xb, w_ref[:, lo:lo + width], preferred_element_type=F32)

    def mm_t(lo, rows):
        return lax.dot_general(wt_ref[lo:lo + rows, :], xb, (((1,), (1,)), ((), ())), preferred_element_type=F32)

    zg_ref[...] = mm(0, 1024)
    zl_ref[...] = mm(1024, 512)
    q = mm(1536, 256)
    nq_ref[:, 0:128] = _rope_pairs(q[:, 0:128], cos, sin)
    nq_ref[:, 128:256] = _rope_pairs(q[:, 128:256], cos, sin)
    sq_ref[...] = mm(1792, 256)
    kv = mm(2560, 256)
    cmp_ref[:, 0:128] = _rope_pairs(kv[:, 0:128], cos, sin)
    cmp_ref[:, 128:256] = kv[:, 128:256]
    sm_ref[...] = mm(3328, 128)
    skvt_ref[0] = mm_t(0, 512)
    for ref, lo in ((cmpt_ref, 512), (slct_ref, 768), (wint_ref, 1024)):
        kvt = mm_t(lo, 256)
        ref[0, 0:128, :] = _rope_rows(kvt[0:128], cos_t, sin_t)
        ref[0, 128:256, :] = kvt[128:256]


def _in_proj_t(x2d, g, w, wt, cos, sin, b, l, tm):
    tps = l // tm
    row = lambda wd: pl.BlockSpec((tm, wd), lambda i: (i, 0))
    tab = pl.BlockSpec((tm, LANES), lambda i: (i % tps, 0))
    tab_t = pl.BlockSpec((LANES, tm), lambda i: (0, i % tps))
    kvt = lambda c: pl.BlockSpec((1, c, tm), lambda i: (i // tps, 0, i % tps))
    row_widths = (1024, 512, 256, 256, 256, 128)
    t_rows = (512, 256, 256, 256)
    return pl.pallas_call(
        _in_proj_t_kernel,
        grid=(b * tps,),
        in_specs=[row(D_MODEL), pl.BlockSpec((1, D_MODEL), lambda i: (0, 0)),
                  pl.BlockSpec((D_MODEL, D_IN_PAD), lambda i: (0, 0)),
                  pl.BlockSpec((KV_T_ROWS, D_MODEL), lambda i: (0, 0)), tab, tab, tab_t, tab_t],
        out_specs=[row(wd) for wd in row_widths] + [kvt(c) for c in t_rows],
        out_shape=[jax.ShapeDtypeStruct((b * l, wd), F32) for wd in row_widths]
                  + [jax.ShapeDtypeStruct((b, c, l), F32) for c in t_rows],
        compiler_params=_params("parallel"),
    )(x2d, g, w, wt, cos, sin, cos.T, sin.T)


def _out_ffn_kernel(*refs, tm, tf, nf, stride, prev_rows, tiles_per_seq, final):
    x_ref, mix_refs, refs = refs[0], refs[1:5], refs[5:]
    if final:
        (wo_ref, g2_ref, wg_ref, wu_ref, cw_ref, wd_ref, buf_ref, gf_ref,
         xo_ref, st_ref, y_ref, xn_s, x1_s, acc_s, carry_s) = refs
    else:
        (wo_ref, g2_ref, wg_ref, wu_ref, cw_ref, wd_ref, buf_ref,
         xo_ref, st_ref, xn_s, x1_s, acc_s, carry_s) = refs
    i, j = pl.program_id(0), pl.program_id(1)

    @pl.when(j == 0)
    def _():
        mix = jnp.concatenate([m[...] for m in mix_refs], axis=1)
        x1 = x_ref[...] + _bdot(mix, wo_ref[...])
        x1_s[...] = x1
        xn_s[...] = _rms(x1, g2_ref[...]).astype(BF16)
        acc_s[...] = jnp.zeros_like(acc_s)

    xn = xn_s[...]
    a = jnp.dot(xn, wg_ref[...], preferred_element_type=F32)
    u = jnp.dot(xn, wu_ref[...], preferred_element_type=F32)
    seq_start = (i % tiles_per_seq) == 0
    prev = jnp.where(seq_start, buf_ref[...], carry_s[j])
    ext = jnp.concatenate([prev, a], axis=0)
    cw = cw_ref[...]
    p, s = prev_rows, stride
    ac = cw[0:1] * ext[p - 2 * s:p - 2 * s + tm] + cw[1:2] * ext[p - s:p - s + tm] + cw[2:3] * ext[p:p + tm]
    carry_s[j] = ext[tm:tm + p]
    st_ref[0] = ext[tm + p - 2 * s:tm + p]
    h = _silu(ac) * u
    acc_s[...] += _bdot(h, wd_ref[...])

    @pl.when(j == nf - 1)
    def _():
        x2 = x1_s[...] + acc_s[...]
        xo_ref[...] = x2
        if final:
            y_ref[...] = _rms(x2, gf_ref[...])


def _out_ffn(x2d, mixes, wo, g2, wg, wu, cw, wd, buf, gf, *, tm, stride, tiles_per_seq, final):
    n = x2d.shape[0]
    nt = n // tm
    nf = 2
    tf = D_FF // nf
    prev_rows = buf.shape[0]
    kern = functools.partial(_out_ffn_kernel, tm=tm, tf=tf, nf=nf, stride=stride, prev_rows=prev_rows,
                             tiles_per_seq=tiles_per_seq, final=final)
    tok = pl.BlockSpec((tm, D_MODEL), lambda i, j: (i, 0))
    part = pl.BlockSpec((tm, GROUP_WIDTH), lambda i, j: (i, 0))
    in_specs = [tok, part, part, part, part,
                pl.BlockSpec((D_MODEL, D_MODEL), lambda i, j: (0, 0)),
                pl.BlockSpec((1, D_MODEL), lambda i, j: (0, 0)),
                pl.BlockSpec((D_MODEL, tf), lambda i, j: (0, j)),
                pl.BlockSpec((D_MODEL, tf), lambda i, j: (0, j)),
                pl.BlockSpec((FFN_CONV, tf), lambda i, j: (0, j)),
                pl.BlockSpec((tf, D_MODEL), lambda i, j: (j, 0)),
                pl.BlockSpec((prev_rows, tf), lambda i, j: (0, j))]
    args = [x2d, *mixes, wo, g2, wg, wu, cw, wd, buf]
    out_specs = [tok, pl.BlockSpec((1, 2 * stride, tf), lambda i, j: (i, 0, j))]
    out_shape = [jax.ShapeDtypeStruct((n, D_MODEL), F32),
                 jax.ShapeDtypeStruct((nt, 2 * stride, D_FF), F32)]
    if final:
        in_specs.append(pl.BlockSpec((1, D_MODEL), lambda i, j: (0, 0)))
        args.append(gf)
        out_specs.append(tok)
        out_shape.append(jax.ShapeDtypeStruct((n, D_MODEL), F32))
    return pl.pallas_call(
        kern,
        grid=(nt, nf),
        in_specs=in_specs,
        out_specs=out_specs,
        out_shape=out_shape,
        scratch_shapes=[pltpu.VMEM((tm, D_MODEL), BF16), pltpu.VMEM((tm, D_MODEL), F32),
                        pltpu.VMEM((tm, D_MODEL), F32), pltpu.VMEM((nf, prev_rows, tf), F32)],
        compiler_params=_params("arbitrary", "arbitrary"),
    )(*args)


def _short_conv(x, carry_ref, cw, tl):
    ext = jnp.concatenate([carry_ref[...], x], axis=0)
    y = cw[0:1] * ext[5:5 + tl]
    for i in range(1, SHORT_CONV):
        y = y + cw[i:i + 1] * ext[5 + i:5 + i + tl]
    carry_ref[...] = ext[tl:tl + SUBLANES]
    return y, ext


def _gdn_kernel(zg_ref, sm_ref, cst_ref, s0_ref, cw_ref, par_ref, nrm_ref,
                o_ref, cout_ref, sout_ref, carry_s, state_s, *, tl, chunk, l_real):
    t = pl.program_id(1)

    @pl.when(t == 0)
    def _():
        carry_s[...] = cst_ref[0]
        state_s[...] = s0_ref[0]

    z = zg_ref[0]
    gate = z[:, 768:1024]
    y, ext = _short_conv(z[:, 0:768], carry_s, cw_ref[...], tl)
    cout_ref[0] = ext[SUBLANES + l_real - 3:SUBLANES + l_real]
    y = _silu(y)

    small = sm_ref[0]
    par = par_ref[...]
    beta_c = _sigmoid(small)
    g_c = -jnp.exp(par[0:1]) * _softplus(small + par[1:2])
    if l_real < tl:
        real = lax.broadcasted_iota(jnp.int32, small.shape, 0) < l_real
        beta_c = jnp.where(real, beta_c, 0.0)
        g_c = jnp.where(real, g_c, 0.0)
    ri = lax.broadcasted_iota(jnp.int32, (tl, tl), 0)
    ci = lax.broadcasted_iota(jnp.int32, (tl, tl), 1)
    same = (ri // chunk) == (ci // chunk)
    gc = _fdot(jnp.where(same & (ci <= ri), 1.0, 0.0), g_c)
    grest = _fdot(jnp.where(same & (ci > ri), 1.0, 0.0), g_c)
    gc_t = gc.T

    ii = lax.broadcasted_iota(jnp.int32, (chunk, chunk), 0)
    jj = lax.broadcasted_iota(jnp.int32, (chunk, chunk), 1)
    causal, strict = ii >= jj, ii > jj
    eye = jnp.where(ii == jj, 1.0, 0.0)
    gnorm = nrm_ref[...]
    n_chunks = tl // chunk
    heads = range(N_HEADS)
    S = [state_s[h] for h in heads]
    out_rows = []
    for c0 in range(0, n_chunks, GDN_GROUP_CHUNKS):
        pairs = [(c, h) for c in range(c0, min(c0 + GDN_GROUP_CHUNKS, n_chunks)) for h in heads]
        rows_of = lambda c: slice(c * chunk, (c + 1) * chunk)
        q = [y[rows_of(c), h * 64:(h + 1) * 64] for c, h in pairs]
        k = [y[rows_of(c), 256 + h * 64:256 + (h + 1) * 64] for c, h in pairs]
        v = [y[rows_of(c), 512 + h * 64:512 + (h + 1) * 64] for c, h in pairs]
        q = [x * lax.rsqrt(jnp.sum(x * x, axis=-1, keepdims=True) + EPS) * SCALE for x in q]
        k = [x * lax.rsqrt(jnp.sum(x * x, axis=-1, keepdims=True) + EPS) for x in k]
        beta = [beta_c[rows_of(c), h:h + 1] for c, h in pairs]
        gcc = [gc[rows_of(c), 4 + h:5 + h] for c, h in pairs]
        gcr = [gc_t[4 + h:5 + h, rows_of(c)] for c, h in pairs]
        decay = [jnp.where(causal, jnp.exp(jnp.where(causal, a - b, 0.0)), 0.0) for a, b in zip(gcc, gcr)]
        kb = [a * b for a, b in zip(k, beta)]
        A = [jnp.where(strict, _bdot_nt(a, b) * d, 0.0) for a, b, d in zip(kb, k, decay)]
        T = [eye - a for a in A]
        P = A
        span = 2
        while span < chunk:
            P = [_dot3(p, p) for p in P]
            T = [t + _dot3(t, p) for t, p in zip(T, P)]
            span *= 2
        u = [_bdot(t, a * b) for t, a, b in zip(T, v, beta)]
        w = [_bdot(t, a * jnp.exp(g)) for t, a, g in zip(T, kb, gcc)]
        attn = [_bdot_nt(a, b) * d for a, b, d in zip(q, k, decay)]
        qe = [a * jnp.exp(g) for a, g in zip(q, gcc)]
        kd = [a * jnp.exp(grest[rows_of(c), 4 + h:5 + h]) for a, (c, h) in zip(k, pairs)]
        for i0 in range(0, len(pairs), N_HEADS):
            c = pairs[i0][0]
            v_new = [u[i0 + h] - _bdot(w[i0 + h], S[h]) for h in heads]
            o = [_bdot(qe[i0 + h], S[h]) + _bdot(attn[i0 + h], v_new[h]) for h in heads]
            last = (c + 1) * chunk - 1
            S = [S[h] * jnp.exp(gc[last:last + 1, 4 + h:5 + h]) + _bdot_tn(kd[i0 + h], v_new[h]) for h in heads]
            o = [_rms(o[h], gnorm) * _silu(gate[rows_of(c), h * 64:(h + 1) * 64]) for h in heads]
            out_rows.append(jnp.concatenate(o, axis=1))
    for h in heads:
        state_s[h] = S[h]
        sout_ref[0, h] = S[h]
    o_ref[0] = jnp.concatenate(out_rows, axis=0)


def _gdn(zg, small, conv_state8, s0, cw, par, gnorm, *, tl, chunk, l_real):
    b, l, _ = zg.shape
    nt = l // tl
    kern = functools.partial(_gdn_kernel, tl=tl, chunk=chunk, l_real=l_real)
    return pl.pallas_call(
        kern,
        grid=(b, nt),
        in_specs=[pl.BlockSpec((1, tl, 1024), lambda i, t: (i, t, 0)),
                  pl.BlockSpec((1, tl, LANES), lambda i, t: (i, t, 0)),
                  pl.BlockSpec((1, SUBLANES, 768), lambda i, t: (i, 0, 0)),
                  pl.BlockSpec((1, N_HEADS, 64, 64), lambda i, t: (i, 0, 0, 0)),
                  pl.BlockSpec((SHORT_CONV, 768), lambda i, t: (0, 0)),
                  pl.BlockSpec((2, LANES), lambda i, t: (0, 0)),
                  pl.BlockSpec((1, HEAD_DIM), lambda i, t: (0, 0))],
        out_specs=[pl.BlockSpec((1, tl, GROUP_WIDTH), lambda i, t: (i, t, 0)),
                   pl.BlockSpec((1, 3, 768), lambda i, t: (i, 0, 0)),
                   pl.BlockSpec((1, N_HEADS, 64, 64), lambda i, t: (i, 0, 0, 0))],
        out_shape=[jax.ShapeDtypeStruct((b, l, GROUP_WIDTH), F32),
                   jax.ShapeDtypeStruct((b, 3, 768), F32),
                   jax.ShapeDtypeStruct((b, N_HEADS, 64, 64), F32)],
        scratch_shapes=[pltpu.VMEM((SUBLANES, 768), F32), pltpu.VMEM((N_HEADS, 64, 64), F32)],
        compiler_params=_params("parallel", "arbitrary"),
    )(zg, small, conv_state8, s0, cw, par, gnorm)


def _lru_kernel(zl_ref, cst_ref, h0_ref, cw_ref, vec_ref, wa_ref, wx_ref,
                y_ref, cout_ref, hout_ref, carry_s, h_s, *, tl, l_real):
    t = pl.program_id(1)

    @pl.when(t == 0)
    def _():
        carry_s[...] = cst_ref[0]
        h_s[...] = h0_ref[0]

    z = zl_ref[0]
    gate = z[:, 256:512]
    vec = vec_ref[...]
    xc, ext = _short_conv(z[:, 0:256], carry_s, cw_ref[...], tl)
    cout_ref[0] = ext[SUBLANES + l_real - 3:SUBLANES + l_real]
    xc = xc + vec[0:1]
    r = _sigmoid(_bdot(xc, wa_ref[...]) + vec[1:2])
    i = _sigmoid(_bdot(xc, wx_ref[...]) + vec[2:3])
    log_a = -LRU_C * r * _softplus(-vec[3:4])
    a = jnp.exp(log_a)
    u = jnp.sqrt(1.0 - a * a) * (i * xc)
    rowid = lax.broadcasted_iota(jnp.int32, u.shape, 0)
    u = u + jnp.where(rowid == 0, a * h_s[...], 0.0)
    s = 1
    while s < tl:
        a_sh = jnp.concatenate([jnp.ones((s, GROUP_WIDTH), F32), a[:tl - s]], axis=0)
        u_sh = jnp.concatenate([jnp.zeros((s, GROUP_WIDTH), F32), u[:tl - s]], axis=0)
        u = a * u_sh + u
        a = a * a_sh
        s *= 2
    h_last = u[l_real - 1:l_real]
    h_s[...] = h_last
    hout_ref[0] = h_last
    y_ref[0] = u * _gelu_tanh(gate)


def _lru(zl, conv_state8, h0, cw, vec, wa, wx, *, tl, l_real):
    b, l, _ = zl.shape
    nt = l // tl
    kern = functools.partial(_lru_kernel, tl=tl, l_real=l_real)
    return pl.pallas_call(
        kern,
        grid=(b, nt),
        in_specs=[pl.BlockSpec((1, tl, 512), lambda i, t: (i, t, 0)),
                  pl.BlockSpec((1, SUBLANES, GROUP_WIDTH), lambda i, t: (i, 0, 0)),
                  pl.BlockSpec((1, 1, GROUP_WIDTH), lambda i, t: (i, 0, 0)),
                  pl.BlockSpec((SHORT_CONV, GROUP_WIDTH), lambda i, t: (0, 0)),
                  pl.BlockSpec((4, GROUP_WIDTH), lambda i, t: (0, 0)),
                  pl.BlockSpec((GROUP_WIDTH, GROUP_WIDTH), lambda i, t: (0, 0)),
                  pl.BlockSpec((GROUP_WIDTH, GROUP_WIDTH), lambda i, t: (0, 0))],
        out_specs=[pl.BlockSpec((1, tl, GROUP_WIDTH), lambda i, t: (i, t, 0)),
                   pl.BlockSpec((1, 3, GROUP_WIDTH), lambda i, t: (i, 0, 0)),
                   pl.BlockSpec((1, 1, GROUP_WIDTH), lambda i, t: (i, 0, 0))],
        out_shape=[jax.ShapeDtypeStruct((b, l, GROUP_WIDTH), F32),
                   jax.ShapeDtypeStruct((b, 3, GROUP_WIDTH), F32),
                   jax.ShapeDtypeStruct((b, 1, GROUP_WIDTH), F32)],
        scratch_shapes=[pltpu.VMEM((SUBLANES, GROUP_WIDTH), F32), pltpu.VMEM((1, GROUP_WIDTH), F32)],
        compiler_params=_params("parallel", "arbitrary"),
    )(zl, conv_state8, h0, cw, vec, wa, wx)


def _compress_tokens(x, pea, peb, wa, wb, w2):
    pa = _bdot(x + pea, wa)
    pb = _bdot(x + peb, wb)
    return pa, pb


def _compress_finish(pa, pb, w2):
    pb_next = jnp.concatenate([pb[1:], jnp.zeros((1, pb.shape[1]), F32)], axis=0)
    return _bdot(_silu(pa + pb_next), w2)


def _cmp_prompt_kernel(x_ref, pea_ref, peb_ref, wa_ref, wb_ref, w2_ref, o_ref):
    pa, pb = _compress_tokens(x_ref[0], pea_ref[...], peb_ref[...], wa_ref[...], wb_ref[...], None)
    o_ref[0] = _compress_finish(pa, pb, w2_ref[...])


def _cmp_prompt(xc, pea, peb, wa, wb, w2):
    b, m, f = xc.shape
    full = lambda a: pl.BlockSpec(a.shape, lambda i: (0,) * a.ndim)
    return pl.pallas_call(
        _cmp_prompt_kernel,
        grid=(b,),
        in_specs=[pl.BlockSpec((1, m, f), lambda i: (i, 0, 0)), full(pea), full(peb), full(wa), full(wb), full(w2)],
        out_specs=pl.BlockSpec((1, m, 256), lambda i: (i, 0, 0)),
        out_shape=jax.ShapeDtypeStruct((b, m, 256), F32),
        compiler_params=_params("parallel"),
    )(xc, pea, peb, wa, wb, w2)


def _overlap_matrix(m, nsp, nc, ns):
    nn = lax.broadcasted_iota(jnp.int32, (m, nsp), 0)
    mm = lax.broadcasted_iota(jnp.int32, (m, nsp), 1)
    ov = (jnp.minimum(nn * CMP_STRIDE + CMP_LEN - 1, mm * SLC_BLOCK + SLC_BLOCK - 1)
          - jnp.maximum(nn * CMP_STRIDE, mm * SLC_BLOCK) + 1)
    return jnp.where((nn < nc) & (mm < ns), jnp.maximum(ov, 0).astype(F32) * (1.0 / CMP_LEN), 0.0)


def _select_blocks(imp, qpos, ns):
    rows, nsp = imp.shape
    j = lax.broadcasted_iota(jnp.int32, (rows, nsp), 1)
    cur = qpos // SLC_BLOCK
    valid = j <= cur
    forced = valid & ((j == 0) | (j > cur - N_LOCAL))
    score = jnp.where(valid, imp + jnp.where(forced, FORCE_BONUS, 0.0), NEG)
    score = jnp.where(j < ns, score, -3.0e38)
    cnt = jnp.zeros((rows, nsp), jnp.int32)
    for i in range(ns):
        si = score[:, i:i + 1]
        beats = (si > score) | ((si == score) & (i < j))
        cnt = cnt + beats.astype(jnp.int32)
    return jnp.where(cnt < min(SLC_TOPK, ns), 1.0, 0.0)


NSA_KEY_CHUNK = 512


def _nsa_prompt_kernel(q_ref, sm_ref, kc_ref, slc_ref, win_ref, e_ref, o_ref, *, tq, l, nc, ns, wk, nsp, kchunk):
    q0 = pl.program_id(1) * tq
    q = q_ref[0]
    gates = _sigmoid(sm_ref[0])
    kcvc = kc_ref[0]
    m = kcvc.shape[0]
    qpos = q0 + lax.broadcasted_iota(jnp.int32, (tq, 1), 0)
    qpos2 = jnp.concatenate([qpos, qpos], axis=0)
    n_id = lax.broadcasted_iota(jnp.int32, (2 * tq, m), 1)
    cmask = (n_id * CMP_STRIDE + CMP_LEN - 1 <= qpos2) & (n_id < nc)
    ovl = _overlap_matrix(m, nsp, nc, ns)
    start = pl.multiple_of(jnp.clip(q0 + tq - wk, 0, l - wk), LANES)
    dwin = qpos2 - (start + lax.broadcasted_iota(jnp.int32, (2 * tq, wk), 1))
    wmask = (dwin >= 0) & (dwin < WINDOW)
    n_need = (q0 + tq + kchunk - 1) // kchunk
    for g in range(NSA_KV_HEADS):
        h0 = g * NSA_GROUP
        qs = jnp.concatenate([q[:, h0 * 64:(h0 + 1) * 64], q[:, (h0 + 1) * 64:(h0 + 2) * 64]], axis=0)
        krow = slice(g * 64, (g + 1) * 64)
        vrow = slice(128 + g * 64, 128 + (g + 1) * 64)
        pc = _masked_softmax(_bdot_nt(qs, kcvc[:, krow]) * SCALE, cmask)
        o_cmp = _bdot(pc, kcvc[:, vrow])
        sel = _select_blocks(_fdot(pc[:tq] + pc[tq:], ovl), qpos, ns).astype(BF16)
        p3 = _masked_softmax(_bdot(qs, win_ref[0, krow, pl.ds(start, wk)]) * SCALE, wmask)
        o_win = _bdot_nt(p3, win_ref[0, vrow, pl.ds(start, wk)])
        part = [gates[:, 8 + 3 * (h0 + hg):9 + 3 * (h0 + hg)] * o_cmp[hg * tq:(hg + 1) * tq]
                + gates[:, 10 + 3 * (h0 + hg):11 + 3 * (h0 + hg)] * o_win[hg * tq:(hg + 1) * tq]
                for hg in range(NSA_GROUP)]
        for nkc in range(1, l // kchunk + 1):

            @pl.when(n_need == nkc)
            def _(nk=nkc * kchunk, g=g, h0=h0, qs=qs, sel=sel, part=part, krow=krow, vrow=vrow):
                selk = jnp.dot(sel, e_ref[:, 0:nk], preferred_element_type=F32)
                smask = (selk > 0.5) & (lax.broadcasted_iota(jnp.int32, (tq, nk), 1) <= qpos)
                p2 = _masked_softmax(_bdot(qs, slc_ref[0, krow, 0:nk]) * SCALE,
                                     jnp.concatenate([smask, smask], axis=0))
                o_slc = _bdot_nt(p2, slc_ref[0, vrow, 0:nk])
                for hg in range(NSA_GROUP):
                    head = h0 + hg
                    c1 = 9 + 3 * head
                    o_ref[0, :, head * 64:(head + 1) * 64] = (part[hg]
                                                              + gates[:, c1:c1 + 1] * o_slc[hg * tq:(hg + 1) * tq])


def _block_expander(nsp, nkeys):
    blk = lax.broadcasted_iota(jnp.int32, (nsp, nkeys), 0)
    key = lax.broadcasted_iota(jnp.int32, (nsp, nkeys), 1)
    return jnp.where(key // SLC_BLOCK == blk, 1.0, 0.0).astype(BF16)


def _nsa_prompt(q, small, kcvc, slc, win, *, tq):
    b, l, _ = q.shape
    m = kcvc.shape[1]
    nc = m - 1
    ns = -(-l // SLC_BLOCK)
    nsp = LANES
    wk = min(WINDOW + tq, l)
    kern = functools.partial(_nsa_prompt_kernel, tq=tq, l=l, nc=nc, ns=ns, wk=wk, nsp=nsp, kchunk=min(NSA_KEY_CHUNK, l))
    seq = lambda rows: pl.BlockSpec((1, rows, l), lambda i, t: (i, 0, 0))
    tile = lambda wd: pl.BlockSpec((1, tq, wd), lambda i, t: (i, t, 0))
    return pl.pallas_call(
        kern,
        grid=(b, l // tq),
        in_specs=[tile(256), tile(LANES), pl.BlockSpec((1, m, 256), lambda i, t: (i, 0, 0)), seq(256), seq(256),
                  pl.BlockSpec((nsp, l), lambda i, t: (0, 0))],
        out_specs=tile(256),
        out_shape=jax.ShapeDtypeStruct((b, l, 256), F32),
        compiler_params=_params("parallel", "arbitrary"),
    )(q, small, kcvc, slc, win, _block_expander(nsp, l))


def _later_keys_matrix(n):
    a = lax.broadcasted_iota(jnp.int32, (n, n), 0)
    c = lax.broadcasted_iota(jnp.int32, (n, n), 1)
    return jnp.where(a > c, 1.0, 0.0).astype(BF16)


def _sb_block(z, mask, later, aft_prev):
    sp = jnp.maximum(z, 0.0) + jnp.log(1.0 + jnp.exp(-jnp.abs(z)))
    lf = -sp if mask is None else jnp.where(mask, -sp, 0.0)
    aft = _split_dot(lf, later) + aft_prev
    a = jnp.exp(z - sp + aft)
    if mask is not None:
        a = jnp.where(mask, a, 0.0)
    return a, aft_prev + jnp.sum(lf, axis=-1, keepdims=True)


def _sb_prompt_kernel(q_ref, kv_ref, later_ref, o_ref, acc_s, aft_s, *, tq):
    qi = pl.program_id(1)
    q = q_ref[0] * SCALE
    later = later_ref[...]
    acc_s[...] = jnp.zeros_like(acc_s)
    aft_s[...] = jnp.zeros_like(aft_s)

    def block(k0, mask):
        for h in range(N_HEADS):
            kt = kv_ref[0, h * 64:(h + 1) * 64, pl.ds(k0, tq)]
            vt = kv_ref[0, 256 + h * 64:256 + (h + 1) * 64, pl.ds(k0, tq)]
            z = _bdot(q[:, h * 64:(h + 1) * 64], kt)
            a, aft = _sb_block(z, mask, later, aft_s[h])
            acc_s[:, h * 64:(h + 1) * 64] += _bdot_nt(a, vt)
            aft_s[h] = aft

    block(pl.multiple_of(qi * tq, tq),
          lax.broadcasted_iota(jnp.int32, (tq, tq), 1) < lax.broadcasted_iota(jnp.int32, (tq, tq), 0))

    def body(it, carry):
        block(pl.multiple_of((qi - 1 - it) * tq, tq), None)
        return carry

    lax.fori_loop(0, qi, body, 0)
    o_ref[0] = acc_s[...]


def _sb_prompt(q, kv, *, tq):
    b, l, _ = q.shape
    kern = functools.partial(_sb_prompt_kernel, tq=tq)
    return pl.pallas_call(
        kern,
        grid=(b, l // tq),
        in_specs=[pl.BlockSpec((1, tq, 256), lambda i, t: (i, t, 0)),
                  pl.BlockSpec((1, 512, l), lambda i, t: (i, 0, 0)),
                  pl.BlockSpec((tq, tq), lambda i, t: (0, 0))],
        out_specs=pl.BlockSpec((1, tq, 256), lambda i, t: (i, t, 0)),
        out_shape=jax.ShapeDtypeStruct((b, l, 256), F32),
        scratch_shapes=[pltpu.VMEM((tq, 256), F32), pltpu.VMEM((N_HEADS, tq, 1), F32)],
        compiler_params=_params("parallel", "arbitrary"),
    )(q, kv, _later_keys_matrix(tq))


def _sb_sample_kernel(pt_ref, qbd_ref, new_ref, later_ref, *rest, pg, n_pg, pos0):
    pages = rest[:pg]
    o_ref, acc_s, aft_s = rest[pg:]
    g = pl.program_id(1)
    qbd = qbd_ref[0]
    later = later_ref[...]
    rows = qbd.shape[0]

    def block(kv_t, mask):
        z = _bdot(qbd, kv_t[0:256]) * SCALE
        a, aft = _sb_block(z, mask, later, aft_s[...])
        acc_s[...] += _bdot_nt(a, kv_t[256:512])
        aft_s[...] = aft

    @pl.when(g == 0)
    def _():
        acc_s[...] = jnp.zeros_like(acc_s)
        aft_s[...] = jnp.zeros_like(aft_s)
        t = lax.broadcasted_iota(jnp.int32, (rows, PAGE_SIZE), 0) % SUBLANES
        key = lax.broadcasted_iota(jnp.int32, (rows, PAGE_SIZE), 1)
        block(new_ref[0], key < t)

    for i in range(pg):
        block(pages[i][0], None)

    @pl.when(g == n_pg - 1)
    def _():
        acc = acc_s[...]
        lane_head = lax.broadcasted_iota(jnp.int32, (SUBLANES, 256), 1) // HEAD_DIM
        out = jnp.zeros((SUBLANES, 256), F32)
        for h in range(N_HEADS):
            out = out + jnp.where(lane_head == h, acc[h * SUBLANES:(h + 1) * SUBLANES], 0.0)
        o_ref[0] = out


def _sb_sample(page_table, qbd, new_rows, cache, layer_base, *, pg, pos0):
    b, n_pages = page_table.shape
    n_pg = n_pages // pg
    kern = functools.partial(_sb_sample_kernel, pg=pg, n_pg=n_pg, pos0=pos0)

    def page_spec(i):
        return pl.BlockSpec((1, 512, PAGE_SIZE),
                            lambda bi, g, pt: (layer_base + pt[bi, (n_pg - 1 - g) * pg + (pg - 1 - i)], 0, 0))

    grid_spec = pltpu.PrefetchScalarGridSpec(
        num_scalar_prefetch=1,
        grid=(b, n_pg),
        in_specs=[pl.BlockSpec((1, 32, 256), lambda bi, g, pt: (bi, 0, 0)),
                  pl.BlockSpec((1, 512, PAGE_SIZE), lambda bi, g, pt: (bi, 0, 0)),
                  pl.BlockSpec((PAGE_SIZE, PAGE_SIZE), lambda bi, g, pt: (0, 0))]
                 + [page_spec(i) for i in range(pg)],
        out_specs=pl.BlockSpec((1, SUBLANES, 256), lambda bi, g, pt: (bi, 0, 0)),
        scratch_shapes=[pltpu.VMEM((32, 256), F32), pltpu.VMEM((32, 1), F32)],
    )
    return pl.pallas_call(
        kern,
        grid_spec=grid_spec,
        out_shape=jax.ShapeDtypeStruct((b, SUBLANES, 256), F32),
        compiler_params=_params("parallel", "arbitrary"),
    )(page_table, qbd, new_rows, _later_keys_matrix(PAGE_SIZE), *([cache] * pg))


CMP_PAGE_GROUP = 16


def _nsa_sample_cmp_kernel(pt_ref, q_ref, pea_ref, peb_ref, wa_ref, wb_ref, w2_ref, *rest, n_pages, pos0, nsp):
    pages = rest[:n_pages]
    ocmp_ref, sel_ref, xs_ref = rest[n_pages:]
    pas, pbs = [], []
    for c in range(0, n_pages, CMP_PAGE_GROUP):
        npg = min(CMP_PAGE_GROUP, n_pages - c)
        for i in range(npg):
            for half in range(2):
                xs_ref[half, i * PAGE_SIZE:(i + 1) * PAGE_SIZE, :] = pages[c + i][0, half * LANES:(half + 1) * LANES, :].T
        rows = npg * (PAGE_SIZE // CMP_STRIDE)
        x = jnp.concatenate([xs_ref[half, pl.ds(l, rows, stride=CMP_STRIDE), :]
                             for l in range(CMP_STRIDE) for half in range(2)], axis=1)
        pa, pb = _compress_tokens(x, pea_ref[...], peb_ref[...], wa_ref[...], wb_ref[...], None)
        pas.append(pa)
        pbs.append(pb)
    kcvc = _compress_finish(jnp.concatenate(pas, axis=0), jnp.concatenate(pbs, axis=0), w2_ref[...])
    m = kcvc.shape[0]
    nc = m - 1
    ns = 2 * n_pages + 1
    q = q_ref[0]
    tq = q.shape[0]
    qpos = pos0 + lax.broadcasted_iota(jnp.int32, (tq, 1), 0)
    qpos2 = jnp.concatenate([qpos, qpos], axis=0)
    n_id = lax.broadcasted_iota(jnp.int32, (2 * tq, m), 1)
    cmask = (n_id * CMP_STRIDE + CMP_LEN - 1 <= qpos2) & (n_id < nc)
    ovl = _overlap_matrix(m, nsp, nc, ns)
    for g in range(NSA_KV_HEADS):
        h0 = g * NSA_GROUP
        qs = jnp.concatenate([q[:, h0 * 64:(h0 + 1) * 64], q[:, (h0 + 1) * 64:(h0 + 2) * 64]], axis=0)
        pc = _masked_softmax(_bdot_nt(qs, kcvc[:, g * 64:(g + 1) * 64]) * SCALE, cmask)
        o_cmp = _bdot(pc, kcvc[:, 128 + g * 64:128 + (g + 1) * 64])
        sel_ref[0, g] = _select_blocks(_fdot(pc[:tq] + pc[tq:], ovl), qpos, ns)
        for hg in range(NSA_GROUP):
            ocmp_ref[0, :, (h0 + hg) * 64:(h0 + hg + 1) * 64] = o_cmp[hg * tq:(hg + 1) * tq]


def _nsa_sample_cmp(page_table, q, pea, peb, wa, wb, w2, cache, layer_base, *, pos0):
    b, n_pages = page_table.shape
    nsp = -(-(2 * n_pages + 1) // LANES) * LANES
    kern = functools.partial(_nsa_sample_cmp_kernel, n_pages=n_pages, pos0=pos0, nsp=nsp)
    full = lambda a: pl.BlockSpec(a.shape, lambda bi, pt: (0,) * a.ndim)

    def page_spec(i):
        return pl.BlockSpec((1, 256, PAGE_SIZE), lambda bi, pt: (layer_base + pt[bi, i], 0, 0))

    grid_spec = pltpu.PrefetchScalarGridSpec(
        num_scalar_prefetch=1,
        grid=(b,),
        in_specs=[pl.BlockSpec((1, SUBLANES, 256), lambda bi, pt: (bi, 0, 0)),
                  full(pea), full(peb), full(wa), full(wb), full(w2)] + [page_spec(i) for i in range(n_pages)],
        out_specs=[pl.BlockSpec((1, SUBLANES, 256), lambda bi, pt: (bi, 0, 0)),
                   pl.BlockSpec((1, NSA_KV_HEADS, SUBLANES, nsp), lambda bi, pt: (bi, 0, 0, 0))],
        scratch_shapes=[pltpu.VMEM((2, CMP_PAGE_GROUP * PAGE_SIZE, LANES), F32)],
    )
    return pl.pallas_call(
        kern,
        grid_spec=grid_spec,
        out_shape=[jax.ShapeDtypeStruct((b, SUBLANES, 256), F32),
                   jax.ShapeDtypeStruct((b, NSA_KV_HEADS, SUBLANES, nsp), F32)],
        compiler_params=_params("parallel"),
    )(page_table, q, pea, peb, wa, wb, w2, *([cache] * n_pages))


def _nsa_sample_slc_kernel(pt_ref, q_ref, sm_ref, ocmp_ref, sel_ref, newslc_ref, pastwin_ref, newwin_ref, e_ref,
                           *rest, n_pages, pos0):
    pages = rest[:n_pages]
    o_ref, z_s = rest[n_pages:]
    q = q_ref[0]
    tq = q.shape[0]
    gates = _sigmoid(sm_ref[0])
    nk = (n_pages + 1) * PAGE_SIZE
    past = pastwin_ref.shape[2]
    qpos = pos0 + lax.broadcasted_iota(jnp.int32, (tq, 1), 0)
    qpos2 = jnp.concatenate([qpos, qpos], axis=0)
    kpos = lax.broadcasted_iota(jnp.int32, (tq, nk), 1)
    kposw = jnp.concatenate([pos0 - past + lax.broadcasted_iota(jnp.int32, (2 * tq, past), 1),
                             pos0 + lax.broadcasted_iota(jnp.int32, (2 * tq, PAGE_SIZE), 1)], axis=1)
    dwin = qpos2 - kposw
    wmask = (dwin >= 0) & (dwin < WINDOW)
    for g in range(NSA_KV_HEADS):
        h0 = g * NSA_GROUP
        krow = slice(g * 64, (g + 1) * 64)
        vrow = slice(128 + g * 64, 128 + (g + 1) * 64)
        qs = jnp.concatenate([q[:, h0 * 64:(h0 + 1) * 64], q[:, (h0 + 1) * 64:(h0 + 2) * 64]], axis=0)
        for p in range(n_pages):
            z_s[:, p * PAGE_SIZE:(p + 1) * PAGE_SIZE] = _bdot(qs, pages[p][0, krow, :])
        z_s[:, n_pages * PAGE_SIZE:nk] = _bdot(qs, newslc_ref[0, krow, :])
        selk = jnp.dot(sel_ref[0, g].astype(BF16), e_ref[...], preferred_element_type=F32)
        smask = (selk > 0.5) & (kpos <= qpos)
        p2 = _masked_softmax(z_s[...] * SCALE, jnp.concatenate([smask, smask], axis=0))
        o_slc = _bdot_nt(p2[:, n_pages * PAGE_SIZE:nk], newslc_ref[0, vrow, :])
        for p in range(n_pages):
            o_slc = o_slc + _bdot_nt(p2[:, p * PAGE_SIZE:(p + 1) * PAGE_SIZE], pages[p][0, vrow, :])
        s3 = jnp.concatenate([_bdot(qs, pastwin_ref[0, krow, :]), _bdot(qs, newwin_ref[0, krow, :])], axis=1)
        p3 = _masked_softmax(s3 * SCALE, wmask)
        o_win = _bdot_nt(p3[:, 0:past], pastwin_ref[0, vrow, :]) + _bdot_nt(p3[:, past:], newwin_ref[0, vrow, :])
        for hg in range(NSA_GROUP):
            head = h0 + hg
            rows = slice(hg * tq, (hg + 1) * tq)
            c0 = 8 + 3 * head
            o_ref[0, :, head * 64:(head + 1) * 64] = (gates[:, c0:c0 + 1] * ocmp_ref[0, :, head * 64:(head + 1) * 64]
                                                      + gates[:, c0 + 1:c0 + 2] * o_slc[rows]
                                                      + gates[:, c0 + 2:c0 + 3] * o_win[rows])


def _nsa_sample_slc(page_table, q, small, o_cmp, sel, new_slc, past_win, past_win_base, new_win, cache, layer_base,
                    *, pos0):
    b, n_pages = page_table.shape
    nsp = sel.shape[-1]
    nk = (n_pages + 1) * PAGE_SIZE
    past = past_win.shape[2]
    kern = functools.partial(_nsa_sample_slc_kernel, n_pages=n_pages, pos0=pos0)
    per_b = lambda r, wd: pl.BlockSpec((1, r, wd), lambda bi, pt: (bi, 0, 0))

    def page_spec(i):
        return pl.BlockSpec((1, 256, PAGE_SIZE), lambda bi, pt: (layer_base + pt[bi, i], 0, 0))

    grid_spec = pltpu.PrefetchScalarGridSpec(
        num_scalar_prefetch=1,
        grid=(b,),
        in_specs=[per_b(SUBLANES, 256), per_b(SUBLANES, LANES), per_b(SUBLANES, 256),
                  pl.BlockSpec((1, NSA_KV_HEADS, SUBLANES, nsp), lambda bi, pt: (bi, 0, 0, 0)),
                  per_b(256, PAGE_SIZE),
                  pl.BlockSpec((1, 256, past), lambda bi, pt: (past_win_base + bi, 0, 0)),
                  per_b(256, PAGE_SIZE),
                  pl.BlockSpec((nsp, nk), lambda bi, pt: (0, 0))] + [page_spec(i) for i in range(n_pages)],
        out_specs=per_b(SUBLANES, 256),
        scratch_shapes=[pltpu.VMEM((2 * SUBLANES, nk), F32)],
    )
    return pl.pallas_call(
        kern,
        grid_spec=grid_spec,
        out_shape=jax.ShapeDtypeStruct((b, SUBLANES, 256), F32),
        compiler_params=_params("parallel"),
    )(page_table, q, small, o_cmp, sel, new_slc, past_win, new_win, _block_expander(nsp, nk), *([cache] * n_pages))


def _prep_layer(l, P):
    w_in = P["w_in"][l]
    col = lambda lo, hi: w_in[:, lo:hi]
    small = jnp.concatenate([col(1024, 1032), col(2568, 2580), jnp.zeros((D_MODEL, LANES - 20), F32)], axis=1)
    w_perm = jnp.concatenate([col(0, 1024), col(1032, 1544), col(1544, 1800), col(2580, 2836), col(2836, 3348),
                              col(1800, 2568), small], axis=1).astype(BF16)
    par = jnp.zeros((2, LANES), F32).at[0, 4:8].set(P["gdn_a_log"][l]).at[1, 4:8].set(P["gdn_dt_bias"][l])

    def block_diag(w):
        return jnp.einsum("ncd,nm->ncmd", w, jnp.eye(4, dtype=F32)).reshape(GROUP_WIDTH, GROUP_WIDTH).astype(BF16)

    eye2 = jnp.eye(2, dtype=F32)
    w1 = P["nsa_cmp_w1"][l]
    half = CMP_LEN // 2

    def expand_w1(w):
        return jnp.einsum("klde,kK,gG->lkgdKGe", w, eye2, eye2).reshape(half * 256, 256).astype(BF16)

    def expand_pe(pe):
        return jnp.broadcast_to(pe.transpose(1, 0, 2)[:, :, None, :], (half, 2, 2, HEAD_DIM)).reshape(1, half * 256)

    pe = P["nsa_cmp_pe"][l]
    w2 = jnp.einsum("kef,kK,gG->kgeKGf", P["nsa_cmp_w2"][l], eye2, eye2).reshape(256, 256).astype(BF16)
    w_kv_t = jnp.concatenate([col(2836, 3348), col(1800, 2568)], axis=1).T.astype(BF16)
    return dict(
        norm1=P["norm1"][l][None], w_in=w_perm, w_kv_t=w_kv_t,
        gdn_cw=P["gdn_conv_w"][l], gdn_par=par, gdn_norm=P["gdn_norm"][l][None],
        lru_cw=P["lru_conv_w"][l],
        lru_vec=jnp.stack([P["lru_conv_b"][l], P["lru_ba"][l], P["lru_bx"][l], P["lru_lambda"][l]]),
        lru_wa=block_diag(P["lru_wa"][l]), lru_wx=block_diag(P["lru_wx"][l]),
        pea=expand_pe(pe[:, :half]), peb=expand_pe(pe[:, half:]),
        cmp_wa=expand_w1(w1[:, :half]), cmp_wb=expand_w1(w1[:, half:]), cmp_w2=w2,
        w_out=P["w_out"][l].astype(BF16), norm2=P["norm2"][l][None],
        wg=P["ffn_w_gate"][l].astype(BF16), wu=P["ffn_w_up"][l].astype(BF16),
        ffn_cw=P["ffn_conv_w"][l], wd=P["ffn_w_down"][l].astype(BF16),
    )


def _rope_tables(pos):
    half = HEAD_DIM // 2
    inv = ROPE_THETA ** (-jnp.arange(half, dtype=F32) / half)
    ang = pos.astype(F32)[:, None] * inv[None, :]
    c, s = jnp.cos(ang), jnp.sin(ang)
    return jnp.tile(jnp.concatenate([c, c], axis=-1), (1, 2)), jnp.tile(jnp.concatenate([-s, s], axis=-1), (1, 2))


PROMPT_TM = 256
PROMPT_TQ = 128
SB_TQ = 256
GDN_CHUNK = 64


def _layer_prompt(x2d, W, b, l, cos, sin, final, norm_f):
    outs = _in_proj_t(x2d, W["norm1"], W["w_in"], W["w_kv_t"], cos, sin, b, l, PROMPT_TM)
    zg, zl, nq, sq, cmp_r, small = [o.reshape(b, l, o.shape[-1]) for o in outs[:6]]
    skv_t, cmp_t, slc_t, win_t = outs[6:]
    o_gdn, gdn_conv, gdn_s = _gdn(zg, small, jnp.zeros((b, SUBLANES, 768), F32), jnp.zeros((b, N_HEADS, 64, 64), F32),
                                  W["gdn_cw"], W["gdn_par"], W["gdn_norm"], tl=PROMPT_TM, chunk=GDN_CHUNK,
                                  l_real=PROMPT_TM)
    o_lru, lru_conv, lru_h = _lru(zl, jnp.zeros((b, SUBLANES, GROUP_WIDTH), F32), jnp.zeros((b, 1, GROUP_WIDTH), F32),
                                  W["lru_cw"], W["lru_vec"], W["lru_wa"], W["lru_wx"], tl=PROMPT_TM, l_real=PROMPT_TM)
    kcvc = _cmp_prompt(cmp_r.reshape(b, l // CMP_STRIDE, CMP_STRIDE * 256), W["pea"], W["peb"],
                       W["cmp_wa"], W["cmp_wb"], W["cmp_w2"])
    o_nsa = _nsa_prompt(nq, small, kcvc, slc_t, win_t, tq=PROMPT_TQ)
    o_sb = _sb_prompt(sq, skv_t, tq=min(SB_TQ, l))
    mixes = [o.reshape(b * l, GROUP_WIDTH) for o in (o_gdn, o_lru, o_nsa, o_sb)]
    res = _out_ffn(x2d, mixes, W["w_out"], W["norm2"], W["wg"], W["wu"], W["ffn_cw"], W["wd"],
                   jnp.zeros((SUBLANES, D_FF), F32), norm_f, tm=PROMPT_TM, stride=1, tiles_per_seq=l // PROMPT_TM,
                   final=final)
    kv5 = lambda a, h: jnp.transpose(a.reshape(b, 2, h, HEAD_DIM, a.shape[-1]), (0, 4, 1, 2, 3))
    wkeep = min(WINDOW, l)
    states = (kv5(cmp_t, 2), kv5(slc_t, 2), kv5(skv_t, 4), kv5(win_t[:, :, l - wkeep:], 2), gdn_conv, gdn_s, lru_conv,
              lru_h[:, 0], res[1][l // PROMPT_TM - 1::l // PROMPT_TM])
    return res[0], states, (res[2] if final else None)


def _layer_sample(x2d, W, layer, S, cos, sin, final, norm_f):
    page_table = S["page_table"]
    b, n_pages = page_table.shape
    t_new = x2d.shape[0] // b
    pos0 = n_pages * PAGE_SIZE
    n_phys = S["n_phys"]
    outs = _in_proj(x2d, W["norm1"], W["w_in"], cos, sin, x2d.shape[0], 1)

    def to_batch_major(a, rows=SUBLANES):
        a = a.reshape(t_new, b, a.shape[-1]).transpose(1, 0, 2)
        return jnp.pad(a, ((0, 0), (0, rows - t_new), (0, 0)))

    zg, zl, nq, sq, skv, cmp_r, slc_r, win_r, small = [to_batch_major(o) for o in outs]
    front = lambda st: jnp.pad(st, ((0, 0), (SUBLANES - st.shape[1], 0), (0, 0)))
    o_gdn, gdn_conv, gdn_s = _gdn(zg, small, front(S["gdn_conv"][layer]), S["gdn"][layer],
                                  W["gdn_cw"], W["gdn_par"], W["gdn_norm"], tl=SUBLANES, chunk=SUBLANES, l_real=t_new)
    o_lru, lru_conv, lru_h = _lru(zl, front(S["lru_conv"][layer]), S["lru"][layer][:, None, :],
                                  W["lru_cw"], W["lru_vec"], W["lru_wa"], W["lru_wx"], tl=SUBLANES, l_real=t_new)
    base = layer * n_phys
    o_cmp, sel = _nsa_sample_cmp(page_table, nq, W["pea"], W["peb"], W["cmp_wa"], W["cmp_wb"], W["cmp_w2"],
                                 S["cmp_pages"], base, pos0=pos0)
    pad_page = lambda a: jnp.pad(a.transpose(0, 2, 1), ((0, 0), (0, 0), (0, PAGE_SIZE - a.shape[1])))
    o_nsa = _nsa_sample_slc(page_table, nq, small, o_cmp, sel, pad_page(slc_r), S["win"], layer * b, pad_page(win_r),
                            S["slc_pages"], base, pos0=pos0)
    head_of_col = jnp.arange(256) // HEAD_DIM
    head_of_row = jnp.arange(N_HEADS * SUBLANES) // SUBLANES
    qbd = jnp.where(head_of_row[:, None] == head_of_col[None, :], jnp.tile(sq, (1, N_HEADS, 1)), 0.0)
    o_sb = _sb_sample(page_table, qbd, pad_page(skv), S["sb_pages"], base, pg=min(32, n_pages), pos0=pos0)

    def to_time_major(a):
        return a[:, :t_new].transpose(1, 0, 2).reshape(t_new * b, a.shape[-1])

    mixes = [to_time_major(o) for o in (o_gdn, o_lru, o_nsa, o_sb)]
    buf = S["ffn_conv"][layer].transpose(1, 0, 2).reshape(2 * b, D_FF)
    res = _out_ffn(x2d, mixes, W["w_out"], W["norm2"], W["wg"], W["wu"], W["ffn_cw"], W["wd"], buf, norm_f,
                   tm=x2d.shape[0], stride=b, tiles_per_seq=1, final=final)
    kv5 = lambda a, h: a[:, :t_new].reshape(b, t_new, 2, h, HEAD_DIM)
    ffn_state = res[1].reshape(2, b, D_FF).transpose(1, 0, 2)
    states = (kv5(cmp_r, 2), kv5(slc_r, 2), kv5(skv, 4), kv5(win_r, 2), gdn_conv, gdn_s, lru_conv, lru_h[:, 0], ffn_state)
    return res[0], states, (res[2] if final else None)


def _sample_state_views(cache_cmp_kv, cache_slc_kv, cache_sb_kv, cache_win_kv):
    def view(c):
        d0, d1, tok, kv, h, d = c.shape
        return jnp.transpose(c, (0, 1, 3, 4, 5, 2)).reshape(d0 * d1, kv * h * d, tok)

    return dict(cmp_pages=view(cache_cmp_kv), slc_pages=view(cache_slc_kv), sb_pages=view(cache_sb_kv),
                win=view(cache_win_kv))


def kernel(x_prompt, x_sample, cache_cmp_kv, cache_slc_kv, cache_sb_kv, cache_win_kv, state_gdn_conv, state_gdn, state_lru_conv, state_lru, state_ffn_conv, page_table, norm1, w_in, gdn_conv_w, gdn_a_log, gdn_dt_bias, gdn_norm, lru_conv_w, lru_conv_b, lru_wa, lru_ba, lru_wx, lru_bx, lru_lambda, nsa_cmp_pe, nsa_cmp_w1, nsa_cmp_w2, w_out, norm2, ffn_w_gate, ffn_w_up, ffn_conv_w, ffn_w_down, norm_f):
    P = dict(norm1=norm1, w_in=w_in, gdn_conv_w=gdn_conv_w, gdn_a_log=gdn_a_log, gdn_dt_bias=gdn_dt_bias,
             gdn_norm=gdn_norm, lru_conv_w=lru_conv_w, lru_conv_b=lru_conv_b, lru_wa=lru_wa, lru_ba=lru_ba,
             lru_wx=lru_wx, lru_bx=lru_bx, lru_lambda=lru_lambda, nsa_cmp_pe=nsa_cmp_pe, nsa_cmp_w1=nsa_cmp_w1,
             nsa_cmp_w2=nsa_cmp_w2, w_out=w_out, norm2=norm2, ffn_w_gate=ffn_w_gate, ffn_w_up=ffn_w_up,
             ffn_conv_w=ffn_conv_w, ffn_w_down=ffn_w_down)
    depth = w_in.shape[0]
    weights = [_prep_layer(l, P) for l in range(depth)]
    gf = norm_f[None]

    b, l, _ = x_prompt.shape
    cos, sin = _rope_tables(jnp.arange(l))
    h = x_prompt.reshape(b * l, D_MODEL)
    p_states, y_prompt = [], None
    for layer in range(depth):
        h, st, y = _layer_prompt(h, weights[layer], b, l, cos, sin, layer == depth - 1, gf)
        p_states.append(st)
        y_prompt = y
    y_prompt = y_prompt.reshape(b, l, D_MODEL)

    db, t_new, _ = x_sample.shape
    n_phys = cache_sb_kv.shape[1]
    n_pages = page_table.shape[1]
    pos0 = n_pages * PAGE_SIZE
    past_win = cache_win_kv.shape[2]
    S = _sample_state_views(cache_cmp_kv, cache_slc_kv, cache_sb_kv, cache_win_kv)
    S.update(page_table=page_table, n_phys=n_phys,
             gdn_conv=state_gdn_conv, gdn=state_gdn, lru_conv=state_lru_conv, lru=state_lru, ffn_conv=state_ffn_conv)
    cos_s, sin_s = _rope_tables(pos0 + jnp.repeat(jnp.arange(t_new), db))
    h = x_sample.transpose(1, 0, 2).reshape(t_new * db, D_MODEL)
    s_states, y_sample = [], None
    for layer in range(depth):
        h, st, y = _layer_sample(h, weights[layer], layer, S, cos_s, sin_s, layer == depth - 1, gf)
        s_states.append(st)
        y_sample = y
    y_sample = y_sample.reshape(t_new, db, D_MODEL).transpose(1, 0, 2)

    stk = lambda states, i: jnp.stack([s[i] for s in states])
    return ((y_prompt, y_sample) + tuple(stk(p_states, i) for i in range(9))
            + tuple(stk(s_states, i) for i in range(9)))
```

```python
import functools
import math

import jax
import jax.numpy as jnp
from jax import lax
from jax.experimental import pallas as pl
from jax.experimental.pallas import tpu as pltpu

F32, BF16 = jnp.float32, jnp.bfloat16

D_MODEL = 1024
HEAD_DIM = 64
GROUP_WIDTH = 256
N_HEADS = 4
NSA_KV_HEADS = 2
NSA_GROUP = 2
D_FF = 2816
SHORT_CONV = 4
FFN_CONV = 3
LRU_C = 8.0
CMP_STRIDE = 16
CMP_LEN = 32
SLC_BLOCK = 64
SLC_TOPK = 16
N_LOCAL = 2
WINDOW = 512
FORCE_BONUS = 1.0e3
ROPE_THETA = 10000.0
EPS = 1e-6
NEG = -1e30
SCALE = HEAD_DIM ** -0.5
PAGE_SIZE = 128

VMEM_LIMIT_BYTES = 56 * 1024 * 1024
LANES = 128
SUBLANES = 8

IN_WIDTHS = (1024, 512, 256, 256, 512, 256, 256, 256, 128)
D_IN_PAD = sum(IN_WIDTHS)


def _params(*sem):
    return pltpu.CompilerParams(dimension_semantics=sem, vmem_limit_bytes=VMEM_LIMIT_BYTES)


def _bdot(a, b):
    return jnp.dot(a.astype(BF16), b.astype(BF16), preferred_element_type=F32)


def _bdot_nt(a, b):
    return lax.dot_general(a.astype(BF16), b.astype(BF16), (((1,), (1,)), ((), ())), preferred_element_type=F32)


def _bdot_tn(a, b):
    return lax.dot_general(a.astype(BF16), b.astype(BF16), (((0,), (0,)), ((), ())), preferred_element_type=F32)


def _fdot(a, b):
    return jnp.dot(a, b, preferred_element_type=F32, precision=lax.Precision.HIGHEST)


def _dot3(a, b):
    ah, bh = a.astype(BF16), b.astype(BF16)
    al, bl = (a - ah.astype(F32)).astype(BF16), (b - bh.astype(F32)).astype(BF16)
    dot = lambda x, y: jnp.dot(x, y, preferred_element_type=F32)
    return dot(ah, bh) + (dot(ah, bl) + dot(al, bh))


GDN_GROUP_CHUNKS = 2


def _split_dot(a, b_exact):
    hi = a.astype(BF16)
    lo = (a - hi.astype(F32)).astype(BF16)
    return (jnp.dot(hi, b_exact, preferred_element_type=F32) + jnp.dot(lo, b_exact, preferred_element_type=F32))


def _sigmoid(x):
    return jax.nn.sigmoid(x)


def _silu(x):
    return x * jax.nn.sigmoid(x)


def _softplus(x):
    return jnp.maximum(x, 0.0) + jnp.log1p(jnp.exp(-jnp.abs(x)))


def _gelu_tanh(x):
    return 0.5 * x * (1.0 + jnp.tanh(math.sqrt(2.0 / math.pi) * (x + 0.044715 * (x * x * x))))


def _rms(x, g):
    return x * lax.rsqrt(jnp.mean(x * x, axis=-1, keepdims=True) + EPS) * g


def _rope_pairs(v, cos, sin):
    lane = lax.broadcasted_iota(jnp.int32, v.shape, 1)
    first_half = (lane % HEAD_DIM) < (HEAD_DIM // 2)
    partner = jnp.where(first_half, pltpu.roll(v, LANES - HEAD_DIM // 2, 1), pltpu.roll(v, HEAD_DIM // 2, 1))
    return v * cos + partner * sin


def _masked_softmax(s, mask):
    sm = jnp.where(mask, s, NEG)
    m = jnp.max(sm, axis=-1, keepdims=True)
    e = jnp.where(mask, jnp.exp(sm - m), 0.0)
    den = jnp.sum(e, axis=-1, keepdims=True)
    return e * (1.0 / jnp.where(den > 0.0, den, 1.0))


def _in_proj_kernel(x_ref, g_ref, w_ref, cos_ref, sin_ref,
                    zg_ref, zl_ref, nq_ref, sq_ref, skv_ref, cmp_ref, slc_ref, win_ref, sm_ref):
    xb = _rms(x_ref[...], g_ref[...]).astype(BF16)
    cos, sin = cos_ref[...], sin_ref[...]

    def mm(lo, width):
        return jnp.dot(xb, w_ref[:, lo:lo + width], preferred_element_type=F32)

    zg_ref[...] = mm(0, 1024)
    zl_ref[...] = mm(1024, 512)
    q = mm(1536, 256)
    nq_ref[:, 0:128] = _rope_pairs(q[:, 0:128], cos, sin)
    nq_ref[:, 128:256] = _rope_pairs(q[:, 128:256], cos, sin)
    sq_ref[...] = mm(1792, 256)
    skv_ref[...] = mm(2048, 512)
    for ref, lo in ((cmp_ref, 2560), (slc_ref, 2816), (win_ref, 3072)):
        kv = mm(lo, 256)
        ref[:, 0:128] = _rope_pairs(kv[:, 0:128], cos, sin)
        ref[:, 128:256] = kv[:, 128:256]
    sm_ref[...] = mm(3328, 128)


def _in_proj(x2d, g, w, cos, sin, tm, table_tiles):
    n = x2d.shape[0]
    nt = n // tm
    row = lambda wd: pl.BlockSpec((tm, wd), lambda i: (i, 0))
    tab = pl.BlockSpec((tm, LANES), lambda i: (i % table_tiles, 0))
    return pl.pallas_call(
        _in_proj_kernel,
        grid=(nt,),
        in_specs=[row(D_MODEL), pl.BlockSpec((1, D_MODEL), lambda i: (0, 0)),
                  pl.BlockSpec((D_MODEL, D_IN_PAD), lambda i: (0, 0)), tab, tab],
        out_specs=[row(wd) for wd in IN_WIDTHS],
        out_shape=[jax.ShapeDtypeStruct((n, wd), F32) for wd in IN_WIDTHS],
        compiler_params=_params("parallel"),
    )(x2d, g, w, cos, sin)


def _rope_rows(v, cos_t, sin_t):
    q = HEAD_DIM // 2
    partner = jnp.concatenate([v[q:2 * q], v[0:q], v[3 * q:4 * q], v[2 * q:3 * q]], axis=0)
    return v * cos_t + partner * sin_t


KV_T_ROWS = 1280


def _in_proj_t_kernel(x_ref, g_ref, w_ref, wt_ref, cos_ref, sin_ref, cost_ref, sint_ref,
                      zg_ref, zl_ref, nq_ref, sq_ref, cmp_ref, sm_ref, skvt_ref, cmpt_ref, slct_ref, wint_ref):
    xb = _rms(x_ref[...], g_ref[...]).astype(BF16)
    cos, sin = cos_ref[...], sin_ref[...]
    cos_t, sin_t = cost_ref[...], sint_ref[...]

    def mm(lo, width):
        return jnp.dot(xb, w_ref[:, lo:lo + width], preferred_element_type=F32)

    def mm_t(lo, rows):
        return lax.dot_general(wt_ref[lo:lo + rows, :], xb, (((1,), (1,)), ((), ())), preferred_element_type=F32)

    zg_ref[...] = mm(0, 1024)
    zl_ref[...] = mm(1024, 512)
    q = mm(1536, 256)
    nq_ref[:, 0:128] = _rope_pairs(q[:, 0:128], cos, sin)
    nq_ref[:, 128:256] = _rope_pairs(q[:, 128:256], cos, sin)
    sq_ref[...] = mm(1792, 256)
    kv = mm(2560, 256)
    cmp_ref[:, 0:128] = _rope_pairs(kv[:, 0:128], cos, sin)
    cmp_ref[:, 128:256] = kv[:, 128:256]
    sm_ref[...] = mm(3328, 128)
    skvt_ref[0] = mm_t(0, 512)
    for ref, lo in ((cmpt_ref, 512), (slct_ref, 768), (wint_ref, 1024)):
        kvt = mm_t(lo, 256)
        ref[0, 0:128, :] = _rope_rows(kvt[0:128], cos_t, sin_t)
        ref[0, 128:256, :] = kvt[128:256]


def _in_proj_t(x2d, g, w, wt, cos, sin, b, l, tm):
    tps = l // tm
    row = lambda wd: pl.BlockSpec((tm, wd), lambda i: (i, 0))
    tab = pl.BlockSpec((tm, LANES), lambda i: (i % tps, 0))
    tab_t = pl.BlockSpec((LANES, tm), lambda i: (0, i % tps))
    kvt = lambda c: pl.BlockSpec((1, c, tm), lambda i: (i // tps, 0, i % tps))
    row_widths = (1024, 512, 256, 256, 256, 128)
    t_rows = (512, 256, 256, 256)
    return pl.pallas_call(
        _in_proj_t_kernel,
        grid=(b * tps,),
        in_specs=[row(D_MODEL), pl.BlockSpec((1, D_MODEL), lambda i: (0, 0)),
                  pl.BlockSpec((D_MODEL, D_IN_PAD), lambda i: (0, 0)),
                  pl.BlockSpec((KV_T_ROWS, D_MODEL), lambda i: (0, 0)), tab, tab, tab_t, tab_t],
        out_specs=[row(wd) for wd in row_widths] + [kvt(c) for c in t_rows],
        out_shape=[jax.ShapeDtypeStruct((b * l, wd), F32) for wd in row_widths]
                  + [jax.ShapeDtypeStruct((b, c, l), F32) for c in t_rows],
        compiler_params=_params("parallel"),
    )(x2d, g, w, wt, cos, sin, cos.T, sin.T)


FFN_SLABS = 2


def _out_ffn_kernel(*refs, tm, tf, nf, stride, prev_rows, tiles_per_seq, final):
    x_ref, mix_refs, refs = refs[0], refs[1:5], refs[5:]
    if final:
        (wo_ref, g2_ref, wg_ref, wu_ref, cw_ref, wd_ref, buf_ref, gf_ref,
         xo_ref, st_ref, y_ref, carry_s) = refs
    else:
        (wo_ref, g2_ref, wg_ref, wu_ref, cw_ref, wd_ref, buf_ref,
         xo_ref, st_ref, carry_s) = refs
    i = pl.program_id(0)
    mix = jnp.concatenate([m[...] for m in mix_refs], axis=1)
    x1 = x_ref[...] + _bdot(mix, wo_ref[...])
    xn = _rms(x1, g2_ref[...]).astype(BF16)
    seq_start = (i % tiles_per_seq) == 0
    p, s = prev_rows, stride
    x2 = x1
    for j in range(nf):
        cols = slice(j * tf, (j + 1) * tf)
        a = jnp.dot(xn, wg_ref[:, cols], preferred_element_type=F32)
        u = jnp.dot(xn, wu_ref[:, cols], preferred_element_type=F32)
        prev = jnp.where(seq_start, buf_ref[:, cols], carry_s[:, cols])
        ext = jnp.concatenate([prev, a], axis=0)
        cw = cw_ref[:, cols]
        ac = cw[0:1] * ext[p - 2 * s:p - 2 * s + tm] + cw[1:2] * ext[p - s:p - s + tm] + cw[2:3] * ext[p:p + tm]
        carry_s[:, cols] = ext[tm:tm + p]
        st_ref[0, :, cols] = ext[tm + p - 2 * s:tm + p]
        x2 = x2 + _bdot(_silu(ac) * u, wd_ref[cols, :])
    xo_ref[...] = x2
    if final:
        y_ref[...] = _rms(x2, gf_ref[...])


def _out_ffn(x2d, mixes, wo, g2, wg, wu, cw, wd, buf, gf, *, tm, stride, tiles_per_seq, final):
    n = x2d.shape[0]
    nt = n // tm
    nf = FFN_SLABS
    tf = D_FF // nf
    prev_rows = buf.shape[0]
    kern = functools.partial(_out_ffn_kernel, tm=tm, tf=tf, nf=nf, stride=stride, prev_rows=prev_rows,
                             tiles_per_seq=tiles_per_seq, final=final)
    tok = pl.BlockSpec((tm, D_MODEL), lambda i: (i, 0))
    part = pl.BlockSpec((tm, GROUP_WIDTH), lambda i: (i, 0))
    const = lambda shape: pl.BlockSpec(shape, lambda i: (0, 0), pipeline_mode=pl.Buffered(1))
    in_specs = [tok, part, part, part, part,
                const((D_MODEL, D_MODEL)), const((1, D_MODEL)), const((D_MODEL, D_FF)), const((D_MODEL, D_FF)),
                const((FFN_CONV, D_FF)), const((D_FF, D_MODEL)), const((prev_rows, D_FF))]
    args = [x2d, *mixes, wo, g2, wg, wu, cw, wd, buf]
    out_specs = [tok, pl.BlockSpec((1, 2 * stride, D_FF), lambda i: (i, 0, 0))]
    out_shape = [jax.ShapeDtypeStruct((n, D_MODEL), F32),
                 jax.ShapeDtypeStruct((nt, 2 * stride, D_FF), F32)]
    if final:
        in_specs.append(const((1, D_MODEL)))
        args.append(gf)
        out_specs.append(tok)
        out_shape.append(jax.ShapeDtypeStruct((n, D_MODEL), F32))
    return pl.pallas_call(
        kern,
        grid=(nt,),
        in_specs=in_specs,
        out_specs=out_specs,
        out_shape=out_shape,
        scratch_shapes=[pltpu.VMEM((prev_rows, D_FF), F32)],
        compiler_params=_params("arbitrary"),
    )(*args)


def _short_conv(x, carry_ref, cw, tl):
    ext = jnp.concatenate([carry_ref[...], x], axis=0)
    y = cw[0:1] * ext[5:5 + tl]
    for i in range(1, SHORT_CONV):
        y = y + cw[i:i + 1] * ext[5 + i:5 + i + tl]
    carry_ref[...] = ext[tl:tl + SUBLANES]
    return y, ext


def _gdn_kernel(zg_ref, sm_ref, cst_ref, s0_ref, cw_ref, par_ref, nrm_ref,
                o_ref, cout_ref, sout_ref, carry_s, state_s, *, tl, chunk, l_real):
    t = pl.program_id(1)

    @pl.when(t == 0)
    def _():
        carry_s[...] = cst_ref[0]
        state_s[...] = s0_ref[0]

    z = zg_ref[0]
    gate = z[:, 768:1024]
    y, ext = _short_conv(z[:, 0:768], carry_s, cw_ref[...], tl)
    cout_ref[0] = ext[SUBLANES + l_real - 3:SUBLANES + l_real]
    y = _silu(y)

    small = sm_ref[0]
    par = par_ref[...]
    beta_c = _sigmoid(small)
    g_c = -jnp.exp(par[0:1]) * _softplus(small + par[1:2])
    if l_real < tl:
        real = lax.broadcasted_iota(jnp.int32, small.shape, 0) < l_real
        beta_c = jnp.where(real, beta_c, 0.0)
        g_c = jnp.where(real, g_c, 0.0)
    ri = lax.broadcasted_iota(jnp.int32, (tl, tl), 0)
    ci = lax.broadcasted_iota(jnp.int32, (tl, tl), 1)
    same = (ri // chunk) == (ci // chunk)
    gc = _fdot(jnp.where(same & (ci <= ri), 1.0, 0.0), g_c)
    grest = _fdot(jnp.where(same & (ci > ri), 1.0, 0.0), g_c)
    gc_t = gc.T

    ii = lax.broadcasted_iota(jnp.int32, (chunk, chunk), 0)
    jj = lax.broadcasted_iota(jnp.int32, (chunk, chunk), 1)
    causal, strict = ii >= jj, ii > jj
    eye = jnp.where(ii == jj, 1.0, 0.0)
    gnorm = nrm_ref[...]
    n_chunks = tl // chunk
    heads = range(N_HEADS)
    S = [state_s[h] for h in heads]
    out_rows = []
    for c0 in range(0, n_chunks, GDN_GROUP_CHUNKS):
        pairs = [(c, h) for c in range(c0, min(c0 + GDN_GROUP_CHUNKS, n_chunks)) for h in heads]
        rows_of = lambda c: slice(c * chunk, (c + 1) * chunk)
        q = [y[rows_of(c), h * 64:(h + 1) * 64] for c, h in pairs]
        k = [y[rows_of(c), 256 + h * 64:256 + (h + 1) * 64] for c, h in pairs]
        v = [y[rows_of(c), 512 + h * 64:512 + (h + 1) * 64] for c, h in pairs]
        q = [x * lax.rsqrt(jnp.sum(x * x, axis=-1, keepdims=True) + EPS) * SCALE for x in q]
        k = [x * lax.rsqrt(jnp.sum(x * x, axis=-1, keepdims=True) + EPS) for x in k]
        beta = [beta_c[rows_of(c), h:h + 1] for c, h in pairs]
        gcc = [gc[rows_of(c), 4 + h:5 + h] for c, h in pairs]
        gcr = [gc_t[4 + h:5 + h, rows_of(c)] for c, h in pairs]
        decay = [jnp.where(causal, jnp.exp(jnp.where(causal, a - b, 0.0)), 0.0) for a, b in zip(gcc, gcr)]
        kb = [a * b for a, b in zip(k, beta)]
        A = [jnp.where(strict, _bdot_nt(a, b) * d, 0.0) for a, b, d in zip(kb, k, decay)]
        T = [eye - a for a in A]
        P = A
        span = 2
        while span < chunk:
            P = [_dot3(p, p) for p in P]
            T = [t + _dot3(t, p) for t, p in zip(T, P)]
            span *= 2
        u = [_bdot(t, a * b) for t, a, b in zip(T, v, beta)]
        w = [_bdot(t, a * jnp.exp(g)) for t, a, g in zip(T, kb, gcc)]
        attn = [_bdot_nt(a, b) * d for a, b, d in zip(q, k, decay)]
        qe = [a * jnp.exp(g) for a, g in zip(q, gcc)]
        kd = [a * jnp.exp(grest[rows_of(c), 4 + h:5 + h]) for a, (c, h) in zip(k, pairs)]
        for i0 in range(0, len(pairs), N_HEADS):
            c = pairs[i0][0]
            v_new = [u[i0 + h] - _bdot(w[i0 + h], S[h]) for h in heads]
            o = [_bdot(qe[i0 + h], S[h]) + _bdot(attn[i0 + h], v_new[h]) for h in heads]
            last = (c + 1) * chunk - 1
            S = [S[h] * jnp.exp(gc[last:last + 1, 4 + h:5 + h]) + _bdot_tn(kd[i0 + h], v_new[h]) for h in heads]
            o = [_rms(o[h], gnorm) * _silu(gate[rows_of(c), h * 64:(h + 1) * 64]) for h in heads]
            out_rows.append(jnp.concatenate(o, axis=1))
    for h in heads:
        state_s[h] = S[h]
        sout_ref[0, h] = S[h]
    o_ref[0] = jnp.concatenate(out_rows, axis=0)


def _gdn(zg, small, conv_state8, s0, cw, par, gnorm, *, tl, chunk, l_real):
    b, l, _ = zg.shape
    nt = l // tl
    kern = functools.partial(_gdn_kernel, tl=tl, chunk=chunk, l_real=l_real)
    return pl.pallas_call(
        kern,
        grid=(b, nt),
        in_specs=[pl.BlockSpec((1, tl, 1024), lambda i, t: (i, t, 0)),
                  pl.BlockSpec((1, tl, LANES), lambda i, t: (i, t, 0)),
                  pl.BlockSpec((1, SUBLANES, 768), lambda i, t: (i, 0, 0)),
                  pl.BlockSpec((1, N_HEADS, 64, 64), lambda i, t: (i, 0, 0, 0)),
                  pl.BlockSpec((SHORT_CONV, 768), lambda i, t: (0, 0)),
                  pl.BlockSpec((2, LANES), lambda i, t: (0, 0)),
                  pl.BlockSpec((1, HEAD_DIM), lambda i, t: (0, 0))],
        out_specs=[pl.BlockSpec((1, tl, GROUP_WIDTH), lambda i, t: (i, t, 0)),
                   pl.BlockSpec((1, 3, 768), lambda i, t: (i, 0, 0)),
                   pl.BlockSpec((1, N_HEADS, 64, 64), lambda i, t: (i, 0, 0, 0))],
        out_shape=[jax.ShapeDtypeStruct((b, l, GROUP_WIDTH), F32),
                   jax.ShapeDtypeStruct((b, 3, 768), F32),
                   jax.ShapeDtypeStruct((b, N_HEADS, 64, 64), F32)],
        scratch_shapes=[pltpu.VMEM((SUBLANES, 768), F32), pltpu.VMEM((N_HEADS, 64, 64), F32)],
        compiler_params=_params("parallel", "arbitrary"),
    )(zg, small, conv_state8, s0, cw, par, gnorm)


def _lru_kernel(zl_ref, cst_ref, h0_ref, cw_ref, vec_ref, wa_ref, wx_ref,
                y_ref, cout_ref, hout_ref, carry_s, h_s, *, tl, l_real):
    t = pl.program_id(1)

    @pl.when(t == 0)
    def _():
        carry_s[...] = cst_ref[0]
        h_s[...] = h0_ref[0]

    z = zl_ref[0]
    gate = z[:, 256:512]
    vec = vec_ref[...]
    xc, ext = _short_conv(z[:, 0:256], carry_s, cw_ref[...], tl)
    cout_ref[0] = ext[SUBLANES + l_real - 3:SUBLANES + l_real]
    xc = xc + vec[0:1]
    r = _sigmoid(_bdot(xc, wa_ref[...]) + vec[1:2])
    i = _sigmoid(_bdot(xc, wx_ref[...]) + vec[2:3])
    log_a = -LRU_C * r * _softplus(-vec[3:4])
    a = jnp.exp(log_a)
    u = jnp.sqrt(1.0 - a * a) * (i * xc)
    rowid = lax.broadcasted_iota(jnp.int32, u.shape, 0)
    u = u + jnp.where(rowid == 0, a * h_s[...], 0.0)
    s = 1
    while s < tl:
        a_sh = jnp.concatenate([jnp.ones((s, GROUP_WIDTH), F32), a[:tl - s]], axis=0)
        u_sh = jnp.concatenate([jnp.zeros((s, GROUP_WIDTH), F32), u[:tl - s]], axis=0)
        u = a * u_sh + u
        a = a * a_sh
        s *= 2
    h_last = u[l_real - 1:l_real]
    h_s[...] = h_last
    hout_ref[0] = h_last
    y_ref[0] = u * _gelu_tanh(gate)


def _lru(zl, conv_state8, h0, cw, vec, wa, wx, *, tl, l_real):
    b, l, _ = zl.shape
    nt = l // tl
    kern = functools.partial(_lru_kernel, tl=tl, l_real=l_real)
    return pl.pallas_call(
        kern,
        grid=(b, nt),
        in_specs=[pl.BlockSpec((1, tl, 512), lambda i, t: (i, t, 0)),
                  pl.BlockSpec((1, SUBLANES, GROUP_WIDTH), lambda i, t: (i, 0, 0)),
                  pl.BlockSpec((1, 1, GROUP_WIDTH), lambda i, t: (i, 0, 0)),
                  pl.BlockSpec((SHORT_CONV, GROUP_WIDTH), lambda i, t: (0, 0)),
                  pl.BlockSpec((4, GROUP_WIDTH), lambda i, t: (0, 0)),
                  pl.BlockSpec((GROUP_WIDTH, GROUP_WIDTH), lambda i, t: (0, 0)),
                  pl.BlockSpec((GROUP_WIDTH, GROUP_WIDTH), lambda i, t: (0, 0))],
        out_specs=[pl.BlockSpec((1, tl, GROUP_WIDTH), lambda i, t: (i, t, 0)),
                   pl.BlockSpec((1, 3, GROUP_WIDTH), lambda i, t: (i, 0, 0)),
                   pl.BlockSpec((1, 1, GROUP_WIDTH), lambda i, t: (i, 0, 0))],
        out_shape=[jax.ShapeDtypeStruct((b, l, GROUP_WIDTH), F32),
                   jax.ShapeDtypeStruct((b, 3, GROUP_WIDTH), F32),
                   jax.ShapeDtypeStruct((b, 1, GROUP_WIDTH), F32)],
        scratch_shapes=[pltpu.VMEM((SUBLANES, GROUP_WIDTH), F32), pltpu.VMEM((1, GROUP_WIDTH), F32)],
        compiler_params=_params("parallel", "arbitrary"),
    )(zl, conv_state8, h0, cw, vec, wa, wx)


def _compress_tokens(x, pea, peb, wa, wb, w2):
    pa = _bdot(x + pea, wa)
    pb = _bdot(x + peb, wb)
    return pa, pb


def _compress_finish(pa, pb, w2):
    pb_next = jnp.concatenate([pb[1:], jnp.zeros((1, pb.shape[1]), F32)], axis=0)
    return _bdot(_silu(pa + pb_next), w2)


def _cmp_prompt_kernel(x_ref, pea_ref, peb_ref, wa_ref, wb_ref, w2_ref, o_ref):
    pa, pb = _compress_tokens(x_ref[0], pea_ref[...], peb_ref[...], wa_ref[...], wb_ref[...], None)
    o_ref[0] = _compress_finish(pa, pb, w2_ref[...])


def _cmp_prompt(xc, pea, peb, wa, wb, w2):
    b, m, f = xc.shape
    full = lambda a: pl.BlockSpec(a.shape, lambda i: (0,) * a.ndim)
    return pl.pallas_call(
        _cmp_prompt_kernel,
        grid=(b,),
        in_specs=[pl.BlockSpec((1, m, f), lambda i: (i, 0, 0)), full(pea), full(peb), full(wa), full(wb), full(w2)],
        out_specs=pl.BlockSpec((1, m, 256), lambda i: (i, 0, 0)),
        out_shape=jax.ShapeDtypeStruct((b, m, 256), F32),
        compiler_params=_params("parallel"),
    )(xc, pea, peb, wa, wb, w2)


def _overlap_matrix(m, nsp, nc, ns, transposed=False):
    shape = (nsp, m) if transposed else (m, nsp)
    nn = lax.broadcasted_iota(jnp.int32, shape, 1 if transposed else 0)
    mm = lax.broadcasted_iota(jnp.int32, shape, 0 if transposed else 1)
    ov = (jnp.minimum(nn * CMP_STRIDE + CMP_LEN - 1, mm * SLC_BLOCK + SLC_BLOCK - 1)
          - jnp.maximum(nn * CMP_STRIDE, mm * SLC_BLOCK) + 1)
    return jnp.where((nn < nc) & (mm < ns), jnp.maximum(ov, 0).astype(F32) * (1.0 / CMP_LEN), 0.0)


def _select_bias(imp, qpos, ns):
    rows, nsp = imp.shape
    j = lax.broadcasted_iota(jnp.int32, (rows, nsp), 1)
    cur = qpos // SLC_BLOCK
    valid = (j <= cur) & (j < ns)
    forced = valid & ((j == 0) | (j > cur - N_LOCAL))
    score = jnp.where(valid, imp + jnp.where(forced, FORCE_BONUS, 0.0), NEG)
    score = jnp.where(j < ns, score, -3.0e38)
    cnt = jnp.zeros((rows, nsp), jnp.int32)
    for i in range(ns):
        si = score[:, i:i + 1]
        beats = (si > score) | ((si == score) & (i < j))
        cnt = cnt + beats.astype(jnp.int32)
    return jnp.where((cnt < min(SLC_TOPK, ns)) & valid, 0.0, NEG)


NSA_KEY_CHUNK = 512


NSA_STRIP = 16
NSA_STRIP_UNROLL = 16
SEL_LANES = 64
DEN_ROWS = 16


def _select_bias_t(imp_t, qpos_row, ns):
    nsr, tq = imp_t.shape
    j = lax.broadcasted_iota(jnp.int32, (nsr, tq), 0)
    cur = qpos_row // SLC_BLOCK
    valid = (j <= cur) & (j < ns)
    forced = valid & ((j == 0) | (j > cur - N_LOCAL))
    score = jnp.where(valid, imp_t + jnp.where(forced, FORCE_BONUS, 0.0), NEG)
    score = jnp.where(j < ns, score, -3.0e38)
    cnt = jnp.zeros((nsr, tq), jnp.int32)
    for i in range(ns):
        si = score[i:i + 1, :]
        beats = (si > score) | ((si == score) & (i < j))
        cnt = cnt + beats.astype(jnp.int32)
    return jnp.where((cnt < min(SLC_TOPK, ns)) & valid, 0.0, NEG)


def _nsa_prompt_kernel(q_ref, sm_ref, kc_ref, slc_ref, win_ref, e_ref, o_ref,
                       krhs_s, vslc_s, kwin_s, vwin_s, s_s, e_s, *, tq, l, nc, ns, nsr, wk, kchunk):
    t = pl.program_id(1)
    q0 = t * tq

    @pl.when(t == 0)
    def _():
        ones = jnp.ones((DEN_ROWS, l), BF16)
        for g in range(NSA_KV_HEADS):
            krow = slice(g * 64, (g + 1) * 64)
            vrow = slice(128 + g * 64, 128 + (g + 1) * 64)
            krhs_s[g, 0:64, :] = slc_ref[0, krow, :].astype(BF16)
            krhs_s[g, 64:64 + SEL_LANES, :] = e_ref[...]
            vslc_s[g, 0:64, :] = slc_ref[0, vrow, :].astype(BF16)
            vslc_s[g, 64:64 + DEN_ROWS, :] = ones
            kwin_s[g] = win_ref[0, krow, :].astype(BF16)
            vwin_s[g, 0:64, :] = win_ref[0, vrow, :].astype(BF16)
            vwin_s[g, 64:64 + DEN_ROWS, :] = ones

    q = q_ref[0] * SCALE
    gates = _sigmoid(sm_ref[0])
    kcvc = kc_ref[0]
    m = kcvc.shape[0]
    qpos = q0 + lax.broadcasted_iota(jnp.int32, (tq, 1), 0)
    qpos2 = jnp.concatenate([qpos, qpos], axis=0)
    qpos_row = q0 + lax.broadcasted_iota(jnp.int32, (1, tq), 1)
    n_id = lax.broadcasted_iota(jnp.int32, (2 * tq, m), 1)
    cmask = (n_id * CMP_STRIDE + CMP_LEN - 1 <= qpos2) & (n_id < nc)
    ovl_t = _overlap_matrix(m, nsr, nc, ns, transposed=True)
    start = pl.multiple_of(jnp.clip(q0 + tq - wk, 0, l - wk), LANES)
    dwin = qpos - (start + lax.broadcasted_iota(jnp.int32, (tq, wk), 1))
    wbias = jnp.where((dwin >= 0) & (dwin < WINDOW), 0.0, NEG)
    tri = jnp.where(lax.broadcasted_iota(jnp.int32, (tq, tq), 1) > lax.broadcasted_iota(jnp.int32, (tq, tq), 0),
                    NEG, 0.0)
    n_need = (q0 + tq + kchunk - 1) // kchunk

    def softmax_pv(nk, v_aug):
        def strip(i, carry):
            r = pl.multiple_of(i * NSA_STRIP, NSA_STRIP)
            s = s_s[pl.ds(r, NSA_STRIP), 0:nk]
            e_s[pl.ds(r, NSA_STRIP), 0:nk] = jnp.exp(s - jnp.max(s, axis=-1, keepdims=True)).astype(BF16)
            return carry

        lax.fori_loop(0, 2 * tq // NSA_STRIP, strip, 0, unroll=NSA_STRIP_UNROLL)
        oa = lax.dot_general(e_s[:, 0:nk], v_aug, (((1,), (1,)), ((), ())), preferred_element_type=F32)
        return oa[:, 0:64] * (1.0 / oa[:, 64:65])

    for g in range(NSA_KV_HEADS):
        h0 = g * NSA_GROUP
        q_heads = [q[:, (h0 + hg) * 64:(h0 + hg + 1) * 64] for hg in range(NSA_GROUP)]
        qs = jnp.concatenate(q_heads, axis=0)
        pc = _masked_softmax(_bdot_nt(qs, kcvc[:, g * 64:(g + 1) * 64]), cmask)
        o_cmp = _bdot(pc, kcvc[:, 128 + g * 64:128 + (g + 1) * 64])
        imp_t = lax.dot_general(ovl_t, pc[:tq] + pc[tq:], (((1,), (1,)), ((), ())), preferred_element_type=F32,
                                precision=lax.Precision.HIGHEST)
        bias_t = _select_bias_t(imp_t, qpos_row, ns)
        bias = jnp.concatenate([bias_t, jnp.zeros((LANES - nsr, tq), F32)], axis=0).T[:, 0:SEL_LANES]
        lhs = jnp.concatenate([jnp.concatenate([qh, bias], axis=1) for qh in q_heads], axis=0).astype(BF16)

        s3 = jnp.dot(qs.astype(BF16), kwin_s[g, :, pl.ds(start, wk)], preferred_element_type=F32)
        s_s[0:tq, 0:wk] = s3[0:tq] + wbias
        s_s[tq:2 * tq, 0:wk] = s3[tq:2 * tq] + wbias
        o_win = softmax_pv(wk, vwin_s[g, :, pl.ds(start, wk)])
        part = [gates[:, 8 + 3 * (h0 + hg):9 + 3 * (h0 + hg)] * o_cmp[hg * tq:(hg + 1) * tq]
                + gates[:, 10 + 3 * (h0 + hg):11 + 3 * (h0 + hg)] * o_win[hg * tq:(hg + 1) * tq]
                for hg in range(NSA_GROUP)]
        for nkc in range(1, l // kchunk + 1):

            @pl.when(n_need == nkc)
            def _(nk=nkc * kchunk, g=g, h0=h0, lhs=lhs, part=part):
                s_s[:, 0:nk] = jnp.dot(lhs, krhs_s[g, :, 0:nk], preferred_element_type=F32)
                s_s[0:tq, pl.ds(pl.multiple_of(q0, LANES), tq)] += tri
                s_s[tq:2 * tq, pl.ds(pl.multiple_of(q0, LANES), tq)] += tri
                o_slc = softmax_pv(nk, vslc_s[g, :, 0:nk])
                for hg in range(NSA_GROUP):
                    head = h0 + hg
                    c1 = 9 + 3 * head
                    o_ref[0, :, head * 64:(head + 1) * 64] = (part[hg]
                                                              + gates[:, c1:c1 + 1] * o_slc[hg * tq:(hg + 1) * tq])


def _block_expander(nsp, nkeys):
    blk = lax.broadcasted_iota(jnp.int32, (nsp, nkeys), 0)
    key = lax.broadcasted_iota(jnp.int32, (nsp, nkeys), 1)
    return jnp.where(key // SLC_BLOCK == blk, 1.0, 0.0).astype(BF16)


def _nsa_prompt(q, small, kcvc, slc, win, *, tq):
    b, l, _ = q.shape
    m = kcvc.shape[1]
    nc = m - 1
    ns = -(-l // SLC_BLOCK)
    nsr = -(-ns // SUBLANES) * SUBLANES
    assert nsr <= SEL_LANES
    wk = min(WINDOW + tq, l)
    kern = functools.partial(_nsa_prompt_kernel, tq=tq, l=l, nc=nc, ns=ns, nsr=nsr, wk=wk, kchunk=min(NSA_KEY_CHUNK, l))
    seq = lambda rows: pl.BlockSpec((1, rows, l), lambda i, t: (i, 0, 0))
    tile = lambda wd: pl.BlockSpec((1, tq, wd), lambda i, t: (i, t, 0))
    return pl.pallas_call(
        kern,
        grid=(b, l // tq),
        in_specs=[tile(256), tile(LANES), pl.BlockSpec((1, m, 256), lambda i, t: (i, 0, 0)), seq(256), seq(256),
                  pl.BlockSpec((SEL_LANES, l), lambda i, t: (0, 0))],
        out_specs=tile(256),
        out_shape=jax.ShapeDtypeStruct((b, l, 256), F32),
        scratch_shapes=[pltpu.VMEM((NSA_KV_HEADS, 64 + SEL_LANES, l), BF16),
                        pltpu.VMEM((NSA_KV_HEADS, 64 + DEN_ROWS, l), BF16),
                        pltpu.VMEM((NSA_KV_HEADS, 64, l), BF16),
                        pltpu.VMEM((NSA_KV_HEADS, 64 + DEN_ROWS, l), BF16),
                        pltpu.VMEM((2 * tq, l), F32),
                        pltpu.VMEM((2 * tq, l), BF16)],
        compiler_params=_params("parallel", "arbitrary"),
    )(q, small, kcvc, slc, win, _block_expander(SEL_LANES, l))


def _later_keys_matrix(n):
    a = lax.broadcasted_iota(jnp.int32, (n, n), 0)
    c = lax.broadcasted_iota(jnp.int32, (n, n), 1)
    return jnp.where(a > c, 1.0, 0.0).astype(BF16)


SB_STRIP = 32


def _sb_prompt_kernel(q_ref, kv_ref, later2_ref, o_ref, acc_s, aft_s, z_s, lf_s, hl_s, loc_s, a_s, *, tq):
    qi = pl.program_id(1)
    q = q_ref[0] * SCALE
    acc_s[...] = jnp.zeros_like(acc_s)
    aft_s[...] = jnp.zeros_like(aft_s)
    strips = [slice(r, r + SB_STRIP) for r in range(0, tq, SB_STRIP)]

    def block(k0, diagonal):
        for h in range(N_HEADS):
            kt = kv_ref[0, h * 64:(h + 1) * 64, pl.ds(k0, tq)]
            z_s[h] = _bdot(q[:, h * 64:(h + 1) * 64], kt)
        for h in range(N_HEADS):
            for rows in strips:
                z = z_s[h, rows, :]
                lf = -(jnp.maximum(z, 0.0) + jnp.log(1.0 + jnp.exp(-jnp.abs(z))))
                if diagonal:
                    keep = (lax.broadcasted_iota(jnp.int32, (SB_STRIP, tq), 1)
                            < rows.start + lax.broadcasted_iota(jnp.int32, (SB_STRIP, tq), 0))
                    lf = jnp.where(keep, lf, 0.0)
                hi = lf.astype(BF16)
                lf_s[h, rows, :] = lf
                hl_s[h, rows, 0:tq] = hi
                hl_s[h, rows, tq:2 * tq] = (lf - hi.astype(F32)).astype(BF16)
        for h in range(N_HEADS):
            loc_s[h] = jnp.dot(hl_s[h], later2_ref[...], preferred_element_type=F32)
        for h in range(N_HEADS):
            for rows in strips:
                a = jnp.exp(z_s[h, rows, :] + lf_s[h, rows, :] + (loc_s[h, rows, :] + aft_s[h, rows, :]))
                if diagonal:
                    keep = (lax.broadcasted_iota(jnp.int32, (SB_STRIP, tq), 1)
                            < rows.start + lax.broadcasted_iota(jnp.int32, (SB_STRIP, tq), 0))
                    a = jnp.where(keep, a, 0.0)
                a_s[h, rows, :] = a.astype(BF16)
        for h in range(N_HEADS):
            vt = kv_ref[0, 256 + h * 64:256 + (h + 1) * 64, pl.ds(k0, tq)]
            acc_s[:, h * 64:(h + 1) * 64] += _bdot_nt(a_s[h], vt)
            aft_s[h] = aft_s[h] + loc_s[h, :, 0:1] + lf_s[h, :, 0:1]

    block(pl.multiple_of(qi * tq, tq), True)

    def body(it, carry):
        block(pl.multiple_of((qi - 1 - it) * tq, tq), False)
        return carry

    lax.fori_loop(0, qi, body, 0)
    o_ref[0] = acc_s[...]


def _sb_prompt(q, kv, *, tq):
    b, l, _ = q.shape
    kern = functools.partial(_sb_prompt_kernel, tq=tq)
    return pl.pallas_call(
        kern,
        grid=(b, l // tq),
        in_specs=[pl.BlockSpec((1, tq, 256), lambda i, t: (i, t, 0)),
                  pl.BlockSpec((1, 512, l), lambda i, t: (i, 0, 0)),
                  pl.BlockSpec((2 * tq, tq), lambda i, t: (0, 0))],
        out_specs=pl.BlockSpec((1, tq, 256), lambda i, t: (i, t, 0)),
        out_shape=jax.ShapeDtypeStruct((b, l, 256), F32),
        scratch_shapes=[pltpu.VMEM((tq, 256), F32), pltpu.VMEM((N_HEADS, tq, 1), F32),
                        pltpu.VMEM((N_HEADS, tq, tq), F32), pltpu.VMEM((N_HEADS, tq, tq), F32),
                        pltpu.VMEM((N_HEADS, tq, 2 * tq), BF16), pltpu.VMEM((N_HEADS, tq, tq), F32),
                        pltpu.VMEM((N_HEADS, tq, tq), BF16)],
        compiler_params=_params("parallel", "arbitrary"),
    )(q, kv, jnp.concatenate([_later_keys_matrix(tq)] * 2, axis=0))


SB_PAGE_GROUP = 8


def _sb_sample_kernel(pt_ref, qbd_ref, new_ref, later_ref, *rest, pg, n_pg, pos0):
    pages = rest[:pg]
    o_ref, acc_s, aft_s = rest[pg:]
    g = pl.program_id(1)
    qbd = (qbd_ref[0] * SCALE).astype(BF16)
    later = later_ref[...]
    rows = qbd.shape[0]

    def blocks(kv_refs, mask):
        n = len(kv_refs)
        z = jnp.concatenate([jnp.dot(qbd, r[0, 0:256, :].astype(BF16), preferred_element_type=F32)
                             for r in kv_refs], axis=0)
        lf = -(jnp.maximum(z, 0.0) + jnp.log(1.0 + jnp.exp(-jnp.abs(z))))
        if mask is not None:
            lf = jnp.where(mask, lf, 0.0)
        loc = _split_dot(lf, later)
        aft = aft_s[...]
        acc = acc_s[...]
        for i, r in enumerate(kv_refs):
            blk = slice(i * rows, (i + 1) * rows)
            a = jnp.exp(z[blk] + lf[blk] + (loc[blk] + aft))
            if mask is not None:
                a = jnp.where(mask, a, 0.0)
            acc = acc + _bdot_nt(a, r[0, 256:512, :])
            aft = aft + (loc[blk, 0:1] + lf[blk, 0:1])
        acc_s[...] = acc
        aft_s[...] = aft

    @pl.when(g == 0)
    def _():
        acc_s[...] = jnp.zeros_like(acc_s)
        aft_s[...] = jnp.zeros_like(aft_s)
        t = lax.broadcasted_iota(jnp.int32, (rows, PAGE_SIZE), 0) % SUBLANES
        key = lax.broadcasted_iota(jnp.int32, (rows, PAGE_SIZE), 1)
        blocks([new_ref], key < t)

    for i0 in range(0, pg, SB_PAGE_GROUP):
        blocks(pages[i0:i0 + SB_PAGE_GROUP], None)

    @pl.when(g == n_pg - 1)
    def _():
        acc = acc_s[...]
        lane_head = lax.broadcasted_iota(jnp.int32, (SUBLANES, 256), 1) // HEAD_DIM
        out = jnp.zeros((SUBLANES, 256), F32)
        for h in range(N_HEADS):
            out = out + jnp.where(lane_head == h, acc[h * SUBLANES:(h + 1) * SUBLANES], 0.0)
        o_ref[0] = out


def _sb_sample(page_table, qbd, new_rows, cache, layer_base, *, pg, pos0):
    b, n_pages = page_table.shape
    n_pg = n_pages // pg
    kern = functools.partial(_sb_sample_kernel, pg=pg, n_pg=n_pg, pos0=pos0)

    def page_spec(i):
        return pl.BlockSpec((1, 512, PAGE_SIZE),
                            lambda bi, g, pt: (layer_base + pt[bi, (n_pg - 1 - g) * pg + (pg - 1 - i)], 0, 0))

    grid_spec = pltpu.PrefetchScalarGridSpec(
        num_scalar_prefetch=1,
        grid=(b, n_pg),
        in_specs=[pl.BlockSpec((1, 32, 256), lambda bi, g, pt: (bi, 0, 0)),
                  pl.BlockSpec((1, 512, PAGE_SIZE), lambda bi, g, pt: (bi, 0, 0)),
                  pl.BlockSpec((PAGE_SIZE, PAGE_SIZE), lambda bi, g, pt: (0, 0))]
                 + [page_spec(i) for i in range(pg)],
        out_specs=pl.BlockSpec((1, SUBLANES, 256), lambda bi, g, pt: (bi, 0, 0)),
        scratch_shapes=[pltpu.VMEM((32, 256), F32), pltpu.VMEM((32, 1), F32)],
    )
    return pl.pallas_call(
        kern,
        grid_spec=grid_spec,
        out_shape=jax.ShapeDtypeStruct((b, SUBLANES, 256), F32),
        compiler_params=_params("parallel", "arbitrary"),
    )(page_table, qbd, new_rows, _later_keys_matrix(PAGE_SIZE), *([cache] * pg))


CMP_PAGE_GROUP = 16


def _nsa_sample_cmp_kernel(pt_ref, q_ref, pea_ref, peb_ref, wa_ref, wb_ref, w2_ref, *rest, n_pages, pos0, nsp):
    pages = rest[:n_pages]
    ocmp_ref, sel_ref, xs_ref = rest[n_pages:]
    pas, pbs = [], []
    for c in range(0, n_pages, CMP_PAGE_GROUP):
        npg = min(CMP_PAGE_GROUP, n_pages - c)
        for i in range(npg):
            for half in range(2):
                xs_ref[half, i * PAGE_SIZE:(i + 1) * PAGE_SIZE, :] = pages[c + i][0, half * LANES:(half + 1) * LANES, :].T
        rows = npg * (PAGE_SIZE // CMP_STRIDE)
        x = jnp.concatenate([xs_ref[half, pl.ds(l, rows, stride=CMP_STRIDE), :]
                             for l in range(CMP_STRIDE) for half in range(2)], axis=1)
        pa, pb = _compress_tokens(x, pea_ref[...], peb_ref[...], wa_ref[...], wb_ref[...], None)
        pas.append(pa)
        pbs.append(pb)
    kcvc = _compress_finish(jnp.concatenate(pas, axis=0), jnp.concatenate(pbs, axis=0), w2_ref[...])
    m = kcvc.shape[0]
    nc = m - 1
    ns = 2 * n_pages + 1
    q = q_ref[0]
    tq = q.shape[0]
    qpos = pos0 + lax.broadcasted_iota(jnp.int32, (tq, 1), 0)
    qpos2 = jnp.concatenate([qpos, qpos], axis=0)
    n_id = lax.broadcasted_iota(jnp.int32, (2 * tq, m), 1)
    cmask = (n_id * CMP_STRIDE + CMP_LEN - 1 <= qpos2) & (n_id < nc)
    ovl = _overlap_matrix(m, nsp, nc, ns)
    for g in range(NSA_KV_HEADS):
        h0 = g * NSA_GROUP
        qs = jnp.concatenate([q[:, h0 * 64:(h0 + 1) * 64], q[:, (h0 + 1) * 64:(h0 + 2) * 64]], axis=0)
        pc = _masked_softmax(_bdot_nt(qs, kcvc[:, g * 64:(g + 1) * 64]) * SCALE, cmask)
        o_cmp = _bdot(pc, kcvc[:, 128 + g * 64:128 + (g + 1) * 64])
        sel_ref[0, g] = _select_bias(_fdot(pc[:tq] + pc[tq:], ovl), qpos, ns)
        for hg in range(NSA_GROUP):
            ocmp_ref[0, :, (h0 + hg) * 64:(h0 + hg + 1) * 64] = o_cmp[hg * tq:(hg + 1) * tq]


def _nsa_sample_cmp(page_table, q, pea, peb, wa, wb, w2, cache, layer_base, *, pos0):
    b, n_pages = page_table.shape
    nsp = -(-(2 * n_pages + 1) // LANES) * LANES
    kern = functools.partial(_nsa_sample_cmp_kernel, n_pages=n_pages, pos0=pos0, nsp=nsp)
    full = lambda a: pl.BlockSpec(a.shape, lambda bi, pt: (0,) * a.ndim)

    def page_spec(i):
        return pl.BlockSpec((1, 256, PAGE_SIZE), lambda bi, pt: (layer_base + pt[bi, i], 0, 0))

    grid_spec = pltpu.PrefetchScalarGridSpec(
        num_scalar_prefetch=1,
        grid=(b,),
        in_specs=[pl.BlockSpec((1, SUBLANES, 256), lambda bi, pt: (bi, 0, 0)),
                  full(pea), full(peb), full(wa), full(wb), full(w2)] + [page_spec(i) for i in range(n_pages)],
        out_specs=[pl.BlockSpec((1, SUBLANES, 256), lambda bi, pt: (bi, 0, 0)),
                   pl.BlockSpec((1, NSA_KV_HEADS, SUBLANES, nsp), lambda bi, pt: (bi, 0, 0, 0))],
        scratch_shapes=[pltpu.VMEM((2, CMP_PAGE_GROUP * PAGE_SIZE, LANES), F32)],
    )
    return pl.pallas_call(
        kern,
        grid_spec=grid_spec,
        out_shape=[jax.ShapeDtypeStruct((b, SUBLANES, 256), F32),
                   jax.ShapeDtypeStruct((b, NSA_KV_HEADS, SUBLANES, nsp), F32)],
        compiler_params=_params("parallel"),
    )(page_table, q, pea, peb, wa, wb, w2, *([cache] * n_pages))


def _softmax_rows(s):
    e = jnp.exp(s - jnp.max(s, axis=-1, keepdims=True))
    return e * (1.0 / jnp.sum(e, axis=-1, keepdims=True))


def _nsa_sample_slc_kernel(pt_ref, q_ref, sm_ref, ocmp_ref, sel_ref, newslc_ref, pastwin_ref, newwin_ref,
                           *rest, n_pages, pos0):
    pages = rest[:n_pages]
    o_ref, z_s = rest[n_pages:]
    q = q_ref[0] * SCALE
    tq = q.shape[0]
    gates = _sigmoid(sm_ref[0])
    nk = (n_pages + 1) * PAGE_SIZE
    past = pastwin_ref.shape[2]
    t2 = lax.broadcasted_iota(jnp.int32, (2 * tq, 1), 0) % tq
    kposw = jnp.concatenate([lax.broadcasted_iota(jnp.int32, (2 * tq, past), 1) - past,
                             lax.broadcasted_iota(jnp.int32, (2 * tq, PAGE_SIZE), 1)], axis=1)
    dwin = t2 - kposw
    wbias = jnp.where((dwin >= 0) & (dwin < WINDOW), 0.0, NEG)
    lane = lax.broadcasted_iota(jnp.int32, (2 * tq, PAGE_SIZE), 1)
    lower_block = lane < SLC_BLOCK
    new_causal = jnp.where(lane <= t2, 0.0, NEG)
    for g in range(NSA_KV_HEADS):
        h0 = g * NSA_GROUP
        krow = slice(g * 64, (g + 1) * 64)
        vrow = slice(128 + g * 64, 128 + (g + 1) * 64)
        qs = jnp.concatenate([q[:, h0 * 64:(h0 + 1) * 64], q[:, (h0 + 1) * 64:(h0 + 2) * 64]], axis=0).astype(BF16)
        bias = jnp.concatenate([sel_ref[0, g], sel_ref[0, g]], axis=0)
        dot = lambda k_t: jnp.dot(qs, k_t.astype(BF16), preferred_element_type=F32)
        for p in range(n_pages):
            z_s[:, p * PAGE_SIZE:(p + 1) * PAGE_SIZE] = dot(pages[p][0, krow, :]) + jnp.where(
                lower_block, bias[:, 2 * p:2 * p + 1], bias[:, 2 * p + 1:2 * p + 2])
        z_s[:, n_pages * PAGE_SIZE:nk] = dot(newslc_ref[0, krow, :]) + bias[:, 2 * n_pages:2 * n_pages + 1] + new_causal
        p2 = _softmax_rows(z_s[...])
        o_slc = _bdot_nt(p2[:, n_pages * PAGE_SIZE:nk], newslc_ref[0, vrow, :])
        for p in range(n_pages):
            o_slc = o_slc + _bdot_nt(p2[:, p * PAGE_SIZE:(p + 1) * PAGE_SIZE], pages[p][0, vrow, :])
        p3 = _softmax_rows(jnp.concatenate([dot(pastwin_ref[0, krow, :]), dot(newwin_ref[0, krow, :])], axis=1) + wbias)
        o_win = _bdot_nt(p3[:, 0:past], pastwin_ref[0, vrow, :]) + _bdot_nt(p3[:, past:], newwin_ref[0, vrow, :])
        for hg in range(NSA_GROUP):
            head = h0 + hg
            rows = slice(hg * tq, (hg + 1) * tq)
            c0 = 8 + 3 * head
            o_ref[0, :, head * 64:(head + 1) * 64] = (gates[:, c0:c0 + 1] * ocmp_ref[0, :, head * 64:(head + 1) * 64]
                                                      + gates[:, c0 + 1:c0 + 2] * o_slc[rows]
                                                      + gates[:, c0 + 2:c0 + 3] * o_win[rows])


def _nsa_sample_slc(page_table, q, small, o_cmp, sel, new_slc, past_win, past_win_base, new_win, cache, layer_base,
                    *, pos0):
    b, n_pages = page_table.shape
    nsp = sel.shape[-1]
    nk = (n_pages + 1) * PAGE_SIZE
    past = past_win.shape[2]
    kern = functools.partial(_nsa_sample_slc_kernel, n_pages=n_pages, pos0=pos0)
    per_b = lambda r, wd: pl.BlockSpec((1, r, wd), lambda bi, pt: (bi, 0, 0))

    def page_spec(i):
        return pl.BlockSpec((1, 256, PAGE_SIZE), lambda bi, pt: (layer_base + pt[bi, i], 0, 0))

    grid_spec = pltpu.PrefetchScalarGridSpec(
        num_scalar_prefetch=1,
        grid=(b,),
        in_specs=[per_b(SUBLANES, 256), per_b(SUBLANES, LANES), per_b(SUBLANES, 256),
                  pl.BlockSpec((1, NSA_KV_HEADS, SUBLANES, nsp), lambda bi, pt: (bi, 0, 0, 0)),
                  per_b(256, PAGE_SIZE),
                  pl.BlockSpec((1, 256, past), lambda bi, pt: (past_win_base + bi, 0, 0)),
                  per_b(256, PAGE_SIZE)] + [page_spec(i) for i in range(n_pages)],
        out_specs=per_b(SUBLANES, 256),
        scratch_shapes=[pltpu.VMEM((2 * SUBLANES, nk), F32)],
    )
    return pl.pallas_call(
        kern,
        grid_spec=grid_spec,
        out_shape=jax.ShapeDtypeStruct((b, SUBLANES, 256), F32),
        compiler_params=_params("parallel"),
    )(page_table, q, small, o_cmp, sel, new_slc, past_win, new_win, *([cache] * n_pages))


def _prep_layer(l, P):
    w_in = P["w_in"][l]
    col = lambda lo, hi: w_in[:, lo:hi]
    small = jnp.concatenate([col(1024, 1032), col(2568, 2580), jnp.zeros((D_MODEL, LANES - 20), F32)], axis=1)
    w_perm = jnp.concatenate([col(0, 1024), col(1032, 1544), col(1544, 1800), col(2580, 2836), col(2836, 3348),
                              col(1800, 2568), small], axis=1).astype(BF16)
    par = jnp.zeros((2, LANES), F32).at[0, 4:8].set(P["gdn_a_log"][l]).at[1, 4:8].set(P["gdn_dt_bias"][l])

    def block_diag(w):
        return jnp.einsum("ncd,nm->ncmd", w, jnp.eye(4, dtype=F32)).reshape(GROUP_WIDTH, GROUP_WIDTH).astype(BF16)

    eye2 = jnp.eye(2, dtype=F32)
    w1 = P["nsa_cmp_w1"][l]
    half = CMP_LEN // 2

    def expand_w1(w):
        return jnp.einsum("klde,kK,gG->lkgdKGe", w, eye2, eye2).reshape(half * 256, 256).astype(BF16)

    def expand_pe(pe):
        return jnp.broadcast_to(pe.transpose(1, 0, 2)[:, :, None, :], (half, 2, 2, HEAD_DIM)).reshape(1, half * 256)

    pe = P["nsa_cmp_pe"][l]
    w2 = jnp.einsum("kef,kK,gG->kgeKGf", P["nsa_cmp_w2"][l], eye2, eye2).reshape(256, 256).astype(BF16)
    w_kv_t = jnp.concatenate([col(2836, 3348), col(1800, 2568)], axis=1).T.astype(BF16)
    return dict(
        norm1=P["norm1"][l][None], w_in=w_perm, w_kv_t=w_kv_t,
        gdn_cw=P["gdn_conv_w"][l], gdn_par=par, gdn_norm=P["gdn_norm"][l][None],
        lru_cw=P["lru_conv_w"][l],
        lru_vec=jnp.stack([P["lru_conv_b"][l], P["lru_ba"][l], P["lru_bx"][l], P["lru_lambda"][l]]),
        lru_wa=block_diag(P["lru_wa"][l]), lru_wx=block_diag(P["lru_wx"][l]),
        pea=expand_pe(pe[:, :half]), peb=expand_pe(pe[:, half:]),
        cmp_wa=expand_w1(w1[:, :half]), cmp_wb=expand_w1(w1[:, half:]), cmp_w2=w2,
        w_out=P["w_out"][l].astype(BF16), norm2=P["norm2"][l][None],
        wg=P["ffn_w_gate"][l].astype(BF16), wu=P["ffn_w_up"][l].astype(BF16),
        ffn_cw=P["ffn_conv_w"][l], wd=P["ffn_w_down"][l].astype(BF16),
    )


def _rope_tables(pos):
    half = HEAD_DIM // 2
    inv = ROPE_THETA ** (-jnp.arange(half, dtype=F32) / half)
    ang = pos.astype(F32)[:, None] * inv[None, :]
    c, s = jnp.cos(ang), jnp.sin(ang)
    return jnp.tile(jnp.concatenate([c, c], axis=-1), (1, 2)), jnp.tile(jnp.concatenate([-s, s], axis=-1), (1, 2))


PROMPT_TM = 256
PROMPT_TQ = 128
SB_TQ = 256
GDN_CHUNK = 64


def _layer_prompt(x2d, W, b, l, cos, sin, final, norm_f):
    outs = _in_proj_t(x2d, W["norm1"], W["w_in"], W["w_kv_t"], cos, sin, b, l, PROMPT_TM)
    zg, zl, nq, sq, cmp_r, small = [o.reshape(b, l, o.shape[-1]) for o in outs[:6]]
    skv_t, cmp_t, slc_t, win_t = outs[6:]
    o_gdn, gdn_conv, gdn_s = _gdn(zg, small, jnp.zeros((b, SUBLANES, 768), F32), jnp.zeros((b, N_HEADS, 64, 64), F32),
                                  W["gdn_cw"], W["gdn_par"], W["gdn_norm"], tl=PROMPT_TM, chunk=GDN_CHUNK,
                                  l_real=PROMPT_TM)
    o_lru, lru_conv, lru_h = _lru(zl, jnp.zeros((b, SUBLANES, GROUP_WIDTH), F32), jnp.zeros((b, 1, GROUP_WIDTH), F32),
                                  W["lru_cw"], W["lru_vec"], W["lru_wa"], W["lru_wx"], tl=PROMPT_TM, l_real=PROMPT_TM)
    kcvc = _cmp_prompt(cmp_r.reshape(b, l // CMP_STRIDE, CMP_STRIDE * 256), W["pea"], W["peb"],
                       W["cmp_wa"], W["cmp_wb"], W["cmp_w2"])
    o_nsa = _nsa_prompt(nq, small, kcvc, slc_t, win_t, tq=PROMPT_TQ)
    o_sb = _sb_prompt(sq, skv_t, tq=min(SB_TQ, l))
    mixes = [o.reshape(b * l, GROUP_WIDTH) for o in (o_gdn, o_lru, o_nsa, o_sb)]
    res = _out_ffn(x2d, mixes, W["w_out"], W["norm2"], W["wg"], W["wu"], W["ffn_cw"], W["wd"],
                   jnp.zeros((SUBLANES, D_FF), F32), norm_f, tm=PROMPT_TM, stride=1, tiles_per_seq=l // PROMPT_TM,
                   final=final)
    kv5 = lambda a, h: jnp.transpose(a.reshape(b, 2, h, HEAD_DIM, a.shape[-1]), (0, 4, 1, 2, 3))
    wkeep = min(WINDOW, l)
    states = (kv5(cmp_t, 2), kv5(slc_t, 2), kv5(skv_t, 4), kv5(win_t[:, :, l - wkeep:], 2), gdn_conv, gdn_s, lru_conv,
              lru_h[:, 0], res[1][l // PROMPT_TM - 1::l // PROMPT_TM])
    return res[0], states, (res[2] if final else None)


def _layer_sample(x2d, W, layer, S, cos, sin, final, norm_f):
    page_table = S["page_table"]
    b, n_pages = page_table.shape
    t_new = x2d.shape[0] // b
    pos0 = n_pages * PAGE_SIZE
    n_phys = S["n_phys"]
    outs = _in_proj(x2d, W["norm1"], W["w_in"], cos, sin, x2d.shape[0], 1)

    def to_batch_major(a, rows=SUBLANES):
        a = a.reshape(t_new, b, a.shape[-1]).transpose(1, 0, 2)
        return jnp.pad(a, ((0, 0), (0, rows - t_new), (0, 0)))

    zg, zl, nq, sq, skv, cmp_r, slc_r, win_r, small = [to_batch_major(o) for o in outs]
    front = lambda st: jnp.pad(st, ((0, 0), (SUBLANES - st.shape[1], 0), (0, 0)))
    o_gdn, gdn_conv, gdn_s = _gdn(zg, small, front(S["gdn_conv"][layer]), S["gdn"][layer],
                                  W["gdn_cw"], W["gdn_par"], W["gdn_norm"], tl=SUBLANES, chunk=SUBLANES, l_real=t_new)
    o_lru, lru_conv, lru_h = _lru(zl, front(S["lru_conv"][layer]), S["lru"][layer][:, None, :],
                                  W["lru_cw"], W["lru_vec"], W["lru_wa"], W["lru_wx"], tl=SUBLANES, l_real=t_new)
    base = layer * n_phys
    o_cmp, sel = _nsa_sample_cmp(page_table, nq, W["pea"], W["peb"], W["cmp_wa"], W["cmp_wb"], W["cmp_w2"],
                                 S["cmp_pages"], base, pos0=pos0)
    pad_page = lambda a: jnp.pad(a.transpose(0, 2, 1), ((0, 0), (0, 0), (0, PAGE_SIZE - a.shape[1])))
    o_nsa = _nsa_sample_slc(page_table, nq, small, o_cmp, sel, pad_page(slc_r), S["win"], layer * b, pad_page(win_r),
                            S["slc_pages"], base, pos0=pos0)
    head_of_col = jnp.arange(256) // HEAD_DIM
    head_of_row = jnp.arange(N_HEADS * SUBLANES) // SUBLANES
    qbd = jnp.where(head_of_row[:, None] == head_of_col[None, :], jnp.tile(sq, (1, N_HEADS, 1)), 0.0)
    o_sb = _sb_sample(page_table, qbd, pad_page(skv), S["sb_pages"], base, pg=min(32, n_pages), pos0=pos0)

    def to_time_major(a):
        return a[:, :t_new].transpose(1, 0, 2).reshape(t_new * b, a.shape[-1])

    mixes = [to_time_major(o) for o in (o_gdn, o_lru, o_nsa, o_sb)]
    buf = S["ffn_conv"][layer].transpose(1, 0, 2).reshape(2 * b, D_FF)
    res = _out_ffn(x2d, mixes, W["w_out"], W["norm2"], W["wg"], W["wu"], W["ffn_cw"], W["wd"], buf, norm_f,
                   tm=x2d.shape[0], stride=b, tiles_per_seq=1, final=final)
    kv5 = lambda a, h: a[:, :t_new].reshape(b, t_new, 2, h, HEAD_DIM)
    ffn_state = res[1].reshape(2, b, D_FF).transpose(1, 0, 2)
    states = (kv5(cmp_r, 2), kv5(slc_r, 2), kv5(skv, 4), kv5(win_r, 2), gdn_conv, gdn_s, lru_conv, lru_h[:, 0], ffn_state)
    return res[0], states, (res[2] if final else None)


def _sample_state_views(cache_cmp_kv, cache_slc_kv, cache_sb_kv, cache_win_kv):
    def view(c):
        d0, d1, tok, kv, h, d = c.shape
        return jnp.transpose(c, (0, 1, 3, 4, 5, 2)).reshape(d0 * d1, kv * h * d, tok)

    return dict(cmp_pages=view(cache_cmp_kv), slc_pages=view(cache_slc_kv), sb_pages=view(cache_sb_kv),
                win=view(cache_win_kv))


def kernel(x_prompt, x_sample, cache_cmp_kv, cache_slc_kv, cache_sb_kv, cache_win_kv, state_gdn_conv, state_gdn, state_lru_conv, state_lru, state_ffn_conv, page_table, norm1, w_in, gdn_conv_w, gdn_a_log, gdn_dt_bias, gdn_norm, lru_conv_w, lru_conv_b, lru_wa, lru_ba, lru_wx, lru_bx, lru_lambda, nsa_cmp_pe, nsa_cmp_w1, nsa_cmp_w2, w_out, norm2, ffn_w_gate, ffn_w_up, ffn_conv_w, ffn_w_down, norm_f):
    P = dict(norm1=norm1, w_in=w_in, gdn_conv_w=gdn_conv_w, gdn_a_log=gdn_a_log, gdn_dt_bias=gdn_dt_bias,
             gdn_norm=gdn_norm, lru_conv_w=lru_conv_w, lru_conv_b=lru_conv_b, lru_wa=lru_wa, lru_ba=lru_ba,
             lru_wx=lru_wx, lru_bx=lru_bx, lru_lambda=lru_lambda, nsa_cmp_pe=nsa_cmp_pe, nsa_cmp_w1=nsa_cmp_w1,
             nsa_cmp_w2=nsa_cmp_w2, w_out=w_out, norm2=norm2, ffn_w_gate=ffn_w_gate, ffn_w_up=ffn_w_up,
             ffn_conv_w=ffn_conv_w, ffn_w_down=ffn_w_down)
    depth = w_in.shape[0]
    weights = [_prep_layer(l, P) for l in range(depth)]
    gf = norm_f[None]

    b, l, _ = x_prompt.shape
    cos, sin = _rope_tables(jnp.arange(l))
    h = x_prompt.reshape(b * l, D_MODEL)
    p_states, y_prompt = [], None
    for layer in range(depth):
        h, st, y = _layer_prompt(h, weights[layer], b, l, cos, sin, layer == depth - 1, gf)
        p_states.append(st)
        y_prompt = y
    y_prompt = y_prompt.reshape(b, l, D_MODEL)

    db, t_new, _ = x_sample.shape
    n_phys = cache_sb_kv.shape[1]
    n_pages = page_table.shape[1]
    pos0 = n_pages * PAGE_SIZE
    past_win = cache_win_kv.shape[2]
    S = _sample_state_views(cache_cmp_kv, cache_slc_kv, cache_sb_kv, cache_win_kv)
    S.update(page_table=page_table, n_phys=n_phys,
             gdn_conv=state_gdn_conv, gdn=state_gdn, lru_conv=state_lru_conv, lru=state_lru, ffn_conv=state_ffn_conv)
    cos_s, sin_s = _rope_tables(pos0 + jnp.repeat(jnp.arange(t_new), db))
    h = x_sample.transpose(1, 0, 2).reshape(t_new * db, D_MODEL)
    s_states, y_sample = [], None
    for layer in range(depth):
        h, st, y = _layer_sample(h, weights[layer], layer, S, cos_s, sin_s, layer == depth - 1, gf)
        s_states.append(st)
        y_sample = y
    y_sample = y_sample.reshape(t_new, db, D_MODEL).transpose(1, 0, 2)

    stk = lambda states, i: jnp.stack([s[i] for s in states])
    return ((y_prompt, y_sample) + tuple(stk(p_states, i) for i in range(9))
            + tuple(stk(s_states, i) for i in range(9)))
```

```python
import functools
import math

import jax
import jax.numpy as jnp
from jax import lax
from jax.experimental import pallas as pl
from jax.experimental.pallas import tpu as pltpu

F32, BF16 = jnp.float32, jnp.bfloat16

D_MODEL = 1024
HEAD_DIM = 64
GROUP_WIDTH = 256
N_HEADS = 4
NSA_KV_HEADS = 2
NSA_GROUP = 2
D_FF = 2816
SHORT_CONV = 4
FFN_CONV = 3
LRU_C = 8.0
CMP_STRIDE = 16
CMP_LEN = 32
SLC_BLOCK = 64
SLC_TOPK = 16
N_LOCAL = 2
WINDOW = 512
FORCE_BONUS = 1.0e3
ROPE_THETA = 10000.0
EPS = 1e-6
NEG = -1e30
SCALE = HEAD_DIM ** -0.5
PAGE_SIZE = 128

VMEM_LIMIT_BYTES = 56 * 1024 * 1024
LANES = 128
SUBLANES = 8

IN_WIDTHS = (1024, 512, 256, 256, 512, 256, 256, 256, 128)
D_IN_PAD = sum(IN_WIDTHS)


def _params(*sem):
    return pltpu.CompilerParams(dimension_semantics=sem, vmem_limit_bytes=VMEM_LIMIT_BYTES)


def _bdot(a, b):
    return jnp.dot(a.astype(BF16), b.astype(BF16), preferred_element_type=F32)


def _bdot_nt(a, b):
    return lax.dot_general(a.astype(BF16), b.astype(BF16), (((1,), (1,)), ((), ())), preferred_element_type=F32)


def _bdot_tn(a, b):
    return lax.dot_general(a.astype(BF16), b.astype(BF16), (((0,), (0,)), ((), ())), preferred_element_type=F32)


def _fdot(a, b):
    return jnp.dot(a, b, preferred_element_type=F32, precision=lax.Precision.HIGHEST)


def _dot3(a, b):
    ah, bh = a.astype(BF16), b.astype(BF16)
    al, bl = (a - ah.astype(F32)).astype(BF16), (b - bh.astype(F32)).astype(BF16)
    dot = lambda x, y: jnp.dot(x, y, preferred_element_type=F32)
    return dot(ah, bh) + (dot(ah, bl) + dot(al, bh))


GDN_GROUP_CHUNKS = 2


def _split_dot(a, b_exact):
    hi = a.astype(BF16)
    lo = (a - hi.astype(F32)).astype(BF16)
    return (jnp.dot(hi, b_exact, preferred_element_type=F32) + jnp.dot(lo, b_exact, preferred_element_type=F32))


def _sigmoid(x):
    return jax.nn.sigmoid(x)


def _silu(x):
    return x * jax.nn.sigmoid(x)


def _softplus(x):
    return jnp.maximum(x, 0.0) + jnp.log1p(jnp.exp(-jnp.abs(x)))


def _gelu_tanh(x):
    return 0.5 * x * (1.0 + jnp.tanh(math.sqrt(2.0 / math.pi) * (x + 0.044715 * (x * x * x))))


def _rms(x, g):
    return x * lax.rsqrt(jnp.mean(x * x, axis=-1, keepdims=True) + EPS) * g


def _rope_pairs(v, cos, sin):
    lane = lax.broadcasted_iota(jnp.int32, v.shape, 1)
    first_half = (lane % HEAD_DIM) < (HEAD_DIM // 2)
    partner = jnp.where(first_half, pltpu.roll(v, LANES - HEAD_DIM // 2, 1), pltpu.roll(v, HEAD_DIM // 2, 1))
    return v * cos + partner * sin


def _masked_softmax(s, mask):
    sm = jnp.where(mask, s, NEG)
    m = jnp.max(sm, axis=-1, keepdims=True)
    e = jnp.where(mask, jnp.exp(sm - m), 0.0)
    den = jnp.sum(e, axis=-1, keepdims=True)
    return e * (1.0 / jnp.where(den > 0.0, den, 1.0))


def _in_proj_kernel(x_ref, g_ref, w_ref, cos_ref, sin_ref,
                    zg_ref, zl_ref, nq_ref, sq_ref, skv_ref, cmp_ref, slc_ref, win_ref, sm_ref):
    xb = _rms(x_ref[...], g_ref[...]).astype(BF16)
    cos, sin = cos_ref[...], sin_ref[...]

    def mm(lo, width):
        return jnp.dot(xb, w_ref[:, lo:lo + width], preferred_element_type=F32)

    zg_ref[...] = mm(0, 1024)
    zl_ref[...] = mm(1024, 512)
    q = mm(1536, 256)
    nq_ref[:, 0:128] = _rope_pairs(q[:, 0:128], cos, sin)
    nq_ref[:, 128:256] = _rope_pairs(q[:, 128:256], cos, sin)
    sq_ref[...] = mm(1792, 256)
    skv_ref[...] = mm(2048, 512)
    for ref, lo in ((cmp_ref, 2560), (slc_ref, 2816), (win_ref, 3072)):
        kv = mm(lo, 256)
        ref[:, 0:128] = _rope_pairs(kv[:, 0:128], cos, sin)
        ref[:, 128:256] = kv[:, 128:256]
    sm_ref[...] = mm(3328, 128)


def _in_proj(x2d, g, w, cos, sin, tm, table_tiles):
    n = x2d.shape[0]
    nt = n // tm
    row = lambda wd: pl.BlockSpec((tm, wd), lambda i: (i, 0))
    tab = pl.BlockSpec((tm, LANES), lambda i: (i % table_tiles, 0))
    return pl.pallas_call(
        _in_proj_kernel,
        grid=(nt,),
        in_specs=[row(D_MODEL), pl.BlockSpec((1, D_MODEL), lambda i: (0, 0)),
                  pl.BlockSpec((D_MODEL, D_IN_PAD), lambda i: (0, 0)), tab, tab],
        out_specs=[row(wd) for wd in IN_WIDTHS],
        out_shape=[jax.ShapeDtypeStruct((n, wd), F32) for wd in IN_WIDTHS],
        compiler_params=_params("parallel"),
    )(x2d, g, w, cos, sin)


def _rope_rows(v, cos_t, sin_t):
    q = HEAD_DIM // 2
    partner = jnp.concatenate([v[q:2 * q], v[0:q], v[3 * q:4 * q], v[2 * q:3 * q]], axis=0)
    return v * cos_t + partner * sin_t


KV_T_ROWS = 1280


def _in_proj_t_kernel(x_ref, g_ref, w_ref, wt_ref, cos_ref, sin_ref, cost_ref, sint_ref,
                      zg_ref, zl_ref, nq_ref, sq_ref, cmp_ref, sm_ref, skvt_ref, cmpt_ref, slct_ref, wint_ref):
    xb = _rms(x_ref[...], g_ref[...]).astype(BF16)
    cos, sin = cos_ref[...], sin_ref[...]
    cos_t, sin_t = cost_ref[...], sint_ref[...]

    def mm(lo, width):
        return jnp.dot(xb, w_ref[:, lo:lo + width], preferred_element_type=F32)

    def mm_t(lo, rows):
        return lax.dot_general(wt_ref[lo:lo + rows, :], xb, (((1,), (1,)), ((), ())), preferred_element_type=F32)

    zg_ref[...] = mm(0, 1024)
    zl_ref[...] = mm(1024, 512)
    q = mm(1536, 256)
    nq_ref[:, 0:128] = _rope_pairs(q[:, 0:128], cos, sin)
    nq_ref[:, 128:256] = _rope_pairs(q[:, 128:256], cos, sin)
    sq_ref[...] = mm(1792, 256)
    kv = mm(2560, 256)
    cmp_ref[:, 0:128] = _rope_pairs(kv[:, 0:128], cos, sin)
    cmp_ref[:, 128:256] = kv[:, 128:256]
    sm_ref[...] = mm(3328, 128)
    skvt_ref[0] = mm_t(0, 512)
    for ref, lo in ((cmpt_ref, 512), (slct_ref, 768), (wint_ref, 1024)):
        kvt = mm_t(lo, 256)
        ref[0, 0:128, :] = _rope_rows(kvt[0:128], cos_t, sin_t)
        ref[0, 128:256, :] = kvt[128:256]


def _in_proj_t(x2d, g, w, wt, cos, sin, b, l, tm):
    tps = l // tm
    row = lambda wd: pl.BlockSpec((tm, wd), lambda i: (i, 0))
    tab = pl.BlockSpec((tm, LANES), lambda i: (i % tps, 0))
    tab_t = pl.BlockSpec((LANES, tm), lambda i: (0, i % tps))
    kvt = lambda c: pl.BlockSpec((1, c, tm), lambda i: (i // tps, 0, i % tps))
    row_widths = (1024, 512, 256, 256, 256, 128)
    t_rows = (512, 256, 256, 256)
    return pl.pallas_call(
        _in_proj_t_kernel,
        grid=(b * tps,),
        in_specs=[row(D_MODEL), pl.BlockSpec((1, D_MODEL), lambda i: (0, 0)),
                  pl.BlockSpec((D_MODEL, D_IN_PAD), lambda i: (0, 0)),
                  pl.BlockSpec((KV_T_ROWS, D_MODEL), lambda i: (0, 0)), tab, tab, tab_t, tab_t],
        out_specs=[row(wd) for wd in row_widths] + [kvt(c) for c in t_rows],
        out_shape=[jax.ShapeDtypeStruct((b * l, wd), F32) for wd in row_widths]
                  + [jax.ShapeDtypeStruct((b, c, l), F32) for c in t_rows],
        compiler_params=_params("parallel"),
    )(x2d, g, w, wt, cos, sin, cos.T, sin.T)


FFN_SLABS = 2


def _out_ffn_kernel(*refs, tm, tf, nf, stride, prev_rows, tiles_per_seq, final):
    x_ref, mix_refs, refs = refs[0], refs[1:5], refs[5:]
    if final:
        (wo_ref, g2_ref, wg_ref, wu_ref, cw_ref, wd_ref, buf_ref, gf_ref,
         xo_ref, st_ref, y_ref, carry_s) = refs
    else:
        (wo_ref, g2_ref, wg_ref, wu_ref, cw_ref, wd_ref, buf_ref,
         xo_ref, st_ref, carry_s) = refs
    i = pl.program_id(0)
    mix = jnp.concatenate([m[...] for m in mix_refs], axis=1)
    x1 = x_ref[...] + _bdot(mix, wo_ref[...])
    xn = _rms(x1, g2_ref[...]).astype(BF16)
    seq_start = (i % tiles_per_seq) == 0
    p, s = prev_rows, stride
    x2 = x1
    for j in range(nf):
        cols = slice(j * tf, (j + 1) * tf)
        a = jnp.dot(xn, wg_ref[:, cols], preferred_element_type=F32)
        u = jnp.dot(xn, wu_ref[:, cols], preferred_element_type=F32)
        prev = jnp.where(seq_start, buf_ref[:, cols], carry_s[:, cols])
        ext = jnp.concatenate([prev, a], axis=0)
        cw = cw_ref[:, cols]
        ac = cw[0:1] * ext[p - 2 * s:p - 2 * s + tm] + cw[1:2] * ext[p - s:p - s + tm] + cw[2:3] * ext[p:p + tm]
        carry_s[:, cols] = ext[tm:tm + p]
        st_ref[0, :, cols] = ext[tm + p - 2 * s:tm + p]
        x2 = x2 + _bdot(_silu(ac) * u, wd_ref[cols, :])
    xo_ref[...] = x2
    if final:
        y_ref[...] = _rms(x2, gf_ref[...])


def _out_ffn(x2d, mixes, wo, g2, wg, wu, cw, wd, buf, gf, *, tm, stride, tiles_per_seq, final):
    n = x2d.shape[0]
    nt = n // tm
    nf = FFN_SLABS
    tf = D_FF // nf
    prev_rows = buf.shape[0]
    kern = functools.partial(_out_ffn_kernel, tm=tm, tf=tf, nf=nf, stride=stride, prev_rows=prev_rows,
                             tiles_per_seq=tiles_per_seq, final=final)
    tok = pl.BlockSpec((tm, D_MODEL), lambda i: (i, 0))
    part = pl.BlockSpec((tm, GROUP_WIDTH), lambda i: (i, 0))
    const = lambda shape: pl.BlockSpec(shape, lambda i: (0, 0), pipeline_mode=pl.Buffered(1))
    in_specs = [tok, part, part, part, part,
                const((D_MODEL, D_MODEL)), const((1, D_MODEL)), const((D_MODEL, D_FF)), const((D_MODEL, D_FF)),
                const((FFN_CONV, D_FF)), const((D_FF, D_MODEL)), const((prev_rows, D_FF))]
    args = [x2d, *mixes, wo, g2, wg, wu, cw, wd, buf]
    out_specs = [tok, pl.BlockSpec((1, 2 * stride, D_FF), lambda i: (i, 0, 0))]
    out_shape = [jax.ShapeDtypeStruct((n, D_MODEL), F32),
                 jax.ShapeDtypeStruct((nt, 2 * stride, D_FF), F32)]
    if final:
        in_specs.append(const((1, D_MODEL)))
        args.append(gf)
        out_specs.append(tok)
        out_shape.append(jax.ShapeDtypeStruct((n, D_MODEL), F32))
    return pl.pallas_call(
        kern,
        grid=(nt,),
        in_specs=in_specs,
        out_specs=out_specs,
        out_shape=out_shape,
        scratch_shapes=[pltpu.VMEM((prev_rows, D_FF), F32)],
        compiler_params=_params("arbitrary"),
    )(*args)


def _short_conv(x, carry_ref, cw, tl):
    ext = jnp.concatenate([carry_ref[...], x], axis=0)
    y = cw[0:1] * ext[5:5 + tl]
    for i in range(1, SHORT_CONV):
        y = y + cw[i:i + 1] * ext[5 + i:5 + i + tl]
    carry_ref[...] = ext[tl:tl + SUBLANES]
    return y, ext


def _gdn_kernel(zg_ref, sm_ref, cst_ref, s0_ref, cw_ref, par_ref, nrm_ref,
                o_ref, cout_ref, sout_ref, carry_s, state_s, *, tl, chunk, l_real):
    t = pl.program_id(1)

    @pl.when(t == 0)
    def _():
        carry_s[...] = cst_ref[0]
        state_s[...] = s0_ref[0]

    z = zg_ref[0]
    gate = z[:, 768:1024]
    y, ext = _short_conv(z[:, 0:768], carry_s, cw_ref[...], tl)
    cout_ref[0] = ext[SUBLANES + l_real - 3:SUBLANES + l_real]
    y = _silu(y)

    small = sm_ref[0]
    par = par_ref[...]
    beta_c = _sigmoid(small)
    g_c = -jnp.exp(par[0:1]) * _softplus(small + par[1:2])
    if l_real < tl:
        real = lax.broadcasted_iota(jnp.int32, small.shape, 0) < l_real
        beta_c = jnp.where(real, beta_c, 0.0)
        g_c = jnp.where(real, g_c, 0.0)
    ri = lax.broadcasted_iota(jnp.int32, (tl, tl), 0)
    ci = lax.broadcasted_iota(jnp.int32, (tl, tl), 1)
    same = (ri // chunk) == (ci // chunk)
    gc = _fdot(jnp.where(same & (ci <= ri), 1.0, 0.0), g_c)
    grest = _fdot(jnp.where(same & (ci > ri), 1.0, 0.0), g_c)
    gc_t = gc.T

    ii = lax.broadcasted_iota(jnp.int32, (chunk, chunk), 0)
    jj = lax.broadcasted_iota(jnp.int32, (chunk, chunk), 1)
    causal, strict = ii >= jj, ii > jj
    eye = jnp.where(ii == jj, 1.0, 0.0)
    gnorm = nrm_ref[...]
    n_chunks = tl // chunk
    heads = range(N_HEADS)
    S = [state_s[h] for h in heads]
    out_rows = []
    for c0 in range(0, n_chunks, GDN_GROUP_CHUNKS):
        pairs = [(c, h) for c in range(c0, min(c0 + GDN_GROUP_CHUNKS, n_chunks)) for h in heads]
        rows_of = lambda c: slice(c * chunk, (c + 1) * chunk)
        q = [y[rows_of(c), h * 64:(h + 1) * 64] for c, h in pairs]
        k = [y[rows_of(c), 256 + h * 64:256 + (h + 1) * 64] for c, h in pairs]
        v = [y[rows_of(c), 512 + h * 64:512 + (h + 1) * 64] for c, h in pairs]
        q = [x * lax.rsqrt(jnp.sum(x * x, axis=-1, keepdims=True) + EPS) * SCALE for x in q]
        k = [x * lax.rsqrt(jnp.sum(x * x, axis=-1, keepdims=True) + EPS) for x in k]
        beta = [beta_c[rows_of(c), h:h + 1] for c, h in pairs]
        gcc = [gc[rows_of(c), 4 + h:5 + h] for c, h in pairs]
        gcr = [gc_t[4 + h:5 + h, rows_of(c)] for c, h in pairs]
        decay = [jnp.where(causal, jnp.exp(jnp.where(causal, a - b, 0.0)), 0.0) for a, b in zip(gcc, gcr)]
        kb = [a * b for a, b in zip(k, beta)]
        A = [jnp.where(strict, _bdot_nt(a, b) * d, 0.0) for a, b, d in zip(kb, k, decay)]
        T = [eye - a for a in A]
        P = A
        span = 2
        while span < chunk:
            P = [_dot3(p, p) for p in P]
            T = [t + _dot3(t, p) for t, p in zip(T, P)]
            span *= 2
        u = [_bdot(t, a * b) for t, a, b in zip(T, v, beta)]
        w = [_bdot(t, a * jnp.exp(g)) for t, a, g in zip(T, kb, gcc)]
        attn = [_bdot_nt(a, b) * d for a, b, d in zip(q, k, decay)]
        qe = [a * jnp.exp(g) for a, g in zip(q, gcc)]
        kd = [a * jnp.exp(grest[rows_of(c), 4 + h:5 + h]) for a, (c, h) in zip(k, pairs)]
        for i0 in range(0, len(pairs), N_HEADS):
            c = pairs[i0][0]
            v_new = [u[i0 + h] - _bdot(w[i0 + h], S[h]) for h in heads]
            o = [_bdot(qe[i0 + h], S[h]) + _bdot(attn[i0 + h], v_new[h]) for h in heads]
            last = (c + 1) * chunk - 1
            S = [S[h] * jnp.exp(gc[last:last + 1, 4 + h:5 + h]) + _bdot_tn(kd[i0 + h], v_new[h]) for h in heads]
            o = [_rms(o[h], gnorm) * _silu(gate[rows_of(c), h * 64:(h + 1) * 64]) for h in heads]
            out_rows.append(jnp.concatenate(o, axis=1))
    for h in heads:
        state_s[h] = S[h]
        sout_ref[0, h] = S[h]
    o_ref[0] = jnp.concatenate(out_rows, axis=0)


def _gdn(zg, small, conv_state8, s0, cw, par, gnorm, *, tl, chunk, l_real):
    b, l, _ = zg.shape
    nt = l // tl
    kern = functools.partial(_gdn_kernel, tl=tl, chunk=chunk, l_real=l_real)
    return pl.pallas_call(
        kern,
        grid=(b, nt),
        in_specs=[pl.BlockSpec((1, tl, 1024), lambda i, t: (i, t, 0)),
                  pl.BlockSpec((1, tl, LANES), lambda i, t: (i, t, 0)),
                  pl.BlockSpec((1, SUBLANES, 768), lambda i, t: (i, 0, 0)),
                  pl.BlockSpec((1, N_HEADS, 64, 64), lambda i, t: (i, 0, 0, 0)),
                  pl.BlockSpec((SHORT_CONV, 768), lambda i, t: (0, 0)),
                  pl.BlockSpec((2, LANES), lambda i, t: (0, 0)),
                  pl.BlockSpec((1, HEAD_DIM), lambda i, t: (0, 0))],
        out_specs=[pl.BlockSpec((1, tl, GROUP_WIDTH), lambda i, t: (i, t, 0)),
                   pl.BlockSpec((1, 3, 768), lambda i, t: (i, 0, 0)),
                   pl.BlockSpec((1, N_HEADS, 64, 64), lambda i, t: (i, 0, 0, 0))],
        out_shape=[jax.ShapeDtypeStruct((b, l, GROUP_WIDTH), F32),
                   jax.ShapeDtypeStruct((b, 3, 768), F32),
                   jax.ShapeDtypeStruct((b, N_HEADS, 64, 64), F32)],
        scratch_shapes=[pltpu.VMEM((SUBLANES, 768), F32), pltpu.VMEM((N_HEADS, 64, 64), F32)],
        compiler_params=_params("parallel", "arbitrary"),
    )(zg, small, conv_state8, s0, cw, par, gnorm)


def _lru_kernel(zl_ref, cst_ref, h0_ref, cw_ref, vec_ref, wa_ref, wx_ref,
                y_ref, cout_ref, hout_ref, carry_s, h_s, *, tl, l_real):
    t = pl.program_id(1)

    @pl.when(t == 0)
    def _():
        carry_s[...] = cst_ref[0]
        h_s[...] = h0_ref[0]

    z = zl_ref[0]
    gate = z[:, 256:512]
    vec = vec_ref[...]
    xc, ext = _short_conv(z[:, 0:256], carry_s, cw_ref[...], tl)
    cout_ref[0] = ext[SUBLANES + l_real - 3:SUBLANES + l_real]
    xc = xc + vec[0:1]
    r = _sigmoid(_bdot(xc, wa_ref[...]) + vec[1:2])
    i = _sigmoid(_bdot(xc, wx_ref[...]) + vec[2:3])
    log_a = -LRU_C * r * _softplus(-vec[3:4])
    a = jnp.exp(log_a)
    u = jnp.sqrt(1.0 - a * a) * (i * xc)
    rowid = lax.broadcasted_iota(jnp.int32, u.shape, 0)
    u = u + jnp.where(rowid == 0, a * h_s[...], 0.0)
    s = 1
    while s < tl:
        a_sh = jnp.concatenate([jnp.ones((s, GROUP_WIDTH), F32), a[:tl - s]], axis=0)
        u_sh = jnp.concatenate([jnp.zeros((s, GROUP_WIDTH), F32), u[:tl - s]], axis=0)
        u = a * u_sh + u
        a = a * a_sh
        s *= 2
    h_last = u[l_real - 1:l_real]
    h_s[...] = h_last
    hout_ref[0] = h_last
    y_ref[0] = u * _gelu_tanh(gate)


def _lru(zl, conv_state8, h0, cw, vec, wa, wx, *, tl, l_real):
    b, l, _ = zl.shape
    nt = l // tl
    kern = functools.partial(_lru_kernel, tl=tl, l_real=l_real)
    return pl.pallas_call(
        kern,
        grid=(b, nt),
        in_specs=[pl.BlockSpec((1, tl, 512), lambda i, t: (i, t, 0)),
                  pl.BlockSpec((1, SUBLANES, GROUP_WIDTH), lambda i, t: (i, 0, 0)),
                  pl.BlockSpec((1, 1, GROUP_WIDTH), lambda i, t: (i, 0, 0)),
                  pl.BlockSpec((SHORT_CONV, GROUP_WIDTH), lambda i, t: (0, 0)),
                  pl.BlockSpec((4, GROUP_WIDTH), lambda i, t: (0, 0)),
                  pl.BlockSpec((GROUP_WIDTH, GROUP_WIDTH), lambda i, t: (0, 0)),
                  pl.BlockSpec((GROUP_WIDTH, GROUP_WIDTH), lambda i, t: (0, 0))],
        out_specs=[pl.BlockSpec((1, tl, GROUP_WIDTH), lambda i, t: (i, t, 0)),
                   pl.BlockSpec((1, 3, GROUP_WIDTH), lambda i, t: (i, 0, 0)),
                   pl.BlockSpec((1, 1, GROUP_WIDTH), lambda i, t: (i, 0, 0))],
        out_shape=[jax.ShapeDtypeStruct((b, l, GROUP_WIDTH), F32),
                   jax.ShapeDtypeStruct((b, 3, GROUP_WIDTH), F32),
                   jax.ShapeDtypeStruct((b, 1, GROUP_WIDTH), F32)],
        scratch_shapes=[pltpu.VMEM((SUBLANES, GROUP_WIDTH), F32), pltpu.VMEM((1, GROUP_WIDTH), F32)],
        compiler_params=_params("parallel", "arbitrary"),
    )(zl, conv_state8, h0, cw, vec, wa, wx)


def _compress_project(x_kv, pe2, w_kv):
    rows = x_kv.shape[0]
    first = lax.broadcasted_iota(jnp.int32, (2 * SUBLANES, x_kv.shape[1]), 0) < SUBLANES
    p = _bdot(jnp.concatenate([x_kv, jnp.where(first, pe2[0:1], pe2[1:2])], axis=0), w_kv)
    return (p[:rows, 0:128] + p[rows:rows + 1, 0:128],
            p[:rows, 128:256] + p[rows + SUBLANES:rows + SUBLANES + 1, 128:256])


def _compress_finish(pa, pb, w2):
    pb_next = jnp.concatenate([pb[1:], jnp.zeros((1, pb.shape[1]), F32)], axis=0)
    return _bdot(_silu(pa + pb_next), w2)


def _cmp_prompt_kernel(x_ref, pe_ref, w_ref, w2_ref, o_ref):
    x = x_ref[0]
    halves = []
    for kv in range(2):
        x_kv = jnp.concatenate([x[:, t * 256 + kv * 128:t * 256 + (kv + 1) * 128] for t in range(CMP_STRIDE)], axis=1)
        halves.append(_compress_project(x_kv, pe_ref[kv], w_ref[kv]))
    o_ref[0] = _compress_finish(jnp.concatenate([halves[0][0], halves[1][0]], axis=1),
                                jnp.concatenate([halves[0][1], halves[1][1]], axis=1), w2_ref[...])


def _cmp_prompt(xc, pe, w, w2):
    b, m, f = xc.shape
    full = lambda a: pl.BlockSpec(a.shape, lambda i: (0,) * a.ndim)
    return pl.pallas_call(
        _cmp_prompt_kernel,
        grid=(b,),
        in_specs=[pl.BlockSpec((1, m, f), lambda i: (i, 0, 0)), full(pe), full(w), full(w2)],
        out_specs=pl.BlockSpec((1, m, 256), lambda i: (i, 0, 0)),
        out_shape=jax.ShapeDtypeStruct((b, m, 256), F32),
        compiler_params=_params("parallel"),
    )(xc, pe, w, w2)


def _overlap_matrix(m, nsp, nc, ns, transposed=False):
    shape = (nsp, m) if transposed else (m, nsp)
    nn = lax.broadcasted_iota(jnp.int32, shape, 1 if transposed else 0)
    mm = lax.broadcasted_iota(jnp.int32, shape, 0 if transposed else 1)
    ov = (jnp.minimum(nn * CMP_STRIDE + CMP_LEN - 1, mm * SLC_BLOCK + SLC_BLOCK - 1)
          - jnp.maximum(nn * CMP_STRIDE, mm * SLC_BLOCK) + 1)
    return jnp.where((nn < nc) & (mm < ns), jnp.maximum(ov, 0).astype(F32) * (1.0 / CMP_LEN), 0.0)


def _select_bias(imp, qpos, ns):
    rows, nsp = imp.shape
    j = lax.broadcasted_iota(jnp.int32, (rows, nsp), 1)
    cur = qpos // SLC_BLOCK
    valid = (j <= cur) & (j < ns)
    forced = valid & ((j == 0) | (j > cur - N_LOCAL))
    score = jnp.where(valid, imp + jnp.where(forced, FORCE_BONUS, 0.0), NEG)
    score = jnp.where(j < ns, score, -3.0e38)
    cnt = jnp.zeros((rows, nsp), jnp.int32)
    for i in range(ns):
        si = score[:, i:i + 1]
        beats = (si > score) | ((si == score) & (i < j))
        cnt = cnt + beats.astype(jnp.int32)
    return jnp.where((cnt < min(SLC_TOPK, ns)) & valid, 0.0, NEG)


NSA_KEY_CHUNK = 512


NSA_STRIP = 16
NSA_STRIP_UNROLL = 16
SEL_LANES = 64
DEN_ROWS = 16


def _select_bias_t(imp_t, qpos_row, ns):
    nsr, tq = imp_t.shape
    j = lax.broadcasted_iota(jnp.int32, (nsr, tq), 0)
    cur = qpos_row // SLC_BLOCK
    valid = (j <= cur) & (j < ns)
    forced = valid & ((j == 0) | (j > cur - N_LOCAL))
    score = jnp.where(valid, imp_t + jnp.where(forced, FORCE_BONUS, 0.0), NEG)
    score = jnp.where(j < ns, score, -3.0e38)
    cnt = jnp.zeros((nsr, tq), jnp.int32)
    for i in range(ns):
        si = score[i:i + 1, :]
        beats = (si > score) | ((si == score) & (i < j))
        cnt = cnt + beats.astype(jnp.int32)
    return jnp.where((cnt < min(SLC_TOPK, ns)) & valid, 0.0, NEG)


def _nsa_prompt_kernel(q_ref, sm_ref, kc_ref, slc_ref, win_ref, e_ref, o_ref,
                       krhs_s, vslc_s, kwin_s, vwin_s, s_s, e_s, *, tq, l, nc, ns, nsr, wk, kchunk):
    t = pl.program_id(1)
    q0 = t * tq

    @pl.when(t == 0)
    def _():
        ones = jnp.ones((DEN_ROWS, l), BF16)
        for g in range(NSA_KV_HEADS):
            krow = slice(g * 64, (g + 1) * 64)
            vrow = slice(128 + g * 64, 128 + (g + 1) * 64)
            krhs_s[g, 0:64, :] = slc_ref[0, krow, :].astype(BF16)
            krhs_s[g, 64:64 + SEL_LANES, :] = e_ref[...]
            vslc_s[g, 0:64, :] = slc_ref[0, vrow, :].astype(BF16)
            vslc_s[g, 64:64 + DEN_ROWS, :] = ones
            kwin_s[g] = win_ref[0, krow, :].astype(BF16)
            vwin_s[g, 0:64, :] = win_ref[0, vrow, :].astype(BF16)
            vwin_s[g, 64:64 + DEN_ROWS, :] = ones

    q = q_ref[0] * SCALE
    gates = _sigmoid(sm_ref[0])
    kcvc = kc_ref[0]
    m = kcvc.shape[0]
    qpos = q0 + lax.broadcasted_iota(jnp.int32, (tq, 1), 0)
    qpos2 = jnp.concatenate([qpos, qpos], axis=0)
    qpos_row = q0 + lax.broadcasted_iota(jnp.int32, (1, tq), 1)
    n_id = lax.broadcasted_iota(jnp.int32, (2 * tq, m), 1)
    cmask = (n_id * CMP_STRIDE + CMP_LEN - 1 <= qpos2) & (n_id < nc)
    ovl_t = _overlap_matrix(m, nsr, nc, ns, transposed=True)
    start = pl.multiple_of(jnp.clip(q0 + tq - wk, 0, l - wk), LANES)
    dwin = qpos - (start + lax.broadcasted_iota(jnp.int32, (tq, wk), 1))
    wbias = jnp.where((dwin >= 0) & (dwin < WINDOW), 0.0, NEG)
    tri = jnp.where(lax.broadcasted_iota(jnp.int32, (tq, tq), 1) > lax.broadcasted_iota(jnp.int32, (tq, tq), 0),
                    NEG, 0.0)
    n_need = (q0 + tq + kchunk - 1) // kchunk

    def softmax_pv(nk, v_aug):
        def strip(i, carry):
            r = pl.multiple_of(i * NSA_STRIP, NSA_STRIP)
            s = s_s[pl.ds(r, NSA_STRIP), 0:nk]
            e_s[pl.ds(r, NSA_STRIP), 0:nk] = jnp.exp(s - jnp.max(s, axis=-1, keepdims=True)).astype(BF16)
            return carry

        lax.fori_loop(0, 2 * tq // NSA_STRIP, strip, 0, unroll=NSA_STRIP_UNROLL)
        oa = lax.dot_general(e_s[:, 0:nk], v_aug, (((1,), (1,)), ((), ())), preferred_element_type=F32)
        return oa[:, 0:64] * (1.0 / oa[:, 64:65])

    for g in range(NSA_KV_HEADS):
        h0 = g * NSA_GROUP
        q_heads = [q[:, (h0 + hg) * 64:(h0 + hg + 1) * 64] for hg in range(NSA_GROUP)]
        qs = jnp.concatenate(q_heads, axis=0)
        pc = _masked_softmax(_bdot_nt(qs, kcvc[:, g * 64:(g + 1) * 64]), cmask)
        o_cmp = _bdot(pc, kcvc[:, 128 + g * 64:128 + (g + 1) * 64])
        imp_t = lax.dot_general(ovl_t, pc[:tq] + pc[tq:], (((1,), (1,)), ((), ())), preferred_element_type=F32,
                                precision=lax.Precision.HIGHEST)
        bias_t = _select_bias_t(imp_t, qpos_row, ns)
        bias = jnp.concatenate([bias_t, jnp.zeros((LANES - nsr, tq), F32)], axis=0).T[:, 0:SEL_LANES]
        lhs = jnp.concatenate([jnp.concatenate([qh, bias], axis=1) for qh in q_heads], axis=0).astype(BF16)

        s3 = jnp.dot(qs.astype(BF16), kwin_s[g, :, pl.ds(start, wk)], preferred_element_type=F32)
        s_s[0:tq, 0:wk] = s3[0:tq] + wbias
        s_s[tq:2 * tq, 0:wk] = s3[tq:2 * tq] + wbias
        o_win = softmax_pv(wk, vwin_s[g, :, pl.ds(start, wk)])
        part = [gates[:, 8 + 3 * (h0 + hg):9 + 3 * (h0 + hg)] * o_cmp[hg * tq:(hg + 1) * tq]
                + gates[:, 10 + 3 * (h0 + hg):11 + 3 * (h0 + hg)] * o_win[hg * tq:(hg + 1) * tq]
                for hg in range(NSA_GROUP)]
        for nkc in range(1, l // kchunk + 1):

            @pl.when(n_need == nkc)
            def _(nk=nkc * kchunk, g=g, h0=h0, lhs=lhs, part=part):
                s_s[:, 0:nk] = jnp.dot(lhs, krhs_s[g, :, 0:nk], preferred_element_type=F32)
                s_s[0:tq, pl.ds(pl.multiple_of(q0, LANES), tq)] += tri
                s_s[tq:2 * tq, pl.ds(pl.multiple_of(q0, LANES), tq)] += tri
                o_slc = softmax_pv(nk, vslc_s[g, :, 0:nk])
                for hg in range(NSA_GROUP):
                    head = h0 + hg
                    c1 = 9 + 3 * head
                    o_ref[0, :, head * 64:(head + 1) * 64] = (part[hg]
                                                              + gates[:, c1:c1 + 1] * o_slc[hg * tq:(hg + 1) * tq])


def _block_expander(nsp, nkeys):
    blk = lax.broadcasted_iota(jnp.int32, (nsp, nkeys), 0)
    key = lax.broadcasted_iota(jnp.int32, (nsp, nkeys), 1)
    return jnp.where(key // SLC_BLOCK == blk, 1.0, 0.0).astype(BF16)


def _nsa_prompt(q, small, kcvc, slc, win, *, tq):
    b, l, _ = q.shape
    m = kcvc.shape[1]
    nc = m - 1
    ns = -(-l // SLC_BLOCK)
    nsr = -(-ns // SUBLANES) * SUBLANES
    assert nsr <= SEL_LANES
    wk = min(WINDOW + tq, l)
    kern = functools.partial(_nsa_prompt_kernel, tq=tq, l=l, nc=nc, ns=ns, nsr=nsr, wk=wk, kchunk=min(NSA_KEY_CHUNK, l))
    seq = lambda rows: pl.BlockSpec((1, rows, l), lambda i, t: (i, 0, 0))
    tile = lambda wd: pl.BlockSpec((1, tq, wd), lambda i, t: (i, t, 0))
    return pl.pallas_call(
        kern,
        grid=(b, l // tq),
        in_specs=[tile(256), tile(LANES), pl.BlockSpec((1, m, 256), lambda i, t: (i, 0, 0)), seq(256), seq(256),
                  pl.BlockSpec((SEL_LANES, l), lambda i, t: (0, 0))],
        out_specs=tile(256),
        out_shape=jax.ShapeDtypeStruct((b, l, 256), F32),
        scratch_shapes=[pltpu.VMEM((NSA_KV_HEADS, 64 + SEL_LANES, l), BF16),
                        pltpu.VMEM((NSA_KV_HEADS, 64 + DEN_ROWS, l), BF16),
                        pltpu.VMEM((NSA_KV_HEADS, 64, l), BF16),
                        pltpu.VMEM((NSA_KV_HEADS, 64 + DEN_ROWS, l), BF16),
                        pltpu.VMEM((2 * tq, l), F32),
                        pltpu.VMEM((2 * tq, l), BF16)],
        compiler_params=_params("parallel", "arbitrary"),
    )(q, small, kcvc, slc, win, _block_expander(SEL_LANES, l))


def _later_keys_matrix(n):
    a = lax.broadcasted_iota(jnp.int32, (n, n), 0)
    c = lax.broadcasted_iota(jnp.int32, (n, n), 1)
    return jnp.where(a > c, 1.0, 0.0).astype(BF16)


SB_STRIP = 32


def _sb_prompt_kernel(q_ref, kv_ref, later2_ref, o_ref, acc_s, aft_s, z_s, lf_s, hl_s, loc_s, a_s, *, tq):
    qi = pl.program_id(1)
    q = q_ref[0] * SCALE
    acc_s[...] = jnp.zeros_like(acc_s)
    aft_s[...] = jnp.zeros_like(aft_s)
    strips = [slice(r, r + SB_STRIP) for r in range(0, tq, SB_STRIP)]

    def block(k0, diagonal):
        for h in range(N_HEADS):
            kt = kv_ref[0, h * 64:(h + 1) * 64, pl.ds(k0, tq)]
            z_s[h] = _bdot(q[:, h * 64:(h + 1) * 64], kt)
        for h in range(N_HEADS):
            for rows in strips:
                z = z_s[h, rows, :]
                lf = -(jnp.maximum(z, 0.0) + jnp.log(1.0 + jnp.exp(-jnp.abs(z))))
                if diagonal:
                    keep = (lax.broadcasted_iota(jnp.int32, (SB_STRIP, tq), 1)
                            < rows.start + lax.broadcasted_iota(jnp.int32, (SB_STRIP, tq), 0))
                    lf = jnp.where(keep, lf, 0.0)
                hi = lf.astype(BF16)
                lf_s[h, rows, :] = lf
                hl_s[h, rows, 0:tq] = hi
                hl_s[h, rows, tq:2 * tq] = (lf - hi.astype(F32)).astype(BF16)
        for h in range(N_HEADS):
            loc_s[h] = jnp.dot(hl_s[h], later2_ref[...], preferred_element_type=F32)
        for h in range(N_HEADS):
            for rows in strips:
                a = jnp.exp(z_s[h, rows, :] + lf_s[h, rows, :] + (loc_s[h, rows, :] + aft_s[h, rows, :]))
                if diagonal:
                    keep = (lax.broadcasted_iota(jnp.int32, (SB_STRIP, tq), 1)
                            < rows.start + lax.broadcasted_iota(jnp.int32, (SB_STRIP, tq), 0))
                    a = jnp.where(keep, a, 0.0)
                a_s[h, rows, :] = a.astype(BF16)
        for h in range(N_HEADS):
            vt = kv_ref[0, 256 + h * 64:256 + (h + 1) * 64, pl.ds(k0, tq)]
            acc_s[:, h * 64:(h + 1) * 64] += _bdot_nt(a_s[h], vt)
            aft_s[h] = aft_s[h] + loc_s[h, :, 0:1] + lf_s[h, :, 0:1]

    block(pl.multiple_of(qi * tq, tq), True)

    def body(it, carry):
        block(pl.multiple_of((qi - 1 - it) * tq, tq), False)
        return carry

    lax.fori_loop(0, qi, body, 0)
    o_ref[0] = acc_s[...]


def _sb_prompt(q, kv, *, tq):
    b, l, _ = q.shape
    kern = functools.partial(_sb_prompt_kernel, tq=tq)
    return pl.pallas_call(
        kern,
        grid=(b, l // tq),
        in_specs=[pl.BlockSpec((1, tq, 256), lambda i, t: (i, t, 0)),
                  pl.BlockSpec((1, 512, l), lambda i, t: (i, 0, 0)),
                  pl.BlockSpec((2 * tq, tq), lambda i, t: (0, 0))],
        out_specs=pl.BlockSpec((1, tq, 256), lambda i, t: (i, t, 0)),
        out_shape=jax.ShapeDtypeStruct((b, l, 256), F32),
        scratch_shapes=[pltpu.VMEM((tq, 256), F32), pltpu.VMEM((N_HEADS, tq, 1), F32),
                        pltpu.VMEM((N_HEADS, tq, tq), F32), pltpu.VMEM((N_HEADS, tq, tq), F32),
                        pltpu.VMEM((N_HEADS, tq, 2 * tq), BF16), pltpu.VMEM((N_HEADS, tq, tq), F32),
                        pltpu.VMEM((N_HEADS, tq, tq), BF16)],
        compiler_params=_params("parallel", "arbitrary"),
    )(q, kv, jnp.concatenate([_later_keys_matrix(tq)] * 2, axis=0))


SB_PAGE_GROUP = 8


def _sb_sample_kernel(pt_ref, qbd_ref, new_ref, later_ref, *rest, pg, n_pg, pos0):
    pages = rest[:pg]
    o_ref, acc_s, aft_s = rest[pg:]
    g = pl.program_id(1)
    qbd = (qbd_ref[0] * SCALE).astype(BF16)
    later = later_ref[...]
    rows = qbd.shape[0]

    def blocks(kv_refs, mask):
        n = len(kv_refs)
        z = jnp.concatenate([jnp.dot(qbd, r[0, 0:256, :].astype(BF16), preferred_element_type=F32)
                             for r in kv_refs], axis=0)
        lf = -(jnp.maximum(z, 0.0) + jnp.log(1.0 + jnp.exp(-jnp.abs(z))))
        if mask is not None:
            lf = jnp.where(mask, lf, 0.0)
        loc = _split_dot(lf, later)
        aft = aft_s[...]
        acc = acc_s[...]
        for i, r in enumerate(kv_refs):
            blk = slice(i * rows, (i + 1) * rows)
            a = jnp.exp(z[blk] + lf[blk] + (loc[blk] + aft))
            if mask is not None:
                a = jnp.where(mask, a, 0.0)
            acc = acc + _bdot_nt(a, r[0, 256:512, :])
            aft = aft + (loc[blk, 0:1] + lf[blk, 0:1])
        acc_s[...] = acc
        aft_s[...] = aft

    @pl.when(g == 0)
    def _():
        acc_s[...] = jnp.zeros_like(acc_s)
        aft_s[...] = jnp.zeros_like(aft_s)
        t = lax.broadcasted_iota(jnp.int32, (rows, PAGE_SIZE), 0) % SUBLANES
        key = lax.broadcasted_iota(jnp.int32, (rows, PAGE_SIZE), 1)
        blocks([new_ref], key < t)

    for i0 in range(0, pg, SB_PAGE_GROUP):
        blocks(pages[i0:i0 + SB_PAGE_GROUP], None)

    @pl.when(g == n_pg - 1)
    def _():
        acc = acc_s[...]
        lane_head = lax.broadcasted_iota(jnp.int32, (SUBLANES, 256), 1) // HEAD_DIM
        out = jnp.zeros((SUBLANES, 256), F32)
        for h in range(N_HEADS):
            out = out + jnp.where(lane_head == h, acc[h * SUBLANES:(h + 1) * SUBLANES], 0.0)
        o_ref[0] = out


def _sb_sample(page_table, qbd, new_rows, cache, layer_base, *, pg, pos0):
    b, n_pages = page_table.shape
    n_pg = n_pages // pg
    kern = functools.partial(_sb_sample_kernel, pg=pg, n_pg=n_pg, pos0=pos0)

    def page_spec(i):
        return pl.BlockSpec((1, 512, PAGE_SIZE),
                            lambda bi, g, pt: (layer_base + pt[bi, (n_pg - 1 - g) * pg + (pg - 1 - i)], 0, 0))

    grid_spec = pltpu.PrefetchScalarGridSpec(
        num_scalar_prefetch=1,
        grid=(b, n_pg),
        in_specs=[pl.BlockSpec((1, 32, 256), lambda bi, g, pt: (bi, 0, 0)),
                  pl.BlockSpec((1, 512, PAGE_SIZE), lambda bi, g, pt: (bi, 0, 0)),
                  pl.BlockSpec((PAGE_SIZE, PAGE_SIZE), lambda bi, g, pt: (0, 0))]
                 + [page_spec(i) for i in range(pg)],
        out_specs=pl.BlockSpec((1, SUBLANES, 256), lambda bi, g, pt: (bi, 0, 0)),
        scratch_shapes=[pltpu.VMEM((32, 256), F32), pltpu.VMEM((32, 1), F32)],
    )
    return pl.pallas_call(
        kern,
        grid_spec=grid_spec,
        out_shape=jax.ShapeDtypeStruct((b, SUBLANES, 256), F32),
        compiler_params=_params("parallel", "arbitrary"),
    )(page_table, qbd, new_rows, _later_keys_matrix(PAGE_SIZE), *([cache] * pg))


CMP_PAGE_GROUP = 16


def _nsa_sample_cmp_kernel(pt_ref, q_ref, pe_ref, w_ref, w2_ref, *rest, n_pages, pos0, nsp):
    pages = rest[:n_pages]
    ocmp_ref, sel_ref, xs_ref = rest[n_pages:]
    pas, pbs = [], []
    for c in range(0, n_pages, CMP_PAGE_GROUP):
        npg = min(CMP_PAGE_GROUP, n_pages - c)
        for i in range(npg):
            for kv in range(2):
                xs_ref[kv, i * PAGE_SIZE:(i + 1) * PAGE_SIZE, :] = pages[c + i][0, kv * LANES:(kv + 1) * LANES, :].T
        rows = npg * (PAGE_SIZE // CMP_STRIDE)
        halves = []
        for kv in range(2):
            x_kv = jnp.concatenate([xs_ref[kv, pl.ds(t, rows, stride=CMP_STRIDE), :] for t in range(CMP_STRIDE)], axis=1)
            halves.append(_compress_project(x_kv, pe_ref[kv], w_ref[kv]))
        pas.append(jnp.concatenate([halves[0][0], halves[1][0]], axis=1))
        pbs.append(jnp.concatenate([halves[0][1], halves[1][1]], axis=1))
    kcvc = _compress_finish(jnp.concatenate(pas, axis=0), jnp.concatenate(pbs, axis=0), w2_ref[...])
    m = kcvc.shape[0]
    nc = m - 1
    ns = 2 * n_pages + 1
    q = q_ref[0]
    tq = q.shape[0]
    qpos = pos0 + lax.broadcasted_iota(jnp.int32, (tq, 1), 0)
    qpos2 = jnp.concatenate([qpos, qpos], axis=0)
    n_id = lax.broadcasted_iota(jnp.int32, (2 * tq, m), 1)
    cmask = (n_id * CMP_STRIDE + CMP_LEN - 1 <= qpos2) & (n_id < nc)
    ovl = _overlap_matrix(m, nsp, nc, ns)
    for g in range(NSA_KV_HEADS):
        h0 = g * NSA_GROUP
        qs = jnp.concatenate([q[:, h0 * 64:(h0 + 1) * 64], q[:, (h0 + 1) * 64:(h0 + 2) * 64]], axis=0)
        pc = _masked_softmax(_bdot_nt(qs, kcvc[:, g * 64:(g + 1) * 64]) * SCALE, cmask)
        o_cmp = _bdot(pc, kcvc[:, 128 + g * 64:128 + (g + 1) * 64])
        sel_ref[0, g] = _select_bias(_fdot(pc[:tq] + pc[tq:], ovl), qpos, ns)
        for hg in range(NSA_GROUP):
            ocmp_ref[0, :, (h0 + hg) * 64:(h0 + hg + 1) * 64] = o_cmp[hg * tq:(hg + 1) * tq]


def _nsa_sample_cmp(page_table, q, pe, w, w2, cache, layer_base, *, pos0):
    b, n_pages = page_table.shape
    nsp = -(-(2 * n_pages + 1) // LANES) * LANES
    kern = functools.partial(_nsa_sample_cmp_kernel, n_pages=n_pages, pos0=pos0, nsp=nsp)
    full = lambda a: pl.BlockSpec(a.shape, lambda bi, pt: (0,) * a.ndim)

    def page_spec(i):
        return pl.BlockSpec((1, 256, PAGE_SIZE), lambda bi, pt: (layer_base + pt[bi, i], 0, 0))

    grid_spec = pltpu.PrefetchScalarGridSpec(
        num_scalar_prefetch=1,
        grid=(b,),
        in_specs=[pl.BlockSpec((1, SUBLANES, 256), lambda bi, pt: (bi, 0, 0)),
                  full(pe), full(w), full(w2)] + [page_spec(i) for i in range(n_pages)],
        out_specs=[pl.BlockSpec((1, SUBLANES, 256), lambda bi, pt: (bi, 0, 0)),
                   pl.BlockSpec((1, NSA_KV_HEADS, SUBLANES, nsp), lambda bi, pt: (bi, 0, 0, 0))],
        scratch_shapes=[pltpu.VMEM((2, CMP_PAGE_GROUP * PAGE_SIZE, LANES), F32)],
    )
    return pl.pallas_call(
        kern,
        grid_spec=grid_spec,
        out_shape=[jax.ShapeDtypeStruct((b, SUBLANES, 256), F32),
                   jax.ShapeDtypeStruct((b, NSA_KV_HEADS, SUBLANES, nsp), F32)],
        compiler_params=_params("parallel"),
    )(page_table, q, pe, w, w2, *([cache] * n_pages))


def _softmax_rows(s):
    e = jnp.exp(s - jnp.max(s, axis=-1, keepdims=True))
    return e * (1.0 / jnp.sum(e, axis=-1, keepdims=True))


def _nsa_sample_slc_kernel(pt_ref, q_ref, sm_ref, ocmp_ref, sel_ref, newslc_ref, pastwin_ref, newwin_ref,
                           *rest, n_pages, pos0):
    pages = rest[:n_pages]
    o_ref, z_s = rest[n_pages:]
    q = q_ref[0] * SCALE
    tq = q.shape[0]
    gates = _sigmoid(sm_ref[0])
    nk = (n_pages + 1) * PAGE_SIZE
    past = pastwin_ref.shape[2]
    t2 = lax.broadcasted_iota(jnp.int32, (2 * tq, 1), 0) % tq
    kposw = jnp.concatenate([lax.broadcasted_iota(jnp.int32, (2 * tq, past), 1) - past,
                             lax.broadcasted_iota(jnp.int32, (2 * tq, PAGE_SIZE), 1)], axis=1)
    dwin = t2 - kposw
    wbias = jnp.where((dwin >= 0) & (dwin < WINDOW), 0.0, NEG)
    lane = lax.broadcasted_iota(jnp.int32, (2 * tq, PAGE_SIZE), 1)
    lower_block = lane < SLC_BLOCK
    new_causal = jnp.where(lane <= t2, 0.0, NEG)
    for g in range(NSA_KV_HEADS):
        h0 = g * NSA_GROUP
        krow = slice(g * 64, (g + 1) * 64)
        vrow = slice(128 + g * 64, 128 + (g + 1) * 64)
        qs = jnp.concatenate([q[:, h0 * 64:(h0 + 1) * 64], q[:, (h0 + 1) * 64:(h0 + 2) * 64]], axis=0).astype(BF16)
        bias = jnp.concatenate([sel_ref[0, g], sel_ref[0, g]], axis=0)
        dot = lambda k_t: jnp.dot(qs, k_t.astype(BF16), preferred_element_type=F32)
        for p in range(n_pages):
            z_s[:, p * PAGE_SIZE:(p + 1) * PAGE_SIZE] = dot(pages[p][0, krow, :]) + jnp.where(
                lower_block, bias[:, 2 * p:2 * p + 1], bias[:, 2 * p + 1:2 * p + 2])
        z_s[:, n_pages * PAGE_SIZE:nk] = dot(newslc_ref[0, krow, :]) + bias[:, 2 * n_pages:2 * n_pages + 1] + new_causal
        p2 = _softmax_rows(z_s[...])
        o_slc = _bdot_nt(p2[:, n_pages * PAGE_SIZE:nk], newslc_ref[0, vrow, :])
        for p in range(n_pages):
            o_slc = o_slc + _bdot_nt(p2[:, p * PAGE_SIZE:(p + 1) * PAGE_SIZE], pages[p][0, vrow, :])
        p3 = _softmax_rows(jnp.concatenate([dot(pastwin_ref[0, krow, :]), dot(newwin_ref[0, krow, :])], axis=1) + wbias)
        o_win = _bdot_nt(p3[:, 0:past], pastwin_ref[0, vrow, :]) + _bdot_nt(p3[:, past:], newwin_ref[0, vrow, :])
        for hg in range(NSA_GROUP):
            head = h0 + hg
            rows = slice(hg * tq, (hg + 1) * tq)
            c0 = 8 + 3 * head
            o_ref[0, :, head * 64:(head + 1) * 64] = (gates[:, c0:c0 + 1] * ocmp_ref[0, :, head * 64:(head + 1) * 64]
                                                      + gates[:, c0 + 1:c0 + 2] * o_slc[rows]
                                                      + gates[:, c0 + 2:c0 + 3] * o_win[rows])


def _nsa_sample_slc(page_table, q, small, o_cmp, sel, new_slc, past_win, past_win_base, new_win, cache, layer_base,
                    *, pos0):
    b, n_pages = page_table.shape
    nsp = sel.shape[-1]
    nk = (n_pages + 1) * PAGE_SIZE
    past = past_win.shape[2]
    kern = functools.partial(_nsa_sample_slc_kernel, n_pages=n_pages, pos0=pos0)
    per_b = lambda r, wd: pl.BlockSpec((1, r, wd), lambda bi, pt: (bi, 0, 0))

    def page_spec(i):
        return pl.BlockSpec((1, 256, PAGE_SIZE), lambda bi, pt: (layer_base + pt[bi, i], 0, 0))

    grid_spec = pltpu.PrefetchScalarGridSpec(
        num_scalar_prefetch=1,
        grid=(b,),
        in_specs=[per_b(SUBLANES, 256), per_b(SUBLANES, LANES), per_b(SUBLANES, 256),
                  pl.BlockSpec((1, NSA_KV_HEADS, SUBLANES, nsp), lambda bi, pt: (bi, 0, 0, 0)),
                  per_b(256, PAGE_SIZE),
                  pl.BlockSpec((1, 256, past), lambda bi, pt: (past_win_base + bi, 0, 0)),
                  per_b(256, PAGE_SIZE)] + [page_spec(i) for i in range(n_pages)],
        out_specs=per_b(SUBLANES, 256),
        scratch_shapes=[pltpu.VMEM((2 * SUBLANES, nk), F32)],
    )
    return pl.pallas_call(
        kern,
        grid_spec=grid_spec,
        out_shape=jax.ShapeDtypeStruct((b, SUBLANES, 256), F32),
        compiler_params=_params("parallel"),
    )(page_table, q, small, o_cmp, sel, new_slc, past_win, new_win, *([cache] * n_pages))


def _transpose_cast_kernel(x_ref, o_ref):
    o_ref[...] = x_ref[...].T.astype(BF16)


def _transpose_cast(x, tr=384):
    r, c = x.shape
    return pl.pallas_call(
        _transpose_cast_kernel,
        grid=(r // tr,),
        in_specs=[pl.BlockSpec((tr, c), lambda i: (i, 0))],
        out_specs=pl.BlockSpec((c, tr), lambda i: (0, i)),
        out_shape=jax.ShapeDtypeStruct((c, r), BF16),
        compiler_params=_params("parallel"),
    )(x)


def _prep_layer(l, P):
    w_t = P["w_in_t"][l]
    rows = lambda lo, hi: w_t[lo:hi]
    small = jnp.concatenate([rows(1024, 1032), rows(2568, 2580), jnp.zeros((LANES - 20, D_MODEL), F32)], axis=0)
    w_perm = _transpose_cast(jnp.concatenate([rows(0, 1024), rows(1032, 1544), rows(1544, 1800), rows(2580, 2836),
                                              rows(2836, 3348), rows(1800, 2568), small], axis=0))
    par = jnp.zeros((2, LANES), F32).at[0, 4:8].set(P["gdn_a_log"][l]).at[1, 4:8].set(P["gdn_dt_bias"][l])

    def block_diag(w):
        return jnp.einsum("ncd,nm->ncmd", w, jnp.eye(4, dtype=F32)).reshape(GROUP_WIDTH, GROUP_WIDTH).astype(BF16)

    eye2 = jnp.eye(2, dtype=F32)
    w1 = P["nsa_cmp_w1"][l]
    half = CMP_LEN // 2

    cmp_w = jnp.einsum("khtde,gG->ktgdhGe", w1.reshape(2, 2, half, HEAD_DIM, HEAD_DIM), eye2)
    cmp_w = cmp_w.reshape(2, half * 128, 256).astype(BF16)
    pe = P["nsa_cmp_pe"][l].reshape(2, 2, half, 1, HEAD_DIM)
    cmp_pe = jnp.broadcast_to(pe, (2, 2, half, 2, HEAD_DIM)).reshape(2, 2, half * 128)
    w2 = jnp.einsum("kef,kK,gG->kgeKGf", P["nsa_cmp_w2"][l], eye2, eye2).reshape(256, 256).astype(BF16)
    w_kv_t = jnp.concatenate([rows(2836, 3348), rows(1800, 2568)], axis=0).astype(BF16)
    return dict(
        norm1=P["norm1"][l][None], w_in=w_perm, w_kv_t=w_kv_t,
        gdn_cw=P["gdn_conv_w"][l], gdn_par=par, gdn_norm=P["gdn_norm"][l][None],
        lru_cw=P["lru_conv_w"][l],
        lru_vec=jnp.stack([P["lru_conv_b"][l], P["lru_ba"][l], P["lru_bx"][l], P["lru_lambda"][l]]),
        lru_wa=block_diag(P["lru_wa"][l]), lru_wx=block_diag(P["lru_wx"][l]),
        cmp_pe=cmp_pe, cmp_w=cmp_w, cmp_w2=w2,
        w_out=P["w_out"][l].astype(BF16), norm2=P["norm2"][l][None],
        wg=P["ffn_w_gate"][l].astype(BF16), wu=P["ffn_w_up"][l].astype(BF16),
        ffn_cw=P["ffn_conv_w"][l], wd=P["ffn_w_down"][l].astype(BF16),
    )


def _rope_tables(pos):
    half = HEAD_DIM // 2
    inv = ROPE_THETA ** (-jnp.arange(half, dtype=F32) / half)
    ang = pos.astype(F32)[:, None] * inv[None, :]
    c, s = jnp.cos(ang), jnp.sin(ang)
    return jnp.tile(jnp.concatenate([c, c], axis=-1), (1, 2)), jnp.tile(jnp.concatenate([-s, s], axis=-1), (1, 2))


PROMPT_TM = 256
PROMPT_TQ = 128
SB_TQ = 256
GDN_CHUNK = 64


def _layer_prompt(x2d, W, b, l, cos, sin, final, norm_f):
    outs = _in_proj_t(x2d, W["norm1"], W["w_in"], W["w_kv_t"], cos, sin, b, l, PROMPT_TM)
    zg, zl, nq, sq, cmp_r, small = [o.reshape(b, l, o.shape[-1]) for o in outs[:6]]
    skv_t, cmp_t, slc_t, win_t = outs[6:]
    o_gdn, gdn_conv, gdn_s = _gdn(zg, small, jnp.zeros((b, SUBLANES, 768), F32), jnp.zeros((b, N_HEADS, 64, 64), F32),
                                  W["gdn_cw"], W["gdn_par"], W["gdn_norm"], tl=PROMPT_TM, chunk=GDN_CHUNK,
                                  l_real=PROMPT_TM)
    o_lru, lru_conv, lru_h = _lru(zl, jnp.zeros((b, SUBLANES, GROUP_WIDTH), F32), jnp.zeros((b, 1, GROUP_WIDTH), F32),
                                  W["lru_cw"], W["lru_vec"], W["lru_wa"], W["lru_wx"], tl=PROMPT_TM, l_real=PROMPT_TM)
    kcvc = _cmp_prompt(cmp_r.reshape(b, l // CMP_STRIDE, CMP_STRIDE * 256), W["cmp_pe"], W["cmp_w"], W["cmp_w2"])
    o_nsa = _nsa_prompt(nq, small, kcvc, slc_t, win_t, tq=PROMPT_TQ)
    o_sb = _sb_prompt(sq, skv_t, tq=min(SB_TQ, l))
    mixes = [o.reshape(b * l, GROUP_WIDTH) for o in (o_gdn, o_lru, o_nsa, o_sb)]
    res = _out_ffn(x2d, mixes, W["w_out"], W["norm2"], W["wg"], W["wu"], W["ffn_cw"], W["wd"],
                   jnp.zeros((SUBLANES, D_FF), F32), norm_f, tm=PROMPT_TM, stride=1, tiles_per_seq=l // PROMPT_TM,
                   final=final)
    kv5 = lambda a, h: jnp.transpose(a.reshape(b, 2, h, HEAD_DIM, a.shape[-1]), (0, 4, 1, 2, 3))
    wkeep = min(WINDOW, l)
    states = (kv5(cmp_t, 2), kv5(slc_t, 2), kv5(skv_t, 4), kv5(win_t[:, :, l - wkeep:], 2), gdn_conv, gdn_s, lru_conv,
              lru_h[:, 0], res[1][l // PROMPT_TM - 1::l // PROMPT_TM])
    return res[0], states, (res[2] if final else None)


def _layer_sample(x2d, W, layer, S, cos, sin, final, norm_f):
    page_table = S["page_table"]
    b, n_pages = page_table.shape
    t_new = x2d.shape[0] // b
    pos0 = n_pages * PAGE_SIZE
    n_phys = S["n_phys"]
    outs = _in_proj(x2d, W["norm1"], W["w_in"], cos, sin, x2d.shape[0], 1)

    def to_batch_major(a, rows=SUBLANES):
        a = a.reshape(t_new, b, a.shape[-1]).transpose(1, 0, 2)
        return jnp.pad(a, ((0, 0), (0, rows - t_new), (0, 0)))

    zg, zl, nq, sq, skv, cmp_r, slc_r, win_r, small = [to_batch_major(o) for o in outs]
    front = lambda st: jnp.pad(st, ((0, 0), (SUBLANES - st.shape[1], 0), (0, 0)))
    o_gdn, gdn_conv, gdn_s = _gdn(zg, small, front(S["gdn_conv"][layer]), S["gdn"][layer],
                                  W["gdn_cw"], W["gdn_par"], W["gdn_norm"], tl=SUBLANES, chunk=SUBLANES, l_real=t_new)
    o_lru, lru_conv, lru_h = _lru(zl, front(S["lru_conv"][layer]), S["lru"][layer][:, None, :],
                                  W["lru_cw"], W["lru_vec"], W["lru_wa"], W["lru_wx"], tl=SUBLANES, l_real=t_new)
    base = layer * n_phys
    o_cmp, sel = _nsa_sample_cmp(page_table, nq, W["cmp_pe"], W["cmp_w"], W["cmp_w2"],
                                 S["cmp_pages"], base, pos0=pos0)
    pad_page = lambda a: jnp.pad(a.transpose(0, 2, 1), ((0, 0), (0, 0), (0, PAGE_SIZE - a.shape[1])))
    o_nsa = _nsa_sample_slc(page_table, nq, small, o_cmp, sel, pad_page(slc_r), S["win"], layer * b, pad_page(win_r),
                            S["slc_pages"], base, pos0=pos0)
    head_of_col = jnp.arange(256) // HEAD_DIM
    head_of_row = jnp.arange(N_HEADS * SUBLANES) // SUBLANES
    qbd = jnp.where(head_of_row[:, None] == head_of_col[None, :], jnp.tile(sq, (1, N_HEADS, 1)), 0.0)
    o_sb = _sb_sample(page_table, qbd, pad_page(skv), S["sb_pages"], base, pg=min(32, n_pages), pos0=pos0)

    def to_time_major(a):
        return a[:, :t_new].transpose(1, 0, 2).reshape(t_new * b, a.shape[-1])

    mixes = [to_time_major(o) for o in (o_gdn, o_lru, o_nsa, o_sb)]
    buf = S["ffn_conv"][layer].transpose(1, 0, 2).reshape(2 * b, D_FF)
    res = _out_ffn(x2d, mixes, W["w_out"], W["norm2"], W["wg"], W["wu"], W["ffn_cw"], W["wd"], buf, norm_f,
                   tm=x2d.shape[0], stride=b, tiles_per_seq=1, final=final)
    kv5 = lambda a, h: a[:, :t_new].reshape(b, t_new, 2, h, HEAD_DIM)
    ffn_state = res[1].reshape(2, b, D_FF).transpose(1, 0, 2)
    states = (kv5(cmp_r, 2), kv5(slc_r, 2), kv5(skv, 4), kv5(win_r, 2), gdn_conv, gdn_s, lru_conv, lru_h[:, 0], ffn_state)
    return res[0], states, (res[2] if final else None)


def _sample_state_views(cache_cmp_kv, cache_slc_kv, cache_sb_kv, cache_win_kv):
    def view(c):
        d0, d1, tok, kv, h, d = c.shape
        return jnp.transpose(c, (0, 1, 3, 4, 5, 2)).reshape(d0 * d1, kv * h * d, tok)

    return dict(cmp_pages=view(cache_cmp_kv), slc_pages=view(cache_slc_kv), sb_pages=view(cache_sb_kv),
                win=view(cache_win_kv))


def kernel(x_prompt, x_sample, cache_cmp_kv, cache_slc_kv, cache_sb_kv, cache_win_kv, state_gdn_conv, state_gdn, state_lru_conv, state_lru, state_ffn_conv, page_table, norm1, w_in, gdn_conv_w, gdn_a_log, gdn_dt_bias, gdn_norm, lru_conv_w, lru_conv_b, lru_wa, lru_ba, lru_wx, lru_bx, lru_lambda, nsa_cmp_pe, nsa_cmp_w1, nsa_cmp_w2, w_out, norm2, ffn_w_gate, ffn_w_up, ffn_conv_w, ffn_w_down, norm_f):
    P = dict(norm1=norm1, w_in_t=jnp.transpose(w_in, (0, 2, 1)), gdn_conv_w=gdn_conv_w, gdn_a_log=gdn_a_log, gdn_dt_bias=gdn_dt_bias,
             gdn_norm=gdn_norm, lru_conv_w=lru_conv_w, lru_conv_b=lru_conv_b, lru_wa=lru_wa, lru_ba=lru_ba,
             lru_wx=lru_wx, lru_bx=lru_bx, lru_lambda=lru_lambda, nsa_cmp_pe=nsa_cmp_pe, nsa_cmp_w1=nsa_cmp_w1,
             nsa_cmp_w2=nsa_cmp_w2, w_out=w_out, norm2=norm2, ffn_w_gate=ffn_w_gate, ffn_w_up=ffn_w_up,
             ffn_conv_w=ffn_conv_w, ffn_w_down=ffn_w_down)
    depth = w_in.shape[0]
    weights = [_prep_layer(l, P) for l in range(depth)]
    gf = norm_f[None]

    b, l, _ = x_prompt.shape
    cos, sin = _rope_tables(jnp.arange(l))
    h = x_prompt.reshape(b * l, D_MODEL)
    p_states, y_prompt = [], None
    for layer in range(depth):
        h, st, y = _layer_prompt(h, weights[layer], b, l, cos, sin, layer == depth - 1, gf)
        p_states.append(st)
        y_prompt = y
    y_prompt = y_prompt.reshape(b, l, D_MODEL)

    db, t_new, _ = x_sample.shape
    n_phys = cache_sb_kv.shape[1]
    n_pages = page_table.shape[1]
    pos0 = n_pages * PAGE_SIZE
    past_win = cache_win_kv.shape[2]
    S = _sample_state_views(cache_cmp_kv, cache_slc_kv, cache_sb_kv, cache_win_kv)
    S.update(page_table=page_table, n_phys=n_phys,
             gdn_conv=state_gdn_conv, gdn=state_gdn, lru_conv=state_lru_conv, lru=state_lru, ffn_conv=state_ffn_conv)
    cos_s, sin_s = _rope_tables(pos0 + jnp.repeat(jnp.arange(t_new), db))
    h = x_sample.transpose(1, 0, 2).reshape(t_new * db, D_MODEL)
    s_states, y_sample = [], None
    for layer in range(depth):
        h, st, y = _layer_sample(h, weights[layer], layer, S, cos_s, sin_s, layer == depth - 1, gf)
        s_states.append(st)
        y_sample = y
    y_sample = y_sample.reshape(t_new, db, D_MODEL).transpose(1, 0, 2)

    stk = lambda states, i: jnp.stack([s[i] for s in states])
    return ((y_prompt, y_sample) + tuple(stk(p_states, i) for i in range(9))
            + tuple(stk(s_states, i) for i in range(9)))
```

```python
import functools
import math

import jax
import jax.numpy as jnp
from jax import lax
from jax.experimental import pallas as pl
from jax.experimental.pallas import tpu as pltpu

F32, BF16 = jnp.float32, jnp.bfloat16

D_MODEL = 1024
HEAD_DIM = 64
GROUP_WIDTH = 256
N_HEADS = 4
NSA_KV_HEADS = 2
NSA_GROUP = 2
D_FF = 2816
SHORT_CONV = 4
FFN_CONV = 3
LRU_C = 8.0
CMP_STRIDE = 16
CMP_LEN = 32
SLC_BLOCK = 64
SLC_TOPK = 16
N_LOCAL = 2
WINDOW = 512
FORCE_BONUS = 1.0e3
ROPE_THETA = 10000.0
EPS = 1e-6
NEG = -1e30
SCALE = HEAD_DIM ** -0.5
PAGE_SIZE = 128

VMEM_LIMIT_BYTES = 56 * 1024 * 1024
LANES = 128
SUBLANES = 8

IN_WIDTHS = (1024, 512, 256, 256, 512, 256, 256, 256, 128)
D_IN_PAD = sum(IN_WIDTHS)


def _params(*sem):
    return pltpu.CompilerParams(dimension_semantics=sem, vmem_limit_bytes=VMEM_LIMIT_BYTES)


def _bdot(a, b):
    return jnp.dot(a.astype(BF16), b.astype(BF16), preferred_element_type=F32)


def _bdot_nt(a, b):
    return lax.dot_general(a.astype(BF16), b.astype(BF16), (((1,), (1,)), ((), ())), preferred_element_type=F32)


def _bdot_tn(a, b):
    return lax.dot_general(a.astype(BF16), b.astype(BF16), (((0,), (0,)), ((), ())), preferred_element_type=F32)


def _fdot(a, b):
    return jnp.dot(a, b, preferred_element_type=F32, precision=lax.Precision.HIGHEST)


GDN_GROUP_CHUNKS = 2


def _split_dot(a, b_exact):
    hi = a.astype(BF16)
    lo = (a - hi.astype(F32)).astype(BF16)
    return (jnp.dot(hi, b_exact, preferred_element_type=F32) + jnp.dot(lo, b_exact, preferred_element_type=F32))


def _sigmoid(x):
    return jax.nn.sigmoid(x)


def _silu(x):
    return x * jax.nn.sigmoid(x)


def _softplus(x):
    return jnp.maximum(x, 0.0) + jnp.log1p(jnp.exp(-jnp.abs(x)))


def _gelu_tanh(x):
    return 0.5 * x * (1.0 + jnp.tanh(math.sqrt(2.0 / math.pi) * (x + 0.044715 * (x * x * x))))


def _rms(x, g):
    return x * lax.rsqrt(jnp.mean(x * x, axis=-1, keepdims=True) + EPS) * g


def _rope_pairs(v, cos, sin):
    lane = lax.broadcasted_iota(jnp.int32, v.shape, 1)
    first_half = (lane % HEAD_DIM) < (HEAD_DIM // 2)
    partner = jnp.where(first_half, pltpu.roll(v, LANES - HEAD_DIM // 2, 1), pltpu.roll(v, HEAD_DIM // 2, 1))
    return v * cos + partner * sin


def _masked_softmax(s, mask):
    sm = jnp.where(mask, s, NEG)
    m = jnp.max(sm, axis=-1, keepdims=True)
    e = jnp.where(mask, jnp.exp(sm - m), 0.0)
    den = jnp.sum(e, axis=-1, keepdims=True)
    return e * (1.0 / jnp.where(den > 0.0, den, 1.0))


def _in_proj_kernel(x_ref, g_ref, w_ref, cos_ref, sin_ref,
                    zg_ref, zl_ref, nq_ref, sq_ref, skv_ref, cmp_ref, slc_ref, win_ref, sm_ref):
    xb = _rms(x_ref[...], g_ref[...]).astype(BF16)
    cos, sin = cos_ref[...], sin_ref[...]

    def mm(lo, width):
        return jnp.dot(xb, w_ref[:, lo:lo + width], preferred_element_type=F32)

    zg_ref[...] = mm(0, 1024)
    zl_ref[...] = mm(1024, 512)
    q = mm(1536, 256)
    nq_ref[:, 0:128] = _rope_pairs(q[:, 0:128], cos, sin)
    nq_ref[:, 128:256] = _rope_pairs(q[:, 128:256], cos, sin)
    sq_ref[...] = mm(1792, 256)
    skv_ref[...] = mm(2048, 512)
    for ref, lo in ((cmp_ref, 2560), (slc_ref, 2816), (win_ref, 3072)):
        kv = mm(lo, 256)
        ref[:, 0:128] = _rope_pairs(kv[:, 0:128], cos, sin)
        ref[:, 128:256] = kv[:, 128:256]
    sm_ref[...] = mm(3328, 128)


def _in_proj(x2d, g, w, cos, sin, tm, table_tiles):
    n = x2d.shape[0]
    nt = n // tm
    row = lambda wd: pl.BlockSpec((tm, wd), lambda i: (i, 0))
    tab = pl.BlockSpec((tm, LANES), lambda i: (i % table_tiles, 0))
    return pl.pallas_call(
        _in_proj_kernel,
        grid=(nt,),
        in_specs=[row(D_MODEL), pl.BlockSpec((1, D_MODEL), lambda i: (0, 0)),
                  pl.BlockSpec((D_MODEL, D_IN_PAD), lambda i: (0, 0)), tab, tab],
        out_specs=[row(wd) for wd in IN_WIDTHS],
        out_shape=[jax.ShapeDtypeStruct((n, wd), F32) for wd in IN_WIDTHS],
        compiler_params=_params("parallel"),
    )(x2d, g, w, cos, sin)


def _rope_rows(v, cos_t, sin_t):
    q = HEAD_DIM // 2
    partner = jnp.concatenate([v[q:2 * q], v[0:q], v[3 * q:4 * q], v[2 * q:3 * q]], axis=0)
    return v * cos_t + partner * sin_t


KV_T_ROWS = 1280


N_KV_OUT = 4


def _in_proj_t_kernel(x_ref, g_ref, w_ref, wt_ref, cos_ref, sin_ref, cost_ref, sint_ref, *rest):
    (zg_ref, zl_ref, nq_ref, sq_ref, cmp_ref, sm_ref, skvt_ref, cmpt_ref, slct_ref, wint_ref) = rest[-(6 + N_KV_OUT):]
    xb = _rms(x_ref[...], g_ref[...]).astype(BF16)
    cos, sin = cos_ref[...], sin_ref[...]
    cos_t, sin_t = cost_ref[...], sint_ref[...]

    def mm(lo, width):
        return jnp.dot(xb, w_ref[:, lo:lo + width], preferred_element_type=F32)

    def mm_t(lo, rows):
        return lax.dot_general(wt_ref[lo:lo + rows, :], xb, (((1,), (1,)), ((), ())), preferred_element_type=F32)

    zg_ref[...] = mm(0, 1024)
    zl_ref[...] = mm(1024, 512)
    q = mm(1536, 256)
    nq_ref[:, 0:128] = _rope_pairs(q[:, 0:128], cos, sin)
    nq_ref[:, 128:256] = _rope_pairs(q[:, 128:256], cos, sin)
    sq_ref[...] = mm(1792, 256)
    kv = mm(2560, 256)
    cmp_ref[:, 0:128] = _rope_pairs(kv[:, 0:128], cos, sin)
    cmp_ref[:, 128:256] = kv[:, 128:256]
    sm_ref[...] = mm(3328, 128)
    skvt_ref[0] = mm_t(0, 512)
    for ref, lo in ((cmpt_ref, 512), (slct_ref, 768), (wint_ref, 1024)):
        kvt = mm_t(lo, 256)
        ref[0, 0:128, :] = _rope_rows(kvt[0:128], cos_t, sin_t)
        ref[0, 128:256, :] = kvt[128:256]


def _in_proj_t(x2d, g, w, wt, cos, sin, b, l, tm, layer, depth, kv_stacks):
    tps = l // tm
    row = lambda wd: pl.BlockSpec((tm, wd), lambda i: (i, 0))
    tab = pl.BlockSpec((tm, LANES), lambda i: (i % tps, 0))
    tab_t = pl.BlockSpec((LANES, tm), lambda i: (0, i % tps))
    kvt = lambda c: pl.BlockSpec((None, 1, c, tm), lambda i: (layer, i // tps, 0, i % tps))
    row_widths = (1024, 512, 256, 256, 256, 128)
    t_rows = (512, 256, 256, 256)
    in_specs = [row(D_MODEL), pl.BlockSpec((1, D_MODEL), lambda i: (0, 0)),
                pl.BlockSpec((D_MODEL, D_IN_PAD), lambda i: (0, 0)),
                pl.BlockSpec((KV_T_ROWS, D_MODEL), lambda i: (0, 0)), tab, tab, tab_t, tab_t]
    args = [x2d, g, w, wt, cos, sin, cos.T, sin.T]
    aliases = {}
    if kv_stacks is not None:
        aliases = {len(args) + k: len(row_widths) + k for k in range(N_KV_OUT)}
        in_specs += [pl.BlockSpec(memory_space=pl.ANY)] * N_KV_OUT
        args += list(kv_stacks)
    return pl.pallas_call(
        _in_proj_t_kernel,
        grid=(b * tps,),
        in_specs=in_specs,
        out_specs=[row(wd) for wd in row_widths] + [kvt(c) for c in t_rows],
        out_shape=[jax.ShapeDtypeStruct((b * l, wd), F32) for wd in row_widths]
                  + [jax.ShapeDtypeStruct((depth, b, c, l), F32) for c in t_rows],
        input_output_aliases=aliases,
        compiler_params=_params("parallel"),
    )(*args)


FFN_SLABS = 2


def _out_ffn_kernel(*refs, tm, tf, nf, stride, prev_rows, tiles_per_seq, final):
    x_ref, mix_refs, refs = refs[0], refs[1:5], refs[5:]
    if final:
        (wo_ref, g2_ref, wg_ref, wu_ref, cw_ref, wd_ref, buf_ref, gf_ref,
         xo_ref, st_ref, y_ref, carry_s) = refs
    else:
        (wo_ref, g2_ref, wg_ref, wu_ref, cw_ref, wd_ref, buf_ref,
         xo_ref, st_ref, carry_s) = refs
    i = pl.program_id(0)
    mix = jnp.concatenate([m[...] for m in mix_refs], axis=1)
    x1 = x_ref[...] + _bdot(mix, wo_ref[...])
    xn = _rms(x1, g2_ref[...]).astype(BF16)
    seq_start = (i % tiles_per_seq) == 0
    p, s = prev_rows, stride
    x2 = x1
    for j in range(nf):
        cols = slice(j * tf, (j + 1) * tf)
        a = jnp.dot(xn, wg_ref[:, cols], preferred_element_type=F32)
        u = jnp.dot(xn, wu_ref[:, cols], preferred_element_type=F32)
        prev = jnp.where(seq_start, buf_ref[:, cols], carry_s[:, cols])
        ext = jnp.concatenate([prev, a], axis=0)
        cw = cw_ref[:, cols]
        ac = cw[0:1] * ext[p - 2 * s:p - 2 * s + tm] + cw[1:2] * ext[p - s:p - s + tm] + cw[2:3] * ext[p:p + tm]
        carry_s[:, cols] = ext[tm:tm + p]
        st_ref[0, :, cols] = ext[tm + p - 2 * s:tm + p]
        x2 = x2 + _bdot(_silu(ac) * u, wd_ref[cols, :])
    xo_ref[...] = x2
    if final:
        y_ref[...] = _rms(x2, gf_ref[...])


def _out_ffn(x2d, mixes, wo, g2, wg, wu, cw, wd, buf, gf, *, tm, stride, tiles_per_seq, final):
    n = x2d.shape[0]
    nt = n // tm
    nf = FFN_SLABS
    tf = D_FF // nf
    prev_rows = buf.shape[0]
    kern = functools.partial(_out_ffn_kernel, tm=tm, tf=tf, nf=nf, stride=stride, prev_rows=prev_rows,
                             tiles_per_seq=tiles_per_seq, final=final)
    tok = pl.BlockSpec((tm, D_MODEL), lambda i: (i, 0))
    part = pl.BlockSpec((tm, GROUP_WIDTH), lambda i: (i, 0))
    const = lambda shape: pl.BlockSpec(shape, lambda i: (0, 0), pipeline_mode=pl.Buffered(1))
    in_specs = [tok, part, part, part, part,
                const((D_MODEL, D_MODEL)), const((1, D_MODEL)), const((D_MODEL, D_FF)), const((D_MODEL, D_FF)),
                const((FFN_CONV, D_FF)), const((D_FF, D_MODEL)), const((prev_rows, D_FF))]
    args = [x2d, *mixes, wo, g2, wg, wu, cw, wd, buf]
    out_specs = [tok, pl.BlockSpec((1, 2 * stride, D_FF), lambda i: (i, 0, 0))]
    out_shape = [jax.ShapeDtypeStruct((n, D_MODEL), F32),
                 jax.ShapeDtypeStruct((nt, 2 * stride, D_FF), F32)]
    if final:
        in_specs.append(const((1, D_MODEL)))
        args.append(gf)
        out_specs.append(tok)
        out_shape.append(jax.ShapeDtypeStruct((n, D_MODEL), F32))
    return pl.pallas_call(
        kern,
        grid=(nt,),
        in_specs=in_specs,
        out_specs=out_specs,
        out_shape=out_shape,
        scratch_shapes=[pltpu.VMEM((prev_rows, D_FF), F32)],
        compiler_params=_params("arbitrary"),
    )(*args)


def _short_conv(x, carry_ref, cw, tl):
    ext = jnp.concatenate([carry_ref[...], x], axis=0)
    y = cw[0:1] * ext[5:5 + tl]
    for i in range(1, SHORT_CONV):
        y = y + cw[i:i + 1] * ext[5 + i:5 + i + tl]
    carry_ref[...] = ext[tl:tl + SUBLANES]
    return y, ext


def _gdn_kernel(zg_ref, sm_ref, cst_ref, s0_ref, cw_ref, par_ref, nrm_ref,
                o_ref, cout_ref, sout_ref, carry_s, state_s, *, tl, chunk, l_real):
    t = pl.program_id(1)

    @pl.when(t == 0)
    def _():
        carry_s[...] = cst_ref[0]
        state_s[...] = s0_ref[0]

    z = zg_ref[0]
    gate = z[:, 768:1024]
    y, ext = _short_conv(z[:, 0:768], carry_s, cw_ref[...], tl)
    cout_ref[0] = ext[SUBLANES + l_real - 3:SUBLANES + l_real]
    y = _silu(y)

    small = sm_ref[0]
    par = par_ref[...]
    beta_c = _sigmoid(small)
    g_c = -jnp.exp(par[0:1]) * _softplus(small + par[1:2])
    if l_real < tl:
        real = lax.broadcasted_iota(jnp.int32, small.shape, 0) < l_real
        beta_c = jnp.where(real, beta_c, 0.0)
        g_c = jnp.where(real, g_c, 0.0)
    ri = lax.broadcasted_iota(jnp.int32, (tl, tl), 0)
    ci = lax.broadcasted_iota(jnp.int32, (tl, tl), 1)
    same = (ri // chunk) == (ci // chunk)
    gc = _fdot(jnp.where(same & (ci <= ri), 1.0, 0.0), g_c)
    grest = _fdot(jnp.where(same & (ci > ri), 1.0, 0.0), g_c)
    gc_t = gc.T

    ii = lax.broadcasted_iota(jnp.int32, (chunk, chunk), 0)
    jj = lax.broadcasted_iota(jnp.int32, (chunk, chunk), 1)
    causal, strict = ii >= jj, ii > jj
    eye = jnp.where(ii == jj, 1.0, 0.0)
    gnorm = nrm_ref[...]
    n_chunks = tl // chunk
    heads = range(N_HEADS)
    S = [state_s[h] for h in heads]
    out_rows = []
    for c0 in range(0, n_chunks, GDN_GROUP_CHUNKS):
        pairs = [(c, h) for c in range(c0, min(c0 + GDN_GROUP_CHUNKS, n_chunks)) for h in heads]
        rows_of = lambda c: slice(c * chunk, (c + 1) * chunk)
        q = [y[rows_of(c), h * 64:(h + 1) * 64] for c, h in pairs]
        k = [y[rows_of(c), 256 + h * 64:256 + (h + 1) * 64] for c, h in pairs]
        v = [y[rows_of(c), 512 + h * 64:512 + (h + 1) * 64] for c, h in pairs]
        q = [x * lax.rsqrt(jnp.sum(x * x, axis=-1, keepdims=True) + EPS) * SCALE for x in q]
        k = [x * lax.rsqrt(jnp.sum(x * x, axis=-1, keepdims=True) + EPS) for x in k]
        beta = [beta_c[rows_of(c), h:h + 1] for c, h in pairs]
        gcc = [gc[rows_of(c), 4 + h:5 + h] for c, h in pairs]
        gcr = [gc_t[4 + h:5 + h, rows_of(c)] for c, h in pairs]
        decay = [jnp.where(causal, jnp.exp(jnp.where(causal, a - b, 0.0)), 0.0) for a, b in zip(gcc, gcr)]
        kb = [a * b for a, b in zip(k, beta)]
        A = [jnp.where(strict, _bdot_nt(a, b) * d, 0.0) for a, b, d in zip(kb, k, decay)]
        T = [eye - a for a in A]
        P = A
        span = 2
        while span < chunk:
            P = [_bdot(p, p) for p in P]
            T = [t + _bdot(t, p) for t, p in zip(T, P)]
            span *= 2
        u = [_bdot(t, a * b) for t, a, b in zip(T, v, beta)]
        w = [_bdot(t, a * jnp.exp(g)) for t, a, g in zip(T, kb, gcc)]
        attn = [_bdot_nt(a, b) * d for a, b, d in zip(q, k, decay)]
        qe = [a * jnp.exp(g) for a, g in zip(q, gcc)]
        kd = [a * jnp.exp(grest[rows_of(c), 4 + h:5 + h]) for a, (c, h) in zip(k, pairs)]
        for i0 in range(0, len(pairs), N_HEADS):
            c = pairs[i0][0]
            v_new = [u[i0 + h] - _bdot(w[i0 + h], S[h]) for h in heads]
            o = [_bdot(qe[i0 + h], S[h]) + _bdot(attn[i0 + h], v_new[h]) for h in heads]
            last = (c + 1) * chunk - 1
            S = [S[h] * jnp.exp(gc[last:last + 1, 4 + h:5 + h]) + _bdot_tn(kd[i0 + h], v_new[h]) for h in heads]
            o = [_rms(o[h], gnorm) * _silu(gate[rows_of(c), h * 64:(h + 1) * 64]) for h in heads]
            out_rows.append(jnp.concatenate(o, axis=1))
    for h in heads:
        state_s[h] = S[h]
        sout_ref[0, h] = S[h]
    o_ref[0] = jnp.concatenate(out_rows, axis=0)


def _gdn(zg, small, conv_state8, s0, cw, par, gnorm, *, tl, chunk, l_real):
    b, l, _ = zg.shape
    nt = l // tl
    kern = functools.partial(_gdn_kernel, tl=tl, chunk=chunk, l_real=l_real)
    return pl.pallas_call(
        kern,
        grid=(b, nt),
        in_specs=[pl.BlockSpec((1, tl, 1024), lambda i, t: (i, t, 0)),
                  pl.BlockSpec((1, tl, LANES), lambda i, t: (i, t, 0)),
                  pl.BlockSpec((1, SUBLANES, 768), lambda i, t: (i, 0, 0)),
                  pl.BlockSpec((1, N_HEADS, 64, 64), lambda i, t: (i, 0, 0, 0)),
                  pl.BlockSpec((SHORT_CONV, 768), lambda i, t: (0, 0)),
                  pl.BlockSpec((2, LANES), lambda i, t: (0, 0)),
                  pl.BlockSpec((1, HEAD_DIM), lambda i, t: (0, 0))],
        out_specs=[pl.BlockSpec((1, tl, GROUP_WIDTH), lambda i, t: (i, t, 0)),
                   pl.BlockSpec((1, 3, 768), lambda i, t: (i, 0, 0)),
                   pl.BlockSpec((1, N_HEADS, 64, 64), lambda i, t: (i, 0, 0, 0))],
        out_shape=[jax.ShapeDtypeStruct((b, l, GROUP_WIDTH), F32),
                   jax.ShapeDtypeStruct((b, 3, 768), F32),
                   jax.ShapeDtypeStruct((b, N_HEADS, 64, 64), F32)],
        scratch_shapes=[pltpu.VMEM((SUBLANES, 768), F32), pltpu.VMEM((N_HEADS, 64, 64), F32)],
        compiler_params=_params("parallel", "arbitrary"),
    )(zg, small, conv_state8, s0, cw, par, gnorm)


def _lru_kernel(zl_ref, cst_ref, h0_ref, cw_ref, vec_ref, wa_ref, wx_ref,
                y_ref, cout_ref, hout_ref, carry_s, h_s, *, tl, l_real):
    t = pl.program_id(1)

    @pl.when(t == 0)
    def _():
        carry_s[...] = cst_ref[0]
        h_s[...] = h0_ref[0]

    z = zl_ref[0]
    gate = z[:, 256:512]
    vec = vec_ref[...]
    xc, ext = _short_conv(z[:, 0:256], carry_s, cw_ref[...], tl)
    cout_ref[0] = ext[SUBLANES + l_real - 3:SUBLANES + l_real]
    xc = xc + vec[0:1]
    r = _sigmoid(_bdot(xc, wa_ref[...]) + vec[1:2])
    i = _sigmoid(_bdot(xc, wx_ref[...]) + vec[2:3])
    log_a = -LRU_C * r * _softplus(-vec[3:4])
    a = jnp.exp(log_a)
    u = jnp.sqrt(1.0 - a * a) * (i * xc)
    rowid = lax.broadcasted_iota(jnp.int32, u.shape, 0)
    u = u + jnp.where(rowid == 0, a * h_s[...], 0.0)
    s = 1
    while s < tl:
        a_sh = jnp.concatenate([jnp.ones((s, GROUP_WIDTH), F32), a[:tl - s]], axis=0)
        u_sh = jnp.concatenate([jnp.zeros((s, GROUP_WIDTH), F32), u[:tl - s]], axis=0)
        u = a * u_sh + u
        a = a * a_sh
        s *= 2
    h_last = u[l_real - 1:l_real]
    h_s[...] = h_last
    hout_ref[0] = h_last
    y_ref[0] = u * _gelu_tanh(gate)


def _lru(zl, conv_state8, h0, cw, vec, wa, wx, *, tl, l_real):
    b, l, _ = zl.shape
    nt = l // tl
    kern = functools.partial(_lru_kernel, tl=tl, l_real=l_real)
    return pl.pallas_call(
        kern,
        grid=(b, nt),
        in_specs=[pl.BlockSpec((1, tl, 512), lambda i, t: (i, t, 0)),
                  pl.BlockSpec((1, SUBLANES, GROUP_WIDTH), lambda i, t: (i, 0, 0)),
                  pl.BlockSpec((1, 1, GROUP_WIDTH), lambda i, t: (i, 0, 0)),
                  pl.BlockSpec((SHORT_CONV, GROUP_WIDTH), lambda i, t: (0, 0)),
                  pl.BlockSpec((4, GROUP_WIDTH), lambda i, t: (0, 0)),
                  pl.BlockSpec((GROUP_WIDTH, GROUP_WIDTH), lambda i, t: (0, 0)),
                  pl.BlockSpec((GROUP_WIDTH, GROUP_WIDTH), lambda i, t: (0, 0))],
        out_specs=[pl.BlockSpec((1, tl, GROUP_WIDTH), lambda i, t: (i, t, 0)),
                   pl.BlockSpec((1, 3, GROUP_WIDTH), lambda i, t: (i, 0, 0)),
                   pl.BlockSpec((1, 1, GROUP_WIDTH), lambda i, t: (i, 0, 0))],
        out_shape=[jax.ShapeDtypeStruct((b, l, GROUP_WIDTH), F32),
                   jax.ShapeDtypeStruct((b, 3, GROUP_WIDTH), F32),
                   jax.ShapeDtypeStruct((b, 1, GROUP_WIDTH), F32)],
        scratch_shapes=[pltpu.VMEM((SUBLANES, GROUP_WIDTH), F32), pltpu.VMEM((1, GROUP_WIDTH), F32)],
        compiler_params=_params("parallel", "arbitrary"),
    )(zl, conv_state8, h0, cw, vec, wa, wx)


def _compress_project(x_kv, pe2, w_kv):
    rows = x_kv.shape[0]
    first = lax.broadcasted_iota(jnp.int32, (2 * SUBLANES, x_kv.shape[1]), 0) < SUBLANES
    p = _bdot(jnp.concatenate([x_kv, jnp.where(first, pe2[0:1], pe2[1:2])], axis=0), w_kv)
    return (p[:rows, 0:128] + p[rows:rows + 1, 0:128],
            p[:rows, 128:256] + p[rows + SUBLANES:rows + SUBLANES + 1, 128:256])


def _compress_finish(pa, pb, w2):
    pb_next = jnp.concatenate([pb[1:], jnp.zeros((1, pb.shape[1]), F32)], axis=0)
    return _bdot(_silu(pa + pb_next), w2)


def _cmp_prompt_kernel(x_ref, pe_ref, w_ref, w2_ref, o_ref):
    x = x_ref[0]
    halves = []
    for kv in range(2):
        x_kv = jnp.concatenate([x[:, t * 256 + kv * 128:t * 256 + (kv + 1) * 128] for t in range(CMP_STRIDE)], axis=1)
        halves.append(_compress_project(x_kv, pe_ref[kv], w_ref[kv]))
    o_ref[0] = _compress_finish(jnp.concatenate([halves[0][0], halves[1][0]], axis=1),
                                jnp.concatenate([halves[0][1], halves[1][1]], axis=1), w2_ref[...])


def _cmp_prompt(xc, pe, w, w2):
    b, m, f = xc.shape
    full = lambda a: pl.BlockSpec(a.shape, lambda i: (0,) * a.ndim)
    return pl.pallas_call(
        _cmp_prompt_kernel,
        grid=(b,),
        in_specs=[pl.BlockSpec((1, m, f), lambda i: (i, 0, 0)), full(pe), full(w), full(w2)],
        out_specs=pl.BlockSpec((1, m, 256), lambda i: (i, 0, 0)),
        out_shape=jax.ShapeDtypeStruct((b, m, 256), F32),
        compiler_params=_params("parallel"),
    )(xc, pe, w, w2)


def _overlap_matrix(m, nsp, nc, ns, transposed=False):
    shape = (nsp, m) if transposed else (m, nsp)
    nn = lax.broadcasted_iota(jnp.int32, shape, 1 if transposed else 0)
    mm = lax.broadcasted_iota(jnp.int32, shape, 0 if transposed else 1)
    ov = (jnp.minimum(nn * CMP_STRIDE + CMP_LEN - 1, mm * SLC_BLOCK + SLC_BLOCK - 1)
          - jnp.maximum(nn * CMP_STRIDE, mm * SLC_BLOCK) + 1)
    return jnp.where((nn < nc) & (mm < ns), jnp.maximum(ov, 0).astype(F32) * (1.0 / CMP_LEN), 0.0)


def _select_bias(imp, qpos, ns):
    rows, nsp = imp.shape
    j = lax.broadcasted_iota(jnp.int32, (rows, nsp), 1)
    cur = qpos // SLC_BLOCK
    valid = (j <= cur) & (j < ns)
    forced = valid & ((j == 0) | (j > cur - N_LOCAL))
    score = jnp.where(valid, imp + jnp.where(forced, FORCE_BONUS, 0.0), NEG)
    score = jnp.where(j < ns, score, -3.0e38)
    cnt = jnp.zeros((rows, nsp), jnp.int32)
    for i in range(ns):
        si = score[:, i:i + 1]
        beats = (si > score) | ((si == score) & (i < j))
        cnt = cnt + beats.astype(jnp.int32)
    return jnp.where((cnt < min(SLC_TOPK, ns)) & valid, 0.0, NEG)


NSA_KEY_CHUNK = 512


NSA_STRIP = 16
NSA_STRIP_UNROLL = 32
SEL_LANES = 64
DEN_ROWS = 16


def _select_bias_t(imp_t, qpos_row, ns):
    nsr, tq = imp_t.shape
    j = lax.broadcasted_iota(jnp.int32, (nsr, tq), 0)
    cur = qpos_row // SLC_BLOCK
    valid = (j <= cur) & (j < ns)
    forced = valid & ((j == 0) | (j > cur - N_LOCAL))
    score = jnp.where(valid, imp_t + jnp.where(forced, FORCE_BONUS, 0.0), NEG)
    score = jnp.where(j < ns, score, -3.0e38)
    cnt = jnp.zeros((nsr, tq), jnp.int32)
    for i in range(ns):
        si = score[i:i + 1, :]
        beats = (si > score) | ((si == score) & (i < j))
        cnt = cnt + beats.astype(jnp.int32)
    return jnp.where((cnt < min(SLC_TOPK, ns)) & valid, 0.0, NEG)


def _nsa_prompt_kernel(q_ref, sm_ref, kc_ref, slc_ref, win_ref, e_ref, o_ref,
                       krhs_s, vslc_s, kwin_s, vwin_s, s_s, e_s, *, tq, l, nc, ns, nsr, wk, kchunk):
    t = pl.program_id(1)
    q0 = t * tq

    @pl.when(t == 0)
    def _():
        ones = jnp.ones((DEN_ROWS, l), BF16)
        for g in range(NSA_KV_HEADS):
            krow = slice(g * 64, (g + 1) * 64)
            vrow = slice(128 + g * 64, 128 + (g + 1) * 64)
            krhs_s[g, 0:64, :] = slc_ref[0, krow, :].astype(BF16)
            krhs_s[g, 64:64 + SEL_LANES, :] = e_ref[...]
            vslc_s[g, 0:64, :] = slc_ref[0, vrow, :].astype(BF16)
            vslc_s[g, 64:64 + DEN_ROWS, :] = ones
            kwin_s[g] = win_ref[0, krow, :].astype(BF16)
            vwin_s[g, 0:64, :] = win_ref[0, vrow, :].astype(BF16)
            vwin_s[g, 64:64 + DEN_ROWS, :] = ones

    q = q_ref[0] * SCALE
    gates = _sigmoid(sm_ref[0])
    kcvc = kc_ref[0]
    m = kcvc.shape[0]
    qpos = q0 + lax.broadcasted_iota(jnp.int32, (tq, 1), 0)
    qpos2 = jnp.concatenate([qpos, qpos], axis=0)
    qpos_row = q0 + lax.broadcasted_iota(jnp.int32, (1, tq), 1)
    n_id = lax.broadcasted_iota(jnp.int32, (2 * tq, m), 1)
    cmask = (n_id * CMP_STRIDE + CMP_LEN - 1 <= qpos2) & (n_id < nc)
    ovl_t = _overlap_matrix(m, nsr, nc, ns, transposed=True)
    start = pl.multiple_of(jnp.clip(q0 + tq - wk, 0, l - wk), LANES)
    dwin = qpos - (start + lax.broadcasted_iota(jnp.int32, (tq, wk), 1))
    wbias = jnp.where((dwin >= 0) & (dwin < WINDOW), 0.0, NEG)
    tri = jnp.where(lax.broadcasted_iota(jnp.int32, (tq, tq), 1) > lax.broadcasted_iota(jnp.int32, (tq, tq), 0),
                    NEG, 0.0)
    n_need = (q0 + tq + kchunk - 1) // kchunk

    def softmax_pv(nk, v_aug):
        def strip(i, carry):
            r = pl.multiple_of(i * NSA_STRIP, NSA_STRIP)
            s = s_s[pl.ds(r, NSA_STRIP), 0:nk]
            e_s[pl.ds(r, NSA_STRIP), 0:nk] = jnp.exp(s - jnp.max(s, axis=-1, keepdims=True)).astype(BF16)
            return carry

        lax.fori_loop(0, 2 * tq // NSA_STRIP, strip, 0, unroll=NSA_STRIP_UNROLL)
        oa = lax.dot_general(e_s[:, 0:nk], v_aug, (((1,), (1,)), ((), ())), preferred_element_type=F32)
        return oa[:, 0:64] * (1.0 / oa[:, 64:65])

    for g in range(NSA_KV_HEADS):
        h0 = g * NSA_GROUP
        q_heads = [q[:, (h0 + hg) * 64:(h0 + hg + 1) * 64] for hg in range(NSA_GROUP)]
        qs = jnp.concatenate(q_heads, axis=0)
        pc = _masked_softmax(_bdot_nt(qs, kcvc[:, g * 64:(g + 1) * 64]), cmask)
        o_cmp = _bdot(pc, kcvc[:, 128 + g * 64:128 + (g + 1) * 64])
        imp_t = lax.dot_general(ovl_t, pc[:tq] + pc[tq:], (((1,), (1,)), ((), ())), preferred_element_type=F32,
                                precision=lax.Precision.HIGHEST)
        bias_t = _select_bias_t(imp_t, qpos_row, ns)
        bias = jnp.concatenate([bias_t, jnp.zeros((LANES - nsr, tq), F32)], axis=0).T[:, 0:SEL_LANES]
        lhs = jnp.concatenate([jnp.concatenate([qh, bias], axis=1) for qh in q_heads], axis=0).astype(BF16)

        s3 = jnp.dot(qs.astype(BF16), kwin_s[g, :, pl.ds(start, wk)], preferred_element_type=F32)
        s_s[0:tq, 0:wk] = s3[0:tq] + wbias
        s_s[tq:2 * tq, 0:wk] = s3[tq:2 * tq] + wbias
        o_win = softmax_pv(wk, vwin_s[g, :, pl.ds(start, wk)])
        part = [gates[:, 8 + 3 * (h0 + hg):9 + 3 * (h0 + hg)] * o_cmp[hg * tq:(hg + 1) * tq]
                + gates[:, 10 + 3 * (h0 + hg):11 + 3 * (h0 + hg)] * o_win[hg * tq:(hg + 1) * tq]
                for hg in range(NSA_GROUP)]
        for nkc in range(1, l // kchunk + 1):

            @pl.when(n_need == nkc)
            def _(nk=nkc * kchunk, g=g, h0=h0, lhs=lhs, part=part):
                s_s[:, 0:nk] = jnp.dot(lhs, krhs_s[g, :, 0:nk], preferred_element_type=F32)
                s_s[0:tq, pl.ds(pl.multiple_of(q0, LANES), tq)] += tri
                s_s[tq:2 * tq, pl.ds(pl.multiple_of(q0, LANES), tq)] += tri
                o_slc = softmax_pv(nk, vslc_s[g, :, 0:nk])
                for hg in range(NSA_GROUP):
                    head = h0 + hg
                    c1 = 9 + 3 * head
                    o_ref[0, :, head * 64:(head + 1) * 64] = (part[hg]
                                                              + gates[:, c1:c1 + 1] * o_slc[hg * tq:(hg + 1) * tq])


def _block_expander(nsp, nkeys):
    blk = lax.broadcasted_iota(jnp.int32, (nsp, nkeys), 0)
    key = lax.broadcasted_iota(jnp.int32, (nsp, nkeys), 1)
    return jnp.where(key // SLC_BLOCK == blk, 1.0, 0.0).astype(BF16)


def _nsa_prompt(q, small, kcvc, slc, win, *, tq, layer):
    b, l, _ = q.shape
    m = kcvc.shape[1]
    nc = m - 1
    ns = -(-l // SLC_BLOCK)
    nsr = -(-ns // SUBLANES) * SUBLANES
    assert nsr <= SEL_LANES
    wk = min(WINDOW + tq, l)
    kern = functools.partial(_nsa_prompt_kernel, tq=tq, l=l, nc=nc, ns=ns, nsr=nsr, wk=wk, kchunk=min(NSA_KEY_CHUNK, l))
    seq = lambda rows: pl.BlockSpec((None, 1, rows, l), lambda i, t: (layer, i, 0, 0))
    tile = lambda wd: pl.BlockSpec((1, tq, wd), lambda i, t: (i, t, 0))
    return pl.pallas_call(
        kern,
        grid=(b, l // tq),
        in_specs=[tile(256), tile(LANES), pl.BlockSpec((1, m, 256), lambda i, t: (i, 0, 0)), seq(256), seq(256),
                  pl.BlockSpec((SEL_LANES, l), lambda i, t: (0, 0))],
        out_specs=tile(256),
        out_shape=jax.ShapeDtypeStruct((b, l, 256), F32),
        scratch_shapes=[pltpu.VMEM((NSA_KV_HEADS, 64 + SEL_LANES, l), BF16),
                        pltpu.VMEM((NSA_KV_HEADS, 64 + DEN_ROWS, l), BF16),
                        pltpu.VMEM((NSA_KV_HEADS, 64, l), BF16),
                        pltpu.VMEM((NSA_KV_HEADS, 64 + DEN_ROWS, l), BF16),
                        pltpu.VMEM((2 * tq, l), F32),
                        pltpu.VMEM((2 * tq, l), BF16)],
        compiler_params=_params("parallel", "arbitrary"),
    )(q, small, kcvc, slc, win, _block_expander(SEL_LANES, l))


def _later_keys_matrix(n):
    a = lax.broadcasted_iota(jnp.int32, (n, n), 0)
    c = lax.broadcasted_iota(jnp.int32, (n, n), 1)
    return jnp.where(a > c, 1.0, 0.0).astype(BF16)


SB_STRIP = 32


def _sb_prompt_kernel(q_ref, kv_ref, later2_ref, o_ref, acc_s, aft_s, z_s, lf_s, hl_s, loc_s, a_s, *, tq):
    qi = pl.program_id(1)
    q = q_ref[0] * SCALE
    acc_s[...] = jnp.zeros_like(acc_s)
    aft_s[...] = jnp.zeros_like(aft_s)
    strips = [slice(r, r + SB_STRIP) for r in range(0, tq, SB_STRIP)]

    def block(k0, diagonal):
        for h in range(N_HEADS):
            kt = kv_ref[0, h * 64:(h + 1) * 64, pl.ds(k0, tq)]
            z_s[h] = _bdot(q[:, h * 64:(h + 1) * 64], kt)
        for h in range(N_HEADS):
            for rows in strips:
                z = z_s[h, rows, :]
                lf = -(jnp.maximum(z, 0.0) + jnp.log(1.0 + jnp.exp(-jnp.abs(z))))
                if diagonal:
                    keep = (lax.broadcasted_iota(jnp.int32, (SB_STRIP, tq), 1)
                            < rows.start + lax.broadcasted_iota(jnp.int32, (SB_STRIP, tq), 0))
                    lf = jnp.where(keep, lf, 0.0)
                hi = lf.astype(BF16)
                lf_s[h, rows, :] = lf
                hl_s[h, rows, 0:tq] = hi
                hl_s[h, rows, tq:2 * tq] = (lf - hi.astype(F32)).astype(BF16)
        for h in range(N_HEADS):
            loc_s[h] = jnp.dot(hl_s[h], later2_ref[...], preferred_element_type=F32)
        for h in range(N_HEADS):
            for rows in strips:
                a = jnp.exp(z_s[h, rows, :] + lf_s[h, rows, :] + (loc_s[h, rows, :] + aft_s[h, rows, :]))
                if diagonal:
                    keep = (lax.broadcasted_iota(jnp.int32, (SB_STRIP, tq), 1)
                            < rows.start + lax.broadcasted_iota(jnp.int32, (SB_STRIP, tq), 0))
                    a = jnp.where(keep, a, 0.0)
                a_s[h, rows, :] = a.astype(BF16)
        for h in range(N_HEADS):
            vt = kv_ref[0, 256 + h * 64:256 + (h + 1) * 64, pl.ds(k0, tq)]
            acc_s[:, h * 64:(h + 1) * 64] += _bdot_nt(a_s[h], vt)
            aft_s[h] = aft_s[h] + loc_s[h, :, 0:1] + lf_s[h, :, 0:1]

    block(pl.multiple_of(qi * tq, tq), True)

    def body(it, carry):
        block(pl.multiple_of((qi - 1 - it) * tq, tq), False)
        return carry

    lax.fori_loop(0, qi, body, 0)
    o_ref[0] = acc_s[...]


def _sb_prompt(q, kv, *, tq, layer):
    b, l, _ = q.shape
    kern = functools.partial(_sb_prompt_kernel, tq=tq)
    return pl.pallas_call(
        kern,
        grid=(b, l // tq),
        in_specs=[pl.BlockSpec((1, tq, 256), lambda i, t: (i, t, 0)),
                  pl.BlockSpec((None, 1, 512, l), lambda i, t: (layer, i, 0, 0)),
                  pl.BlockSpec((2 * tq, tq), lambda i, t: (0, 0))],
        out_specs=pl.BlockSpec((1, tq, 256), lambda i, t: (i, t, 0)),
        out_shape=jax.ShapeDtypeStruct((b, l, 256), F32),
        scratch_shapes=[pltpu.VMEM((tq, 256), F32), pltpu.VMEM((N_HEADS, tq, 1), F32),
                        pltpu.VMEM((N_HEADS, tq, tq), F32), pltpu.VMEM((N_HEADS, tq, tq), F32),
                        pltpu.VMEM((N_HEADS, tq, 2 * tq), BF16), pltpu.VMEM((N_HEADS, tq, tq), F32),
                        pltpu.VMEM((N_HEADS, tq, tq), BF16)],
        compiler_params=_params("parallel", "arbitrary"),
    )(q, kv, jnp.concatenate([_later_keys_matrix(tq)] * 2, axis=0))


SB_PAGE_GROUP = 8


def _sb_sample_kernel(pt_ref, qbd_ref, new_ref, later_ref, *rest, pg, n_pg, pos0):
    pages = rest[:pg]
    o_ref, acc_s, aft_s = rest[pg:]
    g = pl.program_id(1)
    qbd = (qbd_ref[0] * SCALE).astype(BF16)
    later = later_ref[...]
    rows = qbd.shape[0]

    def blocks(kv_refs, mask):
        n = len(kv_refs)
        z = jnp.concatenate([jnp.dot(qbd, r[0, 0:256, :].astype(BF16), preferred_element_type=F32)
                             for r in kv_refs], axis=0)
        lf = -(jnp.maximum(z, 0.0) + jnp.log(1.0 + jnp.exp(-jnp.abs(z))))
        if mask is not None:
            lf = jnp.where(mask, lf, 0.0)
        loc = _split_dot(lf, later)
        aft = aft_s[...]
        acc = acc_s[...]
        for i, r in enumerate(kv_refs):
            blk = slice(i * rows, (i + 1) * rows)
            a = jnp.exp(z[blk] + lf[blk] + (loc[blk] + aft))
            if mask is not None:
                a = jnp.where(mask, a, 0.0)
            acc = acc + _bdot_nt(a, r[0, 256:512, :])
            aft = aft + (loc[blk, 0:1] + lf[blk, 0:1])
        acc_s[...] = acc
        aft_s[...] = aft

    @pl.when(g == 0)
    def _():
        acc_s[...] = jnp.zeros_like(acc_s)
        aft_s[...] = jnp.zeros_like(aft_s)
        t = lax.broadcasted_iota(jnp.int32, (rows, PAGE_SIZE), 0) % SUBLANES
        key = lax.broadcasted_iota(jnp.int32, (rows, PAGE_SIZE), 1)
        blocks([new_ref], key < t)

    for i0 in range(0, pg, SB_PAGE_GROUP):
        blocks(pages[i0:i0 + SB_PAGE_GROUP], None)

    @pl.when(g == n_pg - 1)
    def _():
        acc = acc_s[...]
        lane_head = lax.broadcasted_iota(jnp.int32, (SUBLANES, 256), 1) // HEAD_DIM
        out = jnp.zeros((SUBLANES, 256), F32)
        for h in range(N_HEADS):
            out = out + jnp.where(lane_head == h, acc[h * SUBLANES:(h + 1) * SUBLANES], 0.0)
        o_ref[0] = out


def _sb_sample(page_table, qbd, new_rows, cache, layer_base, *, pg, pos0):
    b, n_pages = page_table.shape
    n_pg = n_pages // pg
    kern = functools.partial(_sb_sample_kernel, pg=pg, n_pg=n_pg, pos0=pos0)

    def page_spec(i):
        return pl.BlockSpec((1, 512, PAGE_SIZE),
                            lambda bi, g, pt: (layer_base + pt[bi, (n_pg - 1 - g) * pg + (pg - 1 - i)], 0, 0))

    grid_spec = pltpu.PrefetchScalarGridSpec(
        num_scalar_prefetch=1,
        grid=(b, n_pg),
        in_specs=[pl.BlockSpec((1, 32, 256), lambda bi, g, pt: (bi, 0, 0)),
                  pl.BlockSpec((1, 512, PAGE_SIZE), lambda bi, g, pt: (bi, 0, 0)),
                  pl.BlockSpec((PAGE_SIZE, PAGE_SIZE), lambda bi, g, pt: (0, 0))]
                 + [page_spec(i) for i in range(pg)],
        out_specs=pl.BlockSpec((1, SUBLANES, 256), lambda bi, g, pt: (bi, 0, 0)),
        scratch_shapes=[pltpu.VMEM((32, 256), F32), pltpu.VMEM((32, 1), F32)],
    )
    return pl.pallas_call(
        kern,
        grid_spec=grid_spec,
        out_shape=jax.ShapeDtypeStruct((b, SUBLANES, 256), F32),
        compiler_params=_params("parallel", "arbitrary"),
    )(page_table, qbd, new_rows, _later_keys_matrix(PAGE_SIZE), *([cache] * pg))


CMP_PAGE_GROUP = 16


def _nsa_sample_cmp_kernel(pt_ref, q_ref, pe_ref, w_ref, w2_ref, *rest, n_pages, pos0, nsp):
    pages = rest[:n_pages]
    ocmp_ref, sel_ref, xs_ref = rest[n_pages:]
    pas, pbs = [], []
    for c in range(0, n_pages, CMP_PAGE_GROUP):
        npg = min(CMP_PAGE_GROUP, n_pages - c)
        for i in range(npg):
            for kv in range(2):
                xs_ref[kv, i * PAGE_SIZE:(i + 1) * PAGE_SIZE, :] = pages[c + i][0, kv * LANES:(kv + 1) * LANES, :].T
        rows = npg * (PAGE_SIZE // CMP_STRIDE)
        halves = []
        for kv in range(2):
            x_kv = jnp.concatenate([xs_ref[kv, pl.ds(t, rows, stride=CMP_STRIDE), :] for t in range(CMP_STRIDE)], axis=1)
            halves.append(_compress_project(x_kv, pe_ref[kv], w_ref[kv]))
        pas.append(jnp.concatenate([halves[0][0], halves[1][0]], axis=1))
        pbs.append(jnp.concatenate([halves[0][1], halves[1][1]], axis=1))
    kcvc = _compress_finish(jnp.concatenate(pas, axis=0), jnp.concatenate(pbs, axis=0), w2_ref[...])
    m = kcvc.shape[0]
    nc = m - 1
    ns = 2 * n_pages + 1
    q = q_ref[0]
    tq = q.shape[0]
    qpos = pos0 + lax.broadcasted_iota(jnp.int32, (tq, 1), 0)
    qpos2 = jnp.concatenate([qpos, qpos], axis=0)
    n_id = lax.broadcasted_iota(jnp.int32, (2 * tq, m), 1)
    cmask = (n_id * CMP_STRIDE + CMP_LEN - 1 <= qpos2) & (n_id < nc)
    ovl = _overlap_matrix(m, nsp, nc, ns)
    for g in range(NSA_KV_HEADS):
        h0 = g * NSA_GROUP
        qs = jnp.concatenate([q[:, h0 * 64:(h0 + 1) * 64], q[:, (h0 + 1) * 64:(h0 + 2) * 64]], axis=0)
        pc = _masked_softmax(_bdot_nt(qs, kcvc[:, g * 64:(g + 1) * 64]) * SCALE, cmask)
        o_cmp = _bdot(pc, kcvc[:, 128 + g * 64:128 + (g + 1) * 64])
        sel_ref[0, g] = _select_bias(_fdot(pc[:tq] + pc[tq:], ovl), qpos, ns)
        for hg in range(NSA_GROUP):
            ocmp_ref[0, :, (h0 + hg) * 64:(h0 + hg + 1) * 64] = o_cmp[hg * tq:(hg + 1) * tq]


def _nsa_sample_cmp(page_table, q, pe, w, w2, cache, layer_base, *, pos0):
    b, n_pages = page_table.shape
    nsp = -(-(2 * n_pages + 1) // LANES) * LANES
    kern = functools.partial(_nsa_sample_cmp_kernel, n_pages=n_pages, pos0=pos0, nsp=nsp)
    full = lambda a: pl.BlockSpec(a.shape, lambda bi, pt: (0,) * a.ndim)

    def page_spec(i):
        return pl.BlockSpec((1, 256, PAGE_SIZE), lambda bi, pt: (layer_base + pt[bi, i], 0, 0))

    grid_spec = pltpu.PrefetchScalarGridSpec(
        num_scalar_prefetch=1,
        grid=(b,),
        in_specs=[pl.BlockSpec((1, SUBLANES, 256), lambda bi, pt: (bi, 0, 0)),
                  full(pe), full(w), full(w2)] + [page_spec(i) for i in range(n_pages)],
        out_specs=[pl.BlockSpec((1, SUBLANES, 256), lambda bi, pt: (bi, 0, 0)),
                   pl.BlockSpec((1, NSA_KV_HEADS, SUBLANES, nsp), lambda bi, pt: (bi, 0, 0, 0))],
        scratch_shapes=[pltpu.VMEM((2, CMP_PAGE_GROUP * PAGE_SIZE, LANES), F32)],
    )
    return pl.pallas_call(
        kern,
        grid_spec=grid_spec,
        out_shape=[jax.ShapeDtypeStruct((b, SUBLANES, 256), F32),
                   jax.ShapeDtypeStruct((b, NSA_KV_HEADS, SUBLANES, nsp), F32)],
        compiler_params=_params("parallel"),
    )(page_table, q, pe, w, w2, *([cache] * n_pages))


def _softmax_rows(s):
    e = jnp.exp(s - jnp.max(s, axis=-1, keepdims=True))
    return e * (1.0 / jnp.sum(e, axis=-1, keepdims=True))


def _nsa_sample_slc_kernel(pt_ref, q_ref, sm_ref, ocmp_ref, sel_ref, newslc_ref, pastwin_ref, newwin_ref,
                           *rest, n_pages, pos0):
    pages = rest[:n_pages]
    o_ref, z_s = rest[n_pages:]
    q = q_ref[0] * SCALE
    tq = q.shape[0]
    gates = _sigmoid(sm_ref[0])
    nk = (n_pages + 1) * PAGE_SIZE
    past = pastwin_ref.shape[2]
    t2 = lax.broadcasted_iota(jnp.int32, (2 * tq, 1), 0) % tq
    kposw = jnp.concatenate([lax.broadcasted_iota(jnp.int32, (2 * tq, past), 1) - past,
                             lax.broadcasted_iota(jnp.int32, (2 * tq, PAGE_SIZE), 1)], axis=1)
    dwin = t2 - kposw
    wbias = jnp.where((dwin >= 0) & (dwin < WINDOW), 0.0, NEG)
    lane = lax.broadcasted_iota(jnp.int32, (2 * tq, PAGE_SIZE), 1)
    lower_block = lane < SLC_BLOCK
    new_causal = jnp.where(lane <= t2, 0.0, NEG)
    for g in range(NSA_KV_HEADS):
        h0 = g * NSA_GROUP
        krow = slice(g * 64, (g + 1) * 64)
        vrow = slice(128 + g * 64, 128 + (g + 1) * 64)
        qs = jnp.concatenate([q[:, h0 * 64:(h0 + 1) * 64], q[:, (h0 + 1) * 64:(h0 + 2) * 64]], axis=0).astype(BF16)
        bias = jnp.concatenate([sel_ref[0, g], sel_ref[0, g]], axis=0)
        dot = lambda k_t: jnp.dot(qs, k_t.astype(BF16), preferred_element_type=F32)
        for p in range(n_pages):
            z_s[:, p * PAGE_SIZE:(p + 1) * PAGE_SIZE] = dot(pages[p][0, krow, :]) + jnp.where(
                lower_block, bias[:, 2 * p:2 * p + 1], bias[:, 2 * p + 1:2 * p + 2])
        z_s[:, n_pages * PAGE_SIZE:nk] = dot(newslc_ref[0, krow, :]) + bias[:, 2 * n_pages:2 * n_pages + 1] + new_causal
        p2 = _softmax_rows(z_s[...])
        o_slc = _bdot_nt(p2[:, n_pages * PAGE_SIZE:nk], newslc_ref[0, vrow, :])
        for p in range(n_pages):
            o_slc = o_slc + _bdot_nt(p2[:, p * PAGE_SIZE:(p + 1) * PAGE_SIZE], pages[p][0, vrow, :])
        p3 = _softmax_rows(jnp.concatenate([dot(pastwin_ref[0, krow, :]), dot(newwin_ref[0, krow, :])], axis=1) + wbias)
        o_win = _bdot_nt(p3[:, 0:past], pastwin_ref[0, vrow, :]) + _bdot_nt(p3[:, past:], newwin_ref[0, vrow, :])
        for hg in range(NSA_GROUP):
            head = h0 + hg
            rows = slice(hg * tq, (hg + 1) * tq)
            c0 = 8 + 3 * head
            o_ref[0, :, head * 64:(head + 1) * 64] = (gates[:, c0:c0 + 1] * ocmp_ref[0, :, head * 64:(head + 1) * 64]
                                                      + gates[:, c0 + 1:c0 + 2] * o_slc[rows]
                                                      + gates[:, c0 + 2:c0 + 3] * o_win[rows])


def _nsa_sample_slc(page_table, q, small, o_cmp, sel, new_slc, past_win, past_win_base, new_win, cache, layer_base,
                    *, pos0):
    b, n_pages = page_table.shape
    nsp = sel.shape[-1]
    nk = (n_pages + 1) * PAGE_SIZE
    past = past_win.shape[2]
    kern = functools.partial(_nsa_sample_slc_kernel, n_pages=n_pages, pos0=pos0)
    per_b = lambda r, wd: pl.BlockSpec((1, r, wd), lambda bi, pt: (bi, 0, 0))

    def page_spec(i):
        return pl.BlockSpec((1, 256, PAGE_SIZE), lambda bi, pt: (layer_base + pt[bi, i], 0, 0))

    grid_spec = pltpu.PrefetchScalarGridSpec(
        num_scalar_prefetch=1,
        grid=(b,),
        in_specs=[per_b(SUBLANES, 256), per_b(SUBLANES, LANES), per_b(SUBLANES, 256),
                  pl.BlockSpec((1, NSA_KV_HEADS, SUBLANES, nsp), lambda bi, pt: (bi, 0, 0, 0)),
                  per_b(256, PAGE_SIZE),
                  pl.BlockSpec((1, 256, past), lambda bi, pt: (past_win_base + bi, 0, 0)),
                  per_b(256, PAGE_SIZE)] + [page_spec(i) for i in range(n_pages)],
        out_specs=per_b(SUBLANES, 256),
        scratch_shapes=[pltpu.VMEM((2 * SUBLANES, nk), F32)],
    )
    return pl.pallas_call(
        kern,
        grid_spec=grid_spec,
        out_shape=jax.ShapeDtypeStruct((b, SUBLANES, 256), F32),
        compiler_params=_params("parallel"),
    )(page_table, q, small, o_cmp, sel, new_slc, past_win, new_win, *([cache] * n_pages))


def _transpose_cast_kernel(x_ref, o_ref):
    o_ref[...] = x_ref[...].T.astype(BF16)


def _transpose_cast(x, tr=384):
    r, c = x.shape
    return pl.pallas_call(
        _transpose_cast_kernel,
        grid=(r // tr,),
        in_specs=[pl.BlockSpec((tr, c), lambda i: (i, 0))],
        out_specs=pl.BlockSpec((c, tr), lambda i: (0, i)),
        out_shape=jax.ShapeDtypeStruct((c, r), BF16),
        compiler_params=_params("parallel"),
    )(x)


def _prep_layer(l, P):
    w_t = P["w_in_t"][l]
    rows = lambda lo, hi: w_t[lo:hi]
    small = jnp.concatenate([rows(1024, 1032), rows(2568, 2580), jnp.zeros((LANES - 20, D_MODEL), F32)], axis=0)
    w_perm = _transpose_cast(jnp.concatenate([rows(0, 1024), rows(1032, 1544), rows(1544, 1800), rows(2580, 2836),
                                              rows(2836, 3348), rows(1800, 2568), small], axis=0))
    par = jnp.zeros((2, LANES), F32).at[0, 4:8].set(P["gdn_a_log"][l]).at[1, 4:8].set(P["gdn_dt_bias"][l])

    def block_diag(w):
        return jnp.einsum("ncd,nm->ncmd", w, jnp.eye(4, dtype=F32)).reshape(GROUP_WIDTH, GROUP_WIDTH).astype(BF16)

    eye2 = jnp.eye(2, dtype=F32)
    w1 = P["nsa_cmp_w1"][l]
    half = CMP_LEN // 2

    cmp_w = jnp.einsum("khtde,gG->ktgdhGe", w1.reshape(2, 2, half, HEAD_DIM, HEAD_DIM), eye2)
    cmp_w = cmp_w.reshape(2, half * 128, 256).astype(BF16)
    pe = P["nsa_cmp_pe"][l].reshape(2, 2, half, 1, HEAD_DIM)
    cmp_pe = jnp.broadcast_to(pe, (2, 2, half, 2, HEAD_DIM)).reshape(2, 2, half * 128)
    w2 = jnp.einsum("kef,kK,gG->kgeKGf", P["nsa_cmp_w2"][l], eye2, eye2).reshape(256, 256).astype(BF16)
    w_kv_t = jnp.concatenate([rows(2836, 3348), rows(1800, 2568)], axis=0).astype(BF16)
    return dict(
        norm1=P["norm1"][l][None], w_in=w_perm, w_kv_t=w_kv_t,
        gdn_cw=P["gdn_conv_w"][l], gdn_par=par, gdn_norm=P["gdn_norm"][l][None],
        lru_cw=P["lru_conv_w"][l],
        lru_vec=jnp.stack([P["lru_conv_b"][l], P["lru_ba"][l], P["lru_bx"][l], P["lru_lambda"][l]]),
        lru_wa=block_diag(P["lru_wa"][l]), lru_wx=block_diag(P["lru_wx"][l]),
        cmp_pe=cmp_pe, cmp_w=cmp_w, cmp_w2=w2,
        w_out=P["w_out"][l].astype(BF16), norm2=P["norm2"][l][None],
        wg=P["ffn_w_gate"][l].astype(BF16), wu=P["ffn_w_up"][l].astype(BF16),
        ffn_cw=P["ffn_conv_w"][l], wd=P["ffn_w_down"][l].astype(BF16),
    )


def _rope_tables(pos):
    half = HEAD_DIM // 2
    inv = ROPE_THETA ** (-jnp.arange(half, dtype=F32) / half)
    ang = pos.astype(F32)[:, None] * inv[None, :]
    c, s = jnp.cos(ang), jnp.sin(ang)
    return jnp.tile(jnp.concatenate([c, c], axis=-1), (1, 2)), jnp.tile(jnp.concatenate([-s, s], axis=-1), (1, 2))


PROMPT_TM = 256
PROMPT_TQ = 256
SB_TQ = 256
GDN_CHUNK = 64


def _layer_prompt(x2d, W, b, l, cos, sin, final, norm_f, layer, depth, kv_stacks):
    outs = _in_proj_t(x2d, W["norm1"], W["w_in"], W["w_kv_t"], cos, sin, b, l, PROMPT_TM, layer, depth, kv_stacks)
    zg, zl, nq, sq, cmp_r, small = [o.reshape(b, l, o.shape[-1]) for o in outs[:6]]
    kv_stacks = outs[6:]
    skv_t, _, slc_t, win_t = kv_stacks
    o_gdn, gdn_conv, gdn_s = _gdn(zg, small, jnp.zeros((b, SUBLANES, 768), F32), jnp.zeros((b, N_HEADS, 64, 64), F32),
                                  W["gdn_cw"], W["gdn_par"], W["gdn_norm"], tl=PROMPT_TM, chunk=GDN_CHUNK,
                                  l_real=PROMPT_TM)
    o_lru, lru_conv, lru_h = _lru(zl, jnp.zeros((b, SUBLANES, GROUP_WIDTH), F32), jnp.zeros((b, 1, GROUP_WIDTH), F32),
                                  W["lru_cw"], W["lru_vec"], W["lru_wa"], W["lru_wx"], tl=PROMPT_TM, l_real=PROMPT_TM)
    kcvc = _cmp_prompt(cmp_r.reshape(b, l // CMP_STRIDE, CMP_STRIDE * 256), W["cmp_pe"], W["cmp_w"], W["cmp_w2"])
    o_nsa = _nsa_prompt(nq, small, kcvc, slc_t, win_t, tq=min(PROMPT_TQ, l), layer=layer)
    o_sb = _sb_prompt(sq, skv_t, tq=min(SB_TQ, l), layer=layer)
    mixes = [o.reshape(b * l, GROUP_WIDTH) for o in (o_gdn, o_lru, o_nsa, o_sb)]
    res = _out_ffn(x2d, mixes, W["w_out"], W["norm2"], W["wg"], W["wu"], W["ffn_cw"], W["wd"],
                   jnp.zeros((SUBLANES, D_FF), F32), norm_f, tm=PROMPT_TM, stride=1, tiles_per_seq=l // PROMPT_TM,
                   final=final)
    states = (gdn_conv, gdn_s, lru_conv, lru_h[:, 0], res[1][l // PROMPT_TM - 1::l // PROMPT_TM])
    return res[0], kv_stacks, states, (res[2] if final else None)


def _prompt_kv_outputs(kv_stacks, l):
    skv_t, cmp_t, slc_t, win_t = kv_stacks
    kv6 = lambda a, h: jnp.transpose(a.reshape(a.shape[0], a.shape[1], 2, h, HEAD_DIM, a.shape[-1]), (0, 1, 5, 2, 3, 4))
    return kv6(cmp_t, 2), kv6(slc_t, 2), kv6(skv_t, 4), kv6(win_t[..., l - min(WINDOW, l):], 2)


def _layer_sample(x2d, W, layer, S, cos, sin, final, norm_f):
    page_table = S["page_table"]
    b, n_pages = page_table.shape
    t_new = x2d.shape[0] // b
    pos0 = n_pages * PAGE_SIZE
    n_phys = S["n_phys"]
    outs = _in_proj(x2d, W["norm1"], W["w_in"], cos, sin, x2d.shape[0], 1)

    def to_batch_major(a, rows=SUBLANES):
        a = a.reshape(t_new, b, a.shape[-1]).transpose(1, 0, 2)
        return jnp.pad(a, ((0, 0), (0, rows - t_new), (0, 0)))

    zg, zl, nq, sq, skv, cmp_r, slc_r, win_r, small = [to_batch_major(o) for o in outs]
    front = lambda st: jnp.pad(st, ((0, 0), (SUBLANES - st.shape[1], 0), (0, 0)))
    o_gdn, gdn_conv, gdn_s = _gdn(zg, small, front(S["gdn_conv"][layer]), S["gdn"][layer],
                                  W["gdn_cw"], W["gdn_par"], W["gdn_norm"], tl=SUBLANES, chunk=SUBLANES, l_real=t_new)
    o_lru, lru_conv, lru_h = _lru(zl, front(S["lru_conv"][layer]), S["lru"][layer][:, None, :],
                                  W["lru_cw"], W["lru_vec"], W["lru_wa"], W["lru_wx"], tl=SUBLANES, l_real=t_new)
    base = layer * n_phys
    o_cmp, sel = _nsa_sample_cmp(page_table, nq, W["cmp_pe"], W["cmp_w"], W["cmp_w2"],
                                 S["cmp_pages"], base, pos0=pos0)
    pad_page = lambda a: jnp.pad(a.transpose(0, 2, 1), ((0, 0), (0, 0), (0, PAGE_SIZE - a.shape[1])))
    o_nsa = _nsa_sample_slc(page_table, nq, small, o_cmp, sel, pad_page(slc_r), S["win"], layer * b, pad_page(win_r),
                            S["slc_pages"], base, pos0=pos0)
    head_of_col = jnp.arange(256) // HEAD_DIM
    head_of_row = jnp.arange(N_HEADS * SUBLANES) // SUBLANES
    qbd = jnp.where(head_of_row[:, None] == head_of_col[None, :], jnp.tile(sq, (1, N_HEADS, 1)), 0.0)
    o_sb = _sb_sample(page_table, qbd, pad_page(skv), S["sb_pages"], base, pg=min(32, n_pages), pos0=pos0)

    def to_time_major(a):
        return a[:, :t_new].transpose(1, 0, 2).reshape(t_new * b, a.shape[-1])

    mixes = [to_time_major(o) for o in (o_gdn, o_lru, o_nsa, o_sb)]
    buf = S["ffn_conv"][layer].transpose(1, 0, 2).reshape(2 * b, D_FF)
    res = _out_ffn(x2d, mixes, W["w_out"], W["norm2"], W["wg"], W["wu"], W["ffn_cw"], W["wd"], buf, norm_f,
                   tm=x2d.shape[0], stride=b, tiles_per_seq=1, final=final)
    kv5 = lambda a, h: a[:, :t_new].reshape(b, t_new, 2, h, HEAD_DIM)
    ffn_state = res[1].reshape(2, b, D_FF).transpose(1, 0, 2)
    states = (kv5(cmp_r, 2), kv5(slc_r, 2), kv5(skv, 4), kv5(win_r, 2), gdn_conv, gdn_s, lru_conv, lru_h[:, 0], ffn_state)
    return res[0], states, (res[2] if final else None)


def _sample_state_views(cache_cmp_kv, cache_slc_kv, cache_sb_kv, cache_win_kv):
    def view(c):
        d0, d1, tok, kv, h, d = c.shape
        return jnp.transpose(c, (0, 1, 3, 4, 5, 2)).reshape(d0 * d1, kv * h * d, tok)

    return dict(cmp_pages=view(cache_cmp_kv), slc_pages=view(cache_slc_kv), sb_pages=view(cache_sb_kv),
                win=view(cache_win_kv))


def kernel(x_prompt, x_sample, cache_cmp_kv, cache_slc_kv, cache_sb_kv, cache_win_kv, state_gdn_conv, state_gdn, state_lru_conv, state_lru, state_ffn_conv, page_table, norm1, w_in, gdn_conv_w, gdn_a_log, gdn_dt_bias, gdn_norm, lru_conv_w, lru_conv_b, lru_wa, lru_ba, lru_wx, lru_bx, lru_lambda, nsa_cmp_pe, nsa_cmp_w1, nsa_cmp_w2, w_out, norm2, ffn_w_gate, ffn_w_up, ffn_conv_w, ffn_w_down, norm_f):
    P = dict(norm1=norm1, w_in_t=jnp.transpose(w_in, (0, 2, 1)), gdn_conv_w=gdn_conv_w, gdn_a_log=gdn_a_log, gdn_dt_bias=gdn_dt_bias,
             gdn_norm=gdn_norm, lru_conv_w=lru_conv_w, lru_conv_b=lru_conv_b, lru_wa=lru_wa, lru_ba=lru_ba,
             lru_wx=lru_wx, lru_bx=lru_bx, lru_lambda=lru_lambda, nsa_cmp_pe=nsa_cmp_pe, nsa_cmp_w1=nsa_cmp_w1,
             nsa_cmp_w2=nsa_cmp_w2, w_out=w_out, norm2=norm2, ffn_w_gate=ffn_w_gate, ffn_w_up=ffn_w_up,
             ffn_conv_w=ffn_conv_w, ffn_w_down=ffn_w_down)
    depth = w_in.shape[0]
    weights = [_prep_layer(l, P) for l in range(depth)]
    gf = norm_f[None]

    b, l, _ = x_prompt.shape
    cos, sin = _rope_tables(jnp.arange(l))
    h = x_prompt.reshape(b * l, D_MODEL)
    p_states, y_prompt, kv_stacks = [], None, None
    for layer in range(depth):
        h, kv_stacks, st, y = _layer_prompt(h, weights[layer], b, l, cos, sin, layer == depth - 1, gf, layer, depth,
                                            kv_stacks)
        p_states.append(st)
        y_prompt = y
    y_prompt = y_prompt.reshape(b, l, D_MODEL)
    p_kv = _prompt_kv_outputs(kv_stacks, l)

    db, t_new, _ = x_sample.shape
    n_phys = cache_sb_kv.shape[1]
    n_pages = page_table.shape[1]
    pos0 = n_pages * PAGE_SIZE
    past_win = cache_win_kv.shape[2]
    S = _sample_state_views(cache_cmp_kv, cache_slc_kv, cache_sb_kv, cache_win_kv)
    S.update(page_table=page_table, n_phys=n_phys,
             gdn_conv=state_gdn_conv, gdn=state_gdn, lru_conv=state_lru_conv, lru=state_lru, ffn_conv=state_ffn_conv)
    cos_s, sin_s = _rope_tables(pos0 + jnp.repeat(jnp.arange(t_new), db))
    h = x_sample.transpose(1, 0, 2).reshape(t_new * db, D_MODEL)
    s_states, y_sample = [], None
    for layer in range(depth):
        h, st, y = _layer_sample(h, weights[layer], layer, S, cos_s, sin_s, layer == depth - 1, gf)
        s_states.append(st)
        y_sample = y
    y_sample = y_sample.reshape(t_new, db, D_MODEL).transpose(1, 0, 2)

    stk = lambda states, i: jnp.stack([s[i] for s in states])
    return ((y_prompt, y_sample) + p_kv + tuple(stk(p_states, i) for i in range(5))
            + tuple(stk(s_states, i) for i in range(9)))
```

```python
import functools
import math

import jax
import jax.numpy as jnp
from jax import lax
from jax.experimental import pallas as pl
from jax.experimental.pallas import tpu as pltpu

F32, BF16 = jnp.float32, jnp.bfloat16

D_MODEL = 1024
HEAD_DIM = 64
GROUP_WIDTH = 256
N_HEADS = 4
NSA_KV_HEADS = 2
NSA_GROUP = 2
D_FF = 2816
SHORT_CONV = 4
FFN_CONV = 3
LRU_C = 8.0
CMP_STRIDE = 16
CMP_LEN = 32
SLC_BLOCK = 64
SLC_TOPK = 16
N_LOCAL = 2
WINDOW = 512
FORCE_BONUS = 1.0e3
ROPE_THETA = 10000.0
EPS = 1e-6
NEG = -1e30
SCALE = HEAD_DIM ** -0.5
PAGE_SIZE = 128

VMEM_LIMIT_BYTES = 56 * 1024 * 1024
LANES = 128
SUBLANES = 8

IN_WIDTHS = (1024, 512, 256, 256, 512, 256, 256, 256, 128)
D_IN_PAD = sum(IN_WIDTHS)


def _params(*sem):
    return pltpu.CompilerParams(dimension_semantics=sem, vmem_limit_bytes=VMEM_LIMIT_BYTES)


def _bdot(a, b):
    return jnp.dot(a.astype(BF16), b.astype(BF16), preferred_element_type=F32)


def _bdot_nt(a, b):
    return lax.dot_general(a.astype(BF16), b.astype(BF16), (((1,), (1,)), ((), ())), preferred_element_type=F32)


def _bdot_tn(a, b):
    return lax.dot_general(a.astype(BF16), b.astype(BF16), (((0,), (0,)), ((), ())), preferred_element_type=F32)


def _fdot(a, b):
    return jnp.dot(a, b, preferred_element_type=F32, precision=lax.Precision.HIGHEST)


def _dot3(a, b):
    ah, bh = a.astype(BF16), b.astype(BF16)
    al, bl = (a - ah.astype(F32)).astype(BF16), (b - bh.astype(F32)).astype(BF16)
    dot = lambda x, y: jnp.dot(x, y, preferred_element_type=F32)
    return dot(ah, bh) + (dot(ah, bl) + dot(al, bh))


GDN_GROUP_CHUNKS = 2


def _split_dot(a, b_exact):
    hi = a.astype(BF16)
    lo = (a - hi.astype(F32)).astype(BF16)
    return (jnp.dot(hi, b_exact, preferred_element_type=F32) + jnp.dot(lo, b_exact, preferred_element_type=F32))


def _sigmoid(x):
    return jax.nn.sigmoid(x)


def _silu(x):
    return x * jax.nn.sigmoid(x)


def _softplus(x):
    return jnp.maximum(x, 0.0) + jnp.log1p(jnp.exp(-jnp.abs(x)))


def _gelu_tanh(x):
    return 0.5 * x * (1.0 + jnp.tanh(math.sqrt(2.0 / math.pi) * (x + 0.044715 * (x * x * x))))


def _rms(x, g):
    return x * lax.rsqrt(jnp.mean(x * x, axis=-1, keepdims=True) + EPS) * g


def _rope_pairs(v, cos, sin):
    lane = lax.broadcasted_iota(jnp.int32, v.shape, 1)
    first_half = (lane % HEAD_DIM) < (HEAD_DIM // 2)
    partner = jnp.where(first_half, pltpu.roll(v, LANES - HEAD_DIM // 2, 1), pltpu.roll(v, HEAD_DIM // 2, 1))
    return v * cos + partner * sin


def _masked_softmax(s, mask):
    sm = jnp.where(mask, s, NEG)
    m = jnp.max(sm, axis=-1, keepdims=True)
    e = jnp.where(mask, jnp.exp(sm - m), 0.0)
    den = jnp.sum(e, axis=-1, keepdims=True)
    return e * (1.0 / jnp.where(den > 0.0, den, 1.0))


def _in_proj_kernel(x_ref, g_ref, w_ref, cos_ref, sin_ref,
                    zg_ref, zl_ref, nq_ref, sq_ref, skv_ref, cmp_ref, slc_ref, win_ref, sm_ref):
    xb = _rms(x_ref[...], g_ref[...]).astype(BF16)
    cos, sin = cos_ref[...], sin_ref[...]

    def mm(lo, width):
        return jnp.dot(xb, w_ref[:, lo:lo + width], preferred_element_type=F32)

    zg_ref[...] = mm(0, 1024)
    zl_ref[...] = mm(1024, 512)
    q = mm(1536, 256)
    nq_ref[:, 0:128] = _rope_pairs(q[:, 0:128], cos, sin)
    nq_ref[:, 128:256] = _rope_pairs(q[:, 128:256], cos, sin)
    sq_ref[...] = mm(1792, 256)
    skv_ref[...] = mm(2048, 512)
    for ref, lo in ((cmp_ref, 2560), (slc_ref, 2816), (win_ref, 3072)):
        kv = mm(lo, 256)
        ref[:, 0:128] = _rope_pairs(kv[:, 0:128], cos, sin)
        ref[:, 128:256] = kv[:, 128:256]
    sm_ref[...] = mm(3328, 128)


def _in_proj(x2d, g, w, cos, sin, tm, table_tiles):
    n = x2d.shape[0]
    nt = n // tm
    row = lambda wd: pl.BlockSpec((tm, wd), lambda i: (i, 0))
    tab = pl.BlockSpec((tm, LANES), lambda i: (i % table_tiles, 0))
    return pl.pallas_call(
        _in_proj_kernel,
        grid=(nt,),
        in_specs=[row(D_MODEL), pl.BlockSpec((1, D_MODEL), lambda i: (0, 0)),
                  pl.BlockSpec((D_MODEL, D_IN_PAD), lambda i: (0, 0)), tab, tab],
        out_specs=[row(wd) for wd in IN_WIDTHS],
        out_shape=[jax.ShapeDtypeStruct((n, wd), F32) for wd in IN_WIDTHS],
        compiler_params=_params("parallel"),
    )(x2d, g, w, cos, sin)


def _rope_rows(v, cos_t, sin_t):
    q = HEAD_DIM // 2
    partner = jnp.concatenate([v[q:2 * q], v[0:q], v[3 * q:4 * q], v[2 * q:3 * q]], axis=0)
    return v * cos_t + partner * sin_t


KV_T_ROWS = 1280


N_KV_OUT = 4


def _in_proj_t_kernel(x_ref, g_ref, w_ref, wt_ref, cos_ref, sin_ref, cost_ref, sint_ref, *rest):
    (zg_ref, zl_ref, nq_ref, sq_ref, cmp_ref, sm_ref, skvt_ref, cmpt_ref, slct_ref, wint_ref) = rest[-(6 + N_KV_OUT):]
    xb = _rms(x_ref[...], g_ref[...]).astype(BF16)
    cos, sin = cos_ref[...], sin_ref[...]
    cos_t, sin_t = cost_ref[...], sint_ref[...]

    def mm(lo, width):
        return jnp.dot(xb, w_ref[:, lo:lo + width], preferred_element_type=F32)

    def mm_t(lo, rows):
        return lax.dot_general(wt_ref[lo:lo + rows, :], xb, (((1,), (1,)), ((), ())), preferred_element_type=F32)

    zg_ref[...] = mm(0, 1024)
    zl_ref[...] = mm(1024, 512)
    q = mm(1536, 256)
    nq_ref[:, 0:128] = _rope_pairs(q[:, 0:128], cos, sin)
    nq_ref[:, 128:256] = _rope_pairs(q[:, 128:256], cos, sin)
    sq_ref[...] = mm(1792, 256)
    kv = mm(2560, 256)
    cmp_ref[:, 0:128] = _rope_pairs(kv[:, 0:128], cos, sin)
    cmp_ref[:, 128:256] = kv[:, 128:256]
    sm_ref[...] = mm(3328, 128)
    skvt_ref[0] = mm_t(0, 512)
    for ref, lo in ((cmpt_ref, 512), (slct_ref, 768), (wint_ref, 1024)):
        kvt = mm_t(lo, 256)
        ref[0, 0:128, :] = _rope_rows(kvt[0:128], cos_t, sin_t)
        ref[0, 128:256, :] = kvt[128:256]


def _in_proj_t(x2d, g, w, wt, cos, sin, b, l, tm, layer, depth, kv_stacks):
    tps = l // tm
    row = lambda wd: pl.BlockSpec((tm, wd), lambda i: (i, 0))
    tab = pl.BlockSpec((tm, LANES), lambda i: (i % tps, 0))
    tab_t = pl.BlockSpec((LANES, tm), lambda i: (0, i % tps))
    kvt = lambda c: pl.BlockSpec((None, 1, c, tm), lambda i: (layer, i // tps, 0, i % tps))
    row_widths = (1024, 512, 256, 256, 256, 128)
    t_rows = (512, 256, 256, 256)
    in_specs = [row(D_MODEL), pl.BlockSpec((1, D_MODEL), lambda i: (0, 0)),
                pl.BlockSpec((D_MODEL, D_IN_PAD), lambda i: (0, 0)),
                pl.BlockSpec((KV_T_ROWS, D_MODEL), lambda i: (0, 0)), tab, tab, tab_t, tab_t]
    args = [x2d, g, w, wt, cos, sin, cos.T, sin.T]
    aliases = {}
    if kv_stacks is not None:
        aliases = {len(args) + k: len(row_widths) + k for k in range(N_KV_OUT)}
        in_specs += [pl.BlockSpec(memory_space=pl.ANY)] * N_KV_OUT
        args += list(kv_stacks)
    return pl.pallas_call(
        _in_proj_t_kernel,
        grid=(b * tps,),
        in_specs=in_specs,
        out_specs=[row(wd) for wd in row_widths] + [kvt(c) for c in t_rows],
        out_shape=[jax.ShapeDtypeStruct((b * l, wd), F32) for wd in row_widths]
                  + [jax.ShapeDtypeStruct((depth, b, c, l), F32) for c in t_rows],
        input_output_aliases=aliases,
        compiler_params=_params("parallel"),
    )(*args)


FFN_SLABS = 2


def _out_ffn_kernel(*refs, tm, tf, nf, stride, prev_rows, tiles_per_seq, final):
    x_ref, mix_refs, refs = refs[0], refs[1:5], refs[5:]
    if final:
        (wo_ref, g2_ref, wg_ref, wu_ref, cw_ref, wd_ref, buf_ref, gf_ref,
         xo_ref, st_ref, y_ref, carry_s) = refs
    else:
        (wo_ref, g2_ref, wg_ref, wu_ref, cw_ref, wd_ref, buf_ref,
         xo_ref, st_ref, carry_s) = refs
    i = pl.program_id(0)
    mix = jnp.concatenate([m[...] for m in mix_refs], axis=1)
    x1 = x_ref[...] + _bdot(mix, wo_ref[...])
    xn = _rms(x1, g2_ref[...]).astype(BF16)
    seq_start = (i % tiles_per_seq) == 0
    p, s = prev_rows, stride
    x2 = x1
    for j in range(nf):
        cols = slice(j * tf, (j + 1) * tf)
        a = jnp.dot(xn, wg_ref[:, cols], preferred_element_type=F32)
        u = jnp.dot(xn, wu_ref[:, cols], preferred_element_type=F32)
        prev = jnp.where(seq_start, buf_ref[:, cols], carry_s[:, cols])
        ext = jnp.concatenate([prev, a], axis=0)
        cw = cw_ref[:, cols]
        ac = cw[0:1] * ext[p - 2 * s:p - 2 * s + tm] + cw[1:2] * ext[p - s:p - s + tm] + cw[2:3] * ext[p:p + tm]
        carry_s[:, cols] = ext[tm:tm + p]
        st_ref[0, :, cols] = ext[tm + p - 2 * s:tm + p]
        x2 = x2 + _bdot(_silu(ac) * u, wd_ref[cols, :])
    xo_ref[...] = x2
    if final:
        y_ref[...] = _rms(x2, gf_ref[...])


def _out_ffn(x2d, mixes, wo, g2, wg, wu, cw, wd, buf, gf, *, tm, stride, tiles_per_seq, final):
    n = x2d.shape[0]
    nt = n // tm
    nf = FFN_SLABS
    tf = D_FF // nf
    prev_rows = buf.shape[0]
    kern = functools.partial(_out_ffn_kernel, tm=tm, tf=tf, nf=nf, stride=stride, prev_rows=prev_rows,
                             tiles_per_seq=tiles_per_seq, final=final)
    tok = pl.BlockSpec((tm, D_MODEL), lambda i: (i, 0))
    part = pl.BlockSpec((tm, GROUP_WIDTH), lambda i: (i, 0))
    const = lambda shape: pl.BlockSpec(shape, lambda i: (0, 0), pipeline_mode=pl.Buffered(1))
    in_specs = [tok, part, part, part, part,
                const((D_MODEL, D_MODEL)), const((1, D_MODEL)), const((D_MODEL, D_FF)), const((D_MODEL, D_FF)),
                const((FFN_CONV, D_FF)), const((D_FF, D_MODEL)), const((prev_rows, D_FF))]
    args = [x2d, *mixes, wo, g2, wg, wu, cw, wd, buf]
    out_specs = [tok, pl.BlockSpec((1, 2 * stride, D_FF), lambda i: (i, 0, 0))]
    out_shape = [jax.ShapeDtypeStruct((n, D_MODEL), F32),
                 jax.ShapeDtypeStruct((nt, 2 * stride, D_FF), F32)]
    if final:
        in_specs.append(const((1, D_MODEL)))
        args.append(gf)
        out_specs.append(tok)
        out_shape.append(jax.ShapeDtypeStruct((n, D_MODEL), F32))
    return pl.pallas_call(
        kern,
        grid=(nt,),
        in_specs=in_specs,
        out_specs=out_specs,
        out_shape=out_shape,
        scratch_shapes=[pltpu.VMEM((prev_rows, D_FF), F32)],
        compiler_params=_params("arbitrary"),
    )(*args)


def _short_conv(x, carry_ref, cw, tl):
    ext = jnp.concatenate([carry_ref[...], x], axis=0)
    y = cw[0:1] * ext[5:5 + tl]
    for i in range(1, SHORT_CONV):
        y = y + cw[i:i + 1] * ext[5 + i:5 + i + tl]
    carry_ref[...] = ext[tl:tl + SUBLANES]
    return y, ext


def _gdn_kernel(zg_ref, sm_ref, cst_ref, s0_ref, cw_ref, par_ref, nrm_ref,
                o_ref, cout_ref, sout_ref, carry_s, state_s, *, tl, chunk, l_real):
    t = pl.program_id(1)

    @pl.when(t == 0)
    def _():
        carry_s[...] = cst_ref[0]
        state_s[...] = s0_ref[0]

    z = zg_ref[0]
    gate = z[:, 768:1024]
    y, ext = _short_conv(z[:, 0:768], carry_s, cw_ref[...], tl)
    cout_ref[0] = ext[SUBLANES + l_real - 3:SUBLANES + l_real]
    y = _silu(y)

    small = sm_ref[0]
    par = par_ref[...]
    beta_c = _sigmoid(small)
    g_c = -jnp.exp(par[0:1]) * _softplus(small + par[1:2])
    if l_real < tl:
        real = lax.broadcasted_iota(jnp.int32, small.shape, 0) < l_real
        beta_c = jnp.where(real, beta_c, 0.0)
        g_c = jnp.where(real, g_c, 0.0)
    ri = lax.broadcasted_iota(jnp.int32, (tl, tl), 0)
    ci = lax.broadcasted_iota(jnp.int32, (tl, tl), 1)
    same = (ri // chunk) == (ci // chunk)
    gc = _fdot(jnp.where(same & (ci <= ri), 1.0, 0.0), g_c)
    grest = _fdot(jnp.where(same & (ci > ri), 1.0, 0.0), g_c)
    gc_t = gc.T

    ii = lax.broadcasted_iota(jnp.int32, (chunk, chunk), 0)
    jj = lax.broadcasted_iota(jnp.int32, (chunk, chunk), 1)
    causal, strict = ii >= jj, ii > jj
    eye = jnp.where(ii == jj, 1.0, 0.0)
    gnorm = nrm_ref[...]
    n_chunks = tl // chunk
    heads = range(N_HEADS)
    S = [state_s[h] for h in heads]
    out_rows = []
    for c0 in range(0, n_chunks, GDN_GROUP_CHUNKS):
        pairs = [(c, h) for c in range(c0, min(c0 + GDN_GROUP_CHUNKS, n_chunks)) for h in heads]
        rows_of = lambda c: slice(c * chunk, (c + 1) * chunk)
        q = [y[rows_of(c), h * 64:(h + 1) * 64] for c, h in pairs]
        k = [y[rows_of(c), 256 + h * 64:256 + (h + 1) * 64] for c, h in pairs]
        v = [y[rows_of(c), 512 + h * 64:512 + (h + 1) * 64] for c, h in pairs]
        q = [x * lax.rsqrt(jnp.sum(x * x, axis=-1, keepdims=True) + EPS) * SCALE for x in q]
        k = [x * lax.rsqrt(jnp.sum(x * x, axis=-1, keepdims=True) + EPS) for x in k]
        beta = [beta_c[rows_of(c), h:h + 1] for c, h in pairs]
        gcc = [gc[rows_of(c), 4 + h:5 + h] for c, h in pairs]
        gcr = [gc_t[4 + h:5 + h, rows_of(c)] for c, h in pairs]
        decay = [jnp.where(causal, jnp.exp(jnp.where(causal, a - b, 0.0)), 0.0) for a, b in zip(gcc, gcr)]
        kb = [a * b for a, b in zip(k, beta)]
        A = [jnp.where(strict, _bdot_nt(a, b) * d, 0.0) for a, b, d in zip(kb, k, decay)]
        T = [eye - a for a in A]
        P = A
        span = 2
        while span < chunk:
            P = [_dot3(p, p) for p in P]
            T = [t + _dot3(t, p) for t, p in zip(T, P)]
            span *= 2
        u = [_bdot(t, a * b) for t, a, b in zip(T, v, beta)]
        w = [_bdot(t, a * jnp.exp(g)) for t, a, g in zip(T, kb, gcc)]
        attn = [_bdot_nt(a, b) * d for a, b, d in zip(q, k, decay)]
        qe = [a * jnp.exp(g) for a, g in zip(q, gcc)]
        kd = [a * jnp.exp(grest[rows_of(c), 4 + h:5 + h]) for a, (c, h) in zip(k, pairs)]
        for i0 in range(0, len(pairs), N_HEADS):
            c = pairs[i0][0]
            v_new = [u[i0 + h] - _bdot(w[i0 + h], S[h]) for h in heads]
            o = [_bdot(qe[i0 + h], S[h]) + _bdot(attn[i0 + h], v_new[h]) for h in heads]
            last = (c + 1) * chunk - 1
            S = [S[h] * jnp.exp(gc[last:last + 1, 4 + h:5 + h]) + _bdot_tn(kd[i0 + h], v_new[h]) for h in heads]
            o = [_rms(o[h], gnorm) * _silu(gate[rows_of(c), h * 64:(h + 1) * 64]) for h in heads]
            out_rows.append(jnp.concatenate(o, axis=1))
    for h in heads:
        state_s[h] = S[h]
        sout_ref[0, h] = S[h]
    o_ref[0] = jnp.concatenate(out_rows, axis=0)


def _gdn(zg, small, conv_state8, s0, cw, par, gnorm, *, tl, chunk, l_real):
    b, l, _ = zg.shape
    nt = l // tl
    kern = functools.partial(_gdn_kernel, tl=tl, chunk=chunk, l_real=l_real)
    return pl.pallas_call(
        kern,
        grid=(b, nt),
        in_specs=[pl.BlockSpec((1, tl, 1024), lambda i, t: (i, t, 0)),
                  pl.BlockSpec((1, tl, LANES), lambda i, t: (i, t, 0)),
                  pl.BlockSpec((1, SUBLANES, 768), lambda i, t: (i, 0, 0)),
                  pl.BlockSpec((1, N_HEADS, 64, 64), lambda i, t: (i, 0, 0, 0)),
                  pl.BlockSpec((SHORT_CONV, 768), lambda i, t: (0, 0)),
                  pl.BlockSpec((2, LANES), lambda i, t: (0, 0)),
                  pl.BlockSpec((1, HEAD_DIM), lambda i, t: (0, 0))],
        out_specs=[pl.BlockSpec((1, tl, GROUP_WIDTH), lambda i, t: (i, t, 0)),
                   pl.BlockSpec((1, 3, 768), lambda i, t: (i, 0, 0)),
                   pl.BlockSpec((1, N_HEADS, 64, 64), lambda i, t: (i, 0, 0, 0))],
        out_shape=[jax.ShapeDtypeStruct((b, l, GROUP_WIDTH), F32),
                   jax.ShapeDtypeStruct((b, 3, 768), F32),
                   jax.ShapeDtypeStruct((b, N_HEADS, 64, 64), F32)],
        scratch_shapes=[pltpu.VMEM((SUBLANES, 768), F32), pltpu.VMEM((N_HEADS, 64, 64), F32)],
        compiler_params=_params("parallel", "arbitrary"),
    )(zg, small, conv_state8, s0, cw, par, gnorm)


def _lru_kernel(zl_ref, cst_ref, h0_ref, cw_ref, vec_ref, wa_ref, wx_ref,
                y_ref, cout_ref, hout_ref, carry_s, h_s, *, tl, l_real):
    t = pl.program_id(1)

    @pl.when(t == 0)
    def _():
        carry_s[...] = cst_ref[0]
        h_s[...] = h0_ref[0]

    z = zl_ref[0]
    gate = z[:, 256:512]
    vec = vec_ref[...]
    xc, ext = _short_conv(z[:, 0:256], carry_s, cw_ref[...], tl)
    cout_ref[0] = ext[SUBLANES + l_real - 3:SUBLANES + l_real]
    xc = xc + vec[0:1]
    r = _sigmoid(_bdot(xc, wa_ref[...]) + vec[1:2])
    i = _sigmoid(_bdot(xc, wx_ref[...]) + vec[2:3])
    log_a = -LRU_C * r * _softplus(-vec[3:4])
    a = jnp.exp(log_a)
    u = jnp.sqrt(1.0 - a * a) * (i * xc)
    rowid = lax.broadcasted_iota(jnp.int32, u.shape, 0)
    u = u + jnp.where(rowid == 0, a * h_s[...], 0.0)
    s = 1
    while s < tl:
        a_sh = jnp.concatenate([jnp.ones((s, GROUP_WIDTH), F32), a[:tl - s]], axis=0)
        u_sh = jnp.concatenate([jnp.zeros((s, GROUP_WIDTH), F32), u[:tl - s]], axis=0)
        u = a * u_sh + u
        a = a * a_sh
        s *= 2
    h_last = u[l_real - 1:l_real]
    h_s[...] = h_last
    hout_ref[0] = h_last
    y_ref[0] = u * _gelu_tanh(gate)


def _lru(zl, conv_state8, h0, cw, vec, wa, wx, *, tl, l_real):
    b, l, _ = zl.shape
    nt = l // tl
    kern = functools.partial(_lru_kernel, tl=tl, l_real=l_real)
    return pl.pallas_call(
        kern,
        grid=(b, nt),
        in_specs=[pl.BlockSpec((1, tl, 512), lambda i, t: (i, t, 0)),
                  pl.BlockSpec((1, SUBLANES, GROUP_WIDTH), lambda i, t: (i, 0, 0)),
                  pl.BlockSpec((1, 1, GROUP_WIDTH), lambda i, t: (i, 0, 0)),
                  pl.BlockSpec((SHORT_CONV, GROUP_WIDTH), lambda i, t: (0, 0)),
                  pl.BlockSpec((4, GROUP_WIDTH), lambda i, t: (0, 0)),
                  pl.BlockSpec((GROUP_WIDTH, GROUP_WIDTH), lambda i, t: (0, 0)),
                  pl.BlockSpec((GROUP_WIDTH, GROUP_WIDTH), lambda i, t: (0, 0))],
        out_specs=[pl.BlockSpec((1, tl, GROUP_WIDTH), lambda i, t: (i, t, 0)),
                   pl.BlockSpec((1, 3, GROUP_WIDTH), lambda i, t: (i, 0, 0)),
                   pl.BlockSpec((1, 1, GROUP_WIDTH), lambda i, t: (i, 0, 0))],
        out_shape=[jax.ShapeDtypeStruct((b, l, GROUP_WIDTH), F32),
                   jax.ShapeDtypeStruct((b, 3, GROUP_WIDTH), F32),
                   jax.ShapeDtypeStruct((b, 1, GROUP_WIDTH), F32)],
        scratch_shapes=[pltpu.VMEM((SUBLANES, GROUP_WIDTH), F32), pltpu.VMEM((1, GROUP_WIDTH), F32)],
        compiler_params=_params("parallel", "arbitrary"),
    )(zl, conv_state8, h0, cw, vec, wa, wx)


def _compress_project(x_kv, pe2, w_kv):
    rows = x_kv.shape[0]
    first = lax.broadcasted_iota(jnp.int32, (2 * SUBLANES, x_kv.shape[1]), 0) < SUBLANES
    p = _bdot(jnp.concatenate([x_kv, jnp.where(first, pe2[0:1], pe2[1:2])], axis=0), w_kv)
    return (p[:rows, 0:128] + p[rows:rows + 1, 0:128],
            p[:rows, 128:256] + p[rows + SUBLANES:rows + SUBLANES + 1, 128:256])


def _compress_finish(pa, pb, w2):
    pb_next = jnp.concatenate([pb[1:], jnp.zeros((1, pb.shape[1]), F32)], axis=0)
    return _bdot(_silu(pa + pb_next), w2)


def _cmp_prompt_kernel(x_ref, pe_ref, w_ref, w2_ref, o_ref):
    x = x_ref[0]
    halves = []
    for kv in range(2):
        x_kv = jnp.concatenate([x[:, t * 256 + kv * 128:t * 256 + (kv + 1) * 128] for t in range(CMP_STRIDE)], axis=1)
        halves.append(_compress_project(x_kv, pe_ref[kv], w_ref[kv]))
    o_ref[0] = _compress_finish(jnp.concatenate([halves[0][0], halves[1][0]], axis=1),
                                jnp.concatenate([halves[0][1], halves[1][1]], axis=1), w2_ref[...])


def _cmp_prompt(xc, pe, w, w2):
    b, m, f = xc.shape
    full = lambda a: pl.BlockSpec(a.shape, lambda i: (0,) * a.ndim)
    return pl.pallas_call(
        _cmp_prompt_kernel,
        grid=(b,),
        in_specs=[pl.BlockSpec((1, m, f), lambda i: (i, 0, 0)), full(pe), full(w), full(w2)],
        out_specs=pl.BlockSpec((1, m, 256), lambda i: (i, 0, 0)),
        out_shape=jax.ShapeDtypeStruct((b, m, 256), F32),
        compiler_params=_params("parallel"),
    )(xc, pe, w, w2)


def _overlap_matrix(m, nsp, nc, ns, transposed=False):
    shape = (nsp, m) if transposed else (m, nsp)
    nn = lax.broadcasted_iota(jnp.int32, shape, 1 if transposed else 0)
    mm = lax.broadcasted_iota(jnp.int32, shape, 0 if transposed else 1)
    ov = (jnp.minimum(nn * CMP_STRIDE + CMP_LEN - 1, mm * SLC_BLOCK + SLC_BLOCK - 1)
          - jnp.maximum(nn * CMP_STRIDE, mm * SLC_BLOCK) + 1)
    return jnp.where((nn < nc) & (mm < ns), jnp.maximum(ov, 0).astype(F32) * (1.0 / CMP_LEN), 0.0)


def _select_bias(imp, qpos, ns):
    rows, nsp = imp.shape
    j = lax.broadcasted_iota(jnp.int32, (rows, nsp), 1)
    cur = qpos // SLC_BLOCK
    valid = (j <= cur) & (j < ns)
    forced = valid & ((j == 0) | (j > cur - N_LOCAL))
    score = jnp.where(valid, imp + jnp.where(forced, FORCE_BONUS, 0.0), NEG)
    score = jnp.where(j < ns, score, -3.0e38)
    cnt = jnp.zeros((rows, nsp), jnp.int32)
    for i in range(ns):
        si = score[:, i:i + 1]
        beats = (si > score) | ((si == score) & (i < j))
        cnt = cnt + beats.astype(jnp.int32)
    return jnp.where((cnt < min(SLC_TOPK, ns)) & valid, 0.0, NEG)


NSA_KEY_CHUNK = 512


NSA_STRIP = 16
NSA_STRIP_UNROLL = 32
SEL_LANES = 64
DEN_ROWS = 16


def _select_bias_t(imp_t, qpos_row, ns):
    nsr, tq = imp_t.shape
    j = lax.broadcasted_iota(jnp.int32, (nsr, tq), 0)
    cur = qpos_row // SLC_BLOCK
    valid = (j <= cur) & (j < ns)
    forced = valid & ((j == 0) | (j > cur - N_LOCAL))
    score = jnp.where(valid, imp_t + jnp.where(forced, FORCE_BONUS, 0.0), NEG)
    score = jnp.where(j < ns, score, -3.0e38)
    cnt = jnp.zeros((nsr, tq), jnp.int32)
    for i in range(ns):
        si = score[i:i + 1, :]
        beats = (si > score) | ((si == score) & (i < j))
        cnt = cnt + beats.astype(jnp.int32)
    return jnp.where((cnt < min(SLC_TOPK, ns)) & valid, 0.0, NEG)


def _nsa_prompt_kernel(q_ref, sm_ref, kc_ref, slc_ref, win_ref, e_ref, o_ref,
                       krhs_s, vslc_s, kwin_s, vwin_s, s_s, e_s, *, tq, l, nc, ns, nsr, wk, kchunk):
    t = pl.program_id(1)
    q0 = t * tq

    @pl.when(t == 0)
    def _():
        ones = jnp.ones((DEN_ROWS, l), BF16)
        for g in range(NSA_KV_HEADS):
            krow = slice(g * 64, (g + 1) * 64)
            vrow = slice(128 + g * 64, 128 + (g + 1) * 64)
            krhs_s[g, 0:64, :] = slc_ref[0, krow, :].astype(BF16)
            krhs_s[g, 64:64 + SEL_LANES, :] = e_ref[...]
            vslc_s[g, 0:64, :] = slc_ref[0, vrow, :].astype(BF16)
            vslc_s[g, 64:64 + DEN_ROWS, :] = ones
            kwin_s[g] = win_ref[0, krow, :].astype(BF16)
            vwin_s[g, 0:64, :] = win_ref[0, vrow, :].astype(BF16)
            vwin_s[g, 64:64 + DEN_ROWS, :] = ones

    q = q_ref[0] * SCALE
    gates = _sigmoid(sm_ref[0])
    kcvc = kc_ref[0]
    m = kcvc.shape[0]
    qpos = q0 + lax.broadcasted_iota(jnp.int32, (tq, 1), 0)
    qpos2 = jnp.concatenate([qpos, qpos], axis=0)
    qpos_row = q0 + lax.broadcasted_iota(jnp.int32, (1, tq), 1)
    n_id = lax.broadcasted_iota(jnp.int32, (2 * tq, m), 1)
    cmask = (n_id * CMP_STRIDE + CMP_LEN - 1 <= qpos2) & (n_id < nc)
    ovl_t = _overlap_matrix(m, nsr, nc, ns, transposed=True)
    start = pl.multiple_of(jnp.clip(q0 + tq - wk, 0, l - wk), LANES)
    dwin = qpos - (start + lax.broadcasted_iota(jnp.int32, (tq, wk), 1))
    wbias = jnp.where((dwin >= 0) & (dwin < WINDOW), 0.0, NEG)
    tri = jnp.where(lax.broadcasted_iota(jnp.int32, (tq, tq), 1) > lax.broadcasted_iota(jnp.int32, (tq, tq), 0),
                    NEG, 0.0)
    n_need = (q0 + tq + kchunk - 1) // kchunk

    def softmax_pv(nk, v_aug):
        def strip(i, carry):
            r = pl.multiple_of(i * NSA_STRIP, NSA_STRIP)
            s = s_s[pl.ds(r, NSA_STRIP), 0:nk]
            e_s[pl.ds(r, NSA_STRIP), 0:nk] = jnp.exp(s - jnp.max(s, axis=-1, keepdims=True)).astype(BF16)
            return carry

        lax.fori_loop(0, 2 * tq // NSA_STRIP, strip, 0, unroll=NSA_STRIP_UNROLL)
        oa = lax.dot_general(e_s[:, 0:nk], v_aug, (((1,), (1,)), ((), ())), preferred_element_type=F32)
        return oa[:, 0:64] * (1.0 / oa[:, 64:65])

    for g in range(NSA_KV_HEADS):
        h0 = g * NSA_GROUP
        q_heads = [q[:, (h0 + hg) * 64:(h0 + hg + 1) * 64] for hg in range(NSA_GROUP)]
        qs = jnp.concatenate(q_heads, axis=0)
        pc = _masked_softmax(_bdot_nt(qs, kcvc[:, g * 64:(g + 1) * 64]), cmask)
        o_cmp = _bdot(pc, kcvc[:, 128 + g * 64:128 + (g + 1) * 64])
        imp_t = lax.dot_general(ovl_t, pc[:tq] + pc[tq:], (((1,), (1,)), ((), ())), preferred_element_type=F32,
                                precision=lax.Precision.HIGHEST)
        bias_t = _select_bias_t(imp_t, qpos_row, ns)
        bias = jnp.concatenate([bias_t, jnp.zeros((LANES - nsr, tq), F32)], axis=0).T[:, 0:SEL_LANES]
        lhs = jnp.concatenate([jnp.concatenate([qh, bias], axis=1) for qh in q_heads], axis=0).astype(BF16)

        s3 = jnp.dot(qs.astype(BF16), kwin_s[g, :, pl.ds(start, wk)], preferred_element_type=F32)
        s_s[0:tq, 0:wk] = s3[0:tq] + wbias
        s_s[tq:2 * tq, 0:wk] = s3[tq:2 * tq] + wbias
        o_win = softmax_pv(wk, vwin_s[g, :, pl.ds(start, wk)])
        part = [gates[:, 8 + 3 * (h0 + hg):9 + 3 * (h0 + hg)] * o_cmp[hg * tq:(hg + 1) * tq]
                + gates[:, 10 + 3 * (h0 + hg):11 + 3 * (h0 + hg)] * o_win[hg * tq:(hg + 1) * tq]
                for hg in range(NSA_GROUP)]
        for nkc in range(1, l // kchunk + 1):

            @pl.when(n_need == nkc)
            def _(nk=nkc * kchunk, g=g, h0=h0, lhs=lhs, part=part):
                s_s[:, 0:nk] = jnp.dot(lhs, krhs_s[g, :, 0:nk], preferred_element_type=F32)
                s_s[0:tq, pl.ds(pl.multiple_of(q0, LANES), tq)] += tri
                s_s[tq:2 * tq, pl.ds(pl.multiple_of(q0, LANES), tq)] += tri
                o_slc = softmax_pv(nk, vslc_s[g, :, 0:nk])
                for hg in range(NSA_GROUP):
                    head = h0 + hg
                    c1 = 9 + 3 * head
                    o_ref[0, :, head * 64:(head + 1) * 64] = (part[hg]
                                                              + gates[:, c1:c1 + 1] * o_slc[hg * tq:(hg + 1) * tq])


def _block_expander(nsp, nkeys):
    blk = lax.broadcasted_iota(jnp.int32, (nsp, nkeys), 0)
    key = lax.broadcasted_iota(jnp.int32, (nsp, nkeys), 1)
    return jnp.where(key // SLC_BLOCK == blk, 1.0, 0.0).astype(BF16)


def _nsa_prompt(q, small, kcvc, slc, win, *, tq, layer):
    b, l, _ = q.shape
    m = kcvc.shape[1]
    nc = m - 1
    ns = -(-l // SLC_BLOCK)
    nsr = -(-ns // SUBLANES) * SUBLANES
    assert nsr <= SEL_LANES
    wk = min(WINDOW + tq, l)
    kern = functools.partial(_nsa_prompt_kernel, tq=tq, l=l, nc=nc, ns=ns, nsr=nsr, wk=wk, kchunk=min(NSA_KEY_CHUNK, l))
    seq = lambda rows: pl.BlockSpec((None, 1, rows, l), lambda i, t: (layer, i, 0, 0))
    tile = lambda wd: pl.BlockSpec((1, tq, wd), lambda i, t: (i, t, 0))
    return pl.pallas_call(
        kern,
        grid=(b, l // tq),
        in_specs=[tile(256), tile(LANES), pl.BlockSpec((1, m, 256), lambda i, t: (i, 0, 0)), seq(256), seq(256),
                  pl.BlockSpec((SEL_LANES, l), lambda i, t: (0, 0))],
        out_specs=tile(256),
        out_shape=jax.ShapeDtypeStruct((b, l, 256), F32),
        scratch_shapes=[pltpu.VMEM((NSA_KV_HEADS, 64 + SEL_LANES, l), BF16),
                        pltpu.VMEM((NSA_KV_HEADS, 64 + DEN_ROWS, l), BF16),
                        pltpu.VMEM((NSA_KV_HEADS, 64, l), BF16),
                        pltpu.VMEM((NSA_KV_HEADS, 64 + DEN_ROWS, l), BF16),
                        pltpu.VMEM((2 * tq, l), F32),
                        pltpu.VMEM((2 * tq, l), BF16)],
        compiler_params=_params("parallel", "arbitrary"),
    )(q, small, kcvc, slc, win, _block_expander(SEL_LANES, l))


def _later_keys_matrix(n):
    a = lax.broadcasted_iota(jnp.int32, (n, n), 0)
    c = lax.broadcasted_iota(jnp.int32, (n, n), 1)
    return jnp.where(a > c, 1.0, 0.0).astype(BF16)


SB_STRIP = 32


def _sb_prompt_kernel(q_ref, kv_ref, later2_ref, o_ref, acc_s, aft_s, z_s, lf_s, hl_s, loc_s, a_s, *, tq):
    qi = pl.program_id(1)
    q = q_ref[0] * SCALE
    acc_s[...] = jnp.zeros_like(acc_s)
    aft_s[...] = jnp.zeros_like(aft_s)
    strips = [slice(r, r + SB_STRIP) for r in range(0, tq, SB_STRIP)]

    def block(k0, diagonal):
        for h in range(N_HEADS):
            kt = kv_ref[0, h * 64:(h + 1) * 64, pl.ds(k0, tq)]
            z_s[h] = _bdot(q[:, h * 64:(h + 1) * 64], kt)
        for h in range(N_HEADS):
            for rows in strips:
                z = z_s[h, rows, :]
                lf = -(jnp.maximum(z, 0.0) + jnp.log(1.0 + jnp.exp(-jnp.abs(z))))
                if diagonal:
                    keep = (lax.broadcasted_iota(jnp.int32, (SB_STRIP, tq), 1)
                            < rows.start + lax.broadcasted_iota(jnp.int32, (SB_STRIP, tq), 0))
                    lf = jnp.where(keep, lf, 0.0)
                hi = lf.astype(BF16)
                lf_s[h, rows, :] = lf
                hl_s[h, rows, 0:tq] = hi
                hl_s[h, rows, tq:2 * tq] = (lf - hi.astype(F32)).astype(BF16)
        for h in range(N_HEADS):
            loc_s[h] = jnp.dot(hl_s[h], later2_ref[...], preferred_element_type=F32)
        for h in range(N_HEADS):
            for rows in strips:
                a = jnp.exp(z_s[h, rows, :] + lf_s[h, rows, :] + (loc_s[h, rows, :] + aft_s[h, rows, :]))
                if diagonal:
                    keep = (lax.broadcasted_iota(jnp.int32, (SB_STRIP, tq), 1)
                            < rows.start + lax.broadcasted_iota(jnp.int32, (SB_STRIP, tq), 0))
                    a = jnp.where(keep, a, 0.0)
                a_s[h, rows, :] = a.astype(BF16)
        for h in range(N_HEADS):
            vt = kv_ref[0, 256 + h * 64:256 + (h + 1) * 64, pl.ds(k0, tq)]
            acc_s[:, h * 64:(h + 1) * 64] += _bdot_nt(a_s[h], vt)
            aft_s[h] = aft_s[h] + loc_s[h, :, 0:1] + lf_s[h, :, 0:1]

    block(pl.multiple_of(qi * tq, tq), True)

    def body(it, carry):
        block(pl.multiple_of((qi - 1 - it) * tq, tq), False)
        return carry

    lax.fori_loop(0, qi, body, 0)
    o_ref[0] = acc_s[...]


def _sb_prompt(q, kv, *, tq, layer):
    b, l, _ = q.shape
    kern = functools.partial(_sb_prompt_kernel, tq=tq)
    return pl.pallas_call(
        kern,
        grid=(b, l // tq),
        in_specs=[pl.BlockSpec((1, tq, 256), lambda i, t: (i, t, 0)),
                  pl.BlockSpec((None, 1, 512, l), lambda i, t: (layer, i, 0, 0)),
                  pl.BlockSpec((2 * tq, tq), lambda i, t: (0, 0))],
        out_specs=pl.BlockSpec((1, tq, 256), lambda i, t: (i, t, 0)),
        out_shape=jax.ShapeDtypeStruct((b, l, 256), F32),
        scratch_shapes=[pltpu.VMEM((tq, 256), F32), pltpu.VMEM((N_HEADS, tq, 1), F32),
                        pltpu.VMEM((N_HEADS, tq, tq), F32), pltpu.VMEM((N_HEADS, tq, tq), F32),
                        pltpu.VMEM((N_HEADS, tq, 2 * tq), BF16), pltpu.VMEM((N_HEADS, tq, tq), F32),
                        pltpu.VMEM((N_HEADS, tq, tq), BF16)],
        compiler_params=_params("parallel", "arbitrary"),
    )(q, kv, jnp.concatenate([_later_keys_matrix(tq)] * 2, axis=0))


SB_PAGE_GROUP = 8


def _sb_sample_kernel(pt_ref, qbd_ref, new_ref, later_ref, *rest, pg, n_pg, pos0):
    pages = rest[:pg]
    o_ref, acc_s, aft_s = rest[pg:]
    g = pl.program_id(1)
    qbd = (qbd_ref[0] * SCALE).astype(BF16)
    later = later_ref[...]
    rows = qbd.shape[0]

    def blocks(kv_refs, mask):
        n = len(kv_refs)
        z = jnp.concatenate([jnp.dot(qbd, r[0, 0:256, :].astype(BF16), preferred_element_type=F32)
                             for r in kv_refs], axis=0)
        lf = -(jnp.maximum(z, 0.0) + jnp.log(1.0 + jnp.exp(-jnp.abs(z))))
        if mask is not None:
            lf = jnp.where(mask, lf, 0.0)
        loc = _split_dot(lf, later)
        aft = aft_s[...]
        acc = acc_s[...]
        for i, r in enumerate(kv_refs):
            blk = slice(i * rows, (i + 1) * rows)
            a = jnp.exp(z[blk] + lf[blk] + (loc[blk] + aft))
            if mask is not None:
                a = jnp.where(mask, a, 0.0)
            acc = acc + _bdot_nt(a, r[0, 256:512, :])
            aft = aft + (loc[blk, 0:1] + lf[blk, 0:1])
        acc_s[...] = acc
        aft_s[...] = aft

    @pl.when(g == 0)
    def _():
        acc_s[...] = jnp.zeros_like(acc_s)
        aft_s[...] = jnp.zeros_like(aft_s)
        t = lax.broadcasted_iota(jnp.int32, (rows, PAGE_SIZE), 0) % SUBLANES
        key = lax.broadcasted_iota(jnp.int32, (rows, PAGE_SIZE), 1)
        blocks([new_ref], key < t)

    for i0 in range(0, pg, SB_PAGE_GROUP):
        blocks(pages[i0:i0 + SB_PAGE_GROUP], None)

    @pl.when(g == n_pg - 1)
    def _():
        acc = acc_s[...]
        lane_head = lax.broadcasted_iota(jnp.int32, (SUBLANES, 256), 1) // HEAD_DIM
        out = jnp.zeros((SUBLANES, 256), F32)
        for h in range(N_HEADS):
            out = out + jnp.where(lane_head == h, acc[h * SUBLANES:(h + 1) * SUBLANES], 0.0)
        o_ref[0] = out


def _sb_sample(page_table, qbd, new_rows, cache, layer_base, *, pg, pos0):
    b, n_pages = page_table.shape
    n_pg = n_pages // pg
    kern = functools.partial(_sb_sample_kernel, pg=pg, n_pg=n_pg, pos0=pos0)

    def page_spec(i):
        return pl.BlockSpec((1, 512, PAGE_SIZE),
                            lambda bi, g, pt: (layer_base + pt[bi, (n_pg - 1 - g) * pg + (pg - 1 - i)], 0, 0))

    grid_spec = pltpu.PrefetchScalarGridSpec(
        num_scalar_prefetch=1,
        grid=(b, n_pg),
        in_specs=[pl.BlockSpec((1, 32, 256), lambda bi, g, pt: (bi, 0, 0)),
                  pl.BlockSpec((1, 512, PAGE_SIZE), lambda bi, g, pt: (bi, 0, 0)),
                  pl.BlockSpec((PAGE_SIZE, PAGE_SIZE), lambda bi, g, pt: (0, 0))]
                 + [page_spec(i) for i in range(pg)],
        out_specs=pl.BlockSpec((1, SUBLANES, 256), lambda bi, g, pt: (bi, 0, 0)),
        scratch_shapes=[pltpu.VMEM((32, 256), F32), pltpu.VMEM((32, 1), F32)],
    )
    return pl.pallas_call(
        kern,
        grid_spec=grid_spec,
        out_shape=jax.ShapeDtypeStruct((b, SUBLANES, 256), F32),
        compiler_params=_params("parallel", "arbitrary"),
    )(page_table, qbd, new_rows, _later_keys_matrix(PAGE_SIZE), *([cache] * pg))


CMP_PAGE_GROUP = 16


def _nsa_sample_cmp_kernel(pt_ref, q_ref, pe_ref, w_ref, w2_ref, *rest, n_pages, pos0, nsp):
    pages = rest[:n_pages]
    ocmp_ref, sel_ref, xs_ref = rest[n_pages:]
    pas, pbs = [], []
    for c in range(0, n_pages, CMP_PAGE_GROUP):
        npg = min(CMP_PAGE_GROUP, n_pages - c)
        for i in range(npg):
            for kv in range(2):
                xs_ref[kv, i * PAGE_SIZE:(i + 1) * PAGE_SIZE, :] = pages[c + i][0, kv * LANES:(kv + 1) * LANES, :].T
        rows = npg * (PAGE_SIZE // CMP_STRIDE)
        halves = []
        for kv in range(2):
            x_kv = jnp.concatenate([xs_ref[kv, pl.ds(t, rows, stride=CMP_STRIDE), :] for t in range(CMP_STRIDE)], axis=1)
            halves.append(_compress_project(x_kv, pe_ref[kv], w_ref[kv]))
        pas.append(jnp.concatenate([halves[0][0], halves[1][0]], axis=1))
        pbs.append(jnp.concatenate([halves[0][1], halves[1][1]], axis=1))
    kcvc = _compress_finish(jnp.concatenate(pas, axis=0), jnp.concatenate(pbs, axis=0), w2_ref[...])
    m = kcvc.shape[0]
    nc = m - 1
    ns = 2 * n_pages + 1
    q = q_ref[0]
    tq = q.shape[0]
    qpos = pos0 + lax.broadcasted_iota(jnp.int32, (tq, 1), 0)
    qpos2 = jnp.concatenate([qpos, qpos], axis=0)
    n_id = lax.broadcasted_iota(jnp.int32, (2 * tq, m), 1)
    cmask = (n_id * CMP_STRIDE + CMP_LEN - 1 <= qpos2) & (n_id < nc)
    ovl = _overlap_matrix(m, nsp, nc, ns)
    for g in range(NSA_KV_HEADS):
        h0 = g * NSA_GROUP
        qs = jnp.concatenate([q[:, h0 * 64:(h0 + 1) * 64], q[:, (h0 + 1) * 64:(h0 + 2) * 64]], axis=0)
        pc = _masked_softmax(_bdot_nt(qs, kcvc[:, g * 64:(g + 1) * 64]) * SCALE, cmask)
        o_cmp = _bdot(pc, kcvc[:, 128 + g * 64:128 + (g + 1) * 64])
        sel_ref[0, g] = _select_bias(_fdot(pc[:tq] + pc[tq:], ovl), qpos, ns)
        for hg in range(NSA_GROUP):
            ocmp_ref[0, :, (h0 + hg) * 64:(h0 + hg + 1) * 64] = o_cmp[hg * tq:(hg + 1) * tq]


def _nsa_sample_cmp(page_table, q, pe, w, w2, cache, layer_base, *, pos0):
    b, n_pages = page_table.shape
    nsp = -(-(2 * n_pages + 1) // LANES) * LANES
    kern = functools.partial(_nsa_sample_cmp_kernel, n_pages=n_pages, pos0=pos0, nsp=nsp)
    full = lambda a: pl.BlockSpec(a.shape, lambda bi, pt: (0,) * a.ndim)

    def page_spec(i):
        return pl.BlockSpec((1, 256, PAGE_SIZE), lambda bi, pt: (layer_base + pt[bi, i], 0, 0))

    grid_spec = pltpu.PrefetchScalarGridSpec(
        num_scalar_prefetch=1,
        grid=(b,),
        in_specs=[pl.BlockSpec((1, SUBLANES, 256), lambda bi, pt: (bi, 0, 0)),
                  full(pe), full(w), full(w2)] + [page_spec(i) for i in range(n_pages)],
        out_specs=[pl.BlockSpec((1, SUBLANES, 256), lambda bi, pt: (bi, 0, 0)),
                   pl.BlockSpec((1, NSA_KV_HEADS, SUBLANES, nsp), lambda bi, pt: (bi, 0, 0, 0))],
        scratch_shapes=[pltpu.VMEM((2, CMP_PAGE_GROUP * PAGE_SIZE, LANES), F32)],
    )
    return pl.pallas_call(
        kern,
        grid_spec=grid_spec,
        out_shape=[jax.ShapeDtypeStruct((b, SUBLANES, 256), F32),
                   jax.ShapeDtypeStruct((b, NSA_KV_HEADS, SUBLANES, nsp), F32)],
        compiler_params=_params("parallel"),
    )(page_table, q, pe, w, w2, *([cache] * n_pages))


def _softmax_rows(s):
    e = jnp.exp(s - jnp.max(s, axis=-1, keepdims=True))
    return e * (1.0 / jnp.sum(e, axis=-1, keepdims=True))


def _nsa_sample_slc_kernel(pt_ref, q_ref, sm_ref, ocmp_ref, sel_ref, newslc_ref, pastwin_ref, newwin_ref,
                           *rest, n_pages, pos0):
    pages = rest[:n_pages]
    o_ref, z_s = rest[n_pages:]
    q = q_ref[0] * SCALE
    tq = q.shape[0]
    gates = _sigmoid(sm_ref[0])
    nk = (n_pages + 1) * PAGE_SIZE
    past = pastwin_ref.shape[2]
    t2 = lax.broadcasted_iota(jnp.int32, (2 * tq, 1), 0) % tq
    kposw = jnp.concatenate([lax.broadcasted_iota(jnp.int32, (2 * tq, past), 1) - past,
                             lax.broadcasted_iota(jnp.int32, (2 * tq, PAGE_SIZE), 1)], axis=1)
    dwin = t2 - kposw
    wbias = jnp.where((dwin >= 0) & (dwin < WINDOW), 0.0, NEG)
    lane = lax.broadcasted_iota(jnp.int32, (2 * tq, PAGE_SIZE), 1)
    lower_block = lane < SLC_BLOCK
    new_causal = jnp.where(lane <= t2, 0.0, NEG)
    for g in range(NSA_KV_HEADS):
        h0 = g * NSA_GROUP
        krow = slice(g * 64, (g + 1) * 64)
        vrow = slice(128 + g * 64, 128 + (g + 1) * 64)
        qs = jnp.concatenate([q[:, h0 * 64:(h0 + 1) * 64], q[:, (h0 + 1) * 64:(h0 + 2) * 64]], axis=0).astype(BF16)
        bias = jnp.concatenate([sel_ref[0, g], sel_ref[0, g]], axis=0)
        dot = lambda k_t: jnp.dot(qs, k_t.astype(BF16), preferred_element_type=F32)
        for p in range(n_pages):
            z_s[:, p * PAGE_SIZE:(p + 1) * PAGE_SIZE] = dot(pages[p][0, krow, :]) + jnp.where(
                lower_block, bias[:, 2 * p:2 * p + 1], bias[:, 2 * p + 1:2 * p + 2])
        z_s[:, n_pages * PAGE_SIZE:nk] = dot(newslc_ref[0, krow, :]) + bias[:, 2 * n_pages:2 * n_pages + 1] + new_causal
        p2 = _softmax_rows(z_s[...])
        o_slc = _bdot_nt(p2[:, n_pages * PAGE_SIZE:nk], newslc_ref[0, vrow, :])
        for p in range(n_pages):
            o_slc = o_slc + _bdot_nt(p2[:, p * PAGE_SIZE:(p + 1) * PAGE_SIZE], pages[p][0, vrow, :])
        p3 = _softmax_rows(jnp.concatenate([dot(pastwin_ref[0, krow, :]), dot(newwin_ref[0, krow, :])], axis=1) + wbias)
        o_win = _bdot_nt(p3[:, 0:past], pastwin_ref[0, vrow, :]) + _bdot_nt(p3[:, past:], newwin_ref[0, vrow, :])
        for hg in range(NSA_GROUP):
            head = h0 + hg
            rows = slice(hg * tq, (hg + 1) * tq)
            c0 = 8 + 3 * head
            o_ref[0, :, head * 64:(head + 1) * 64] = (gates[:, c0:c0 + 1] * ocmp_ref[0, :, head * 64:(head + 1) * 64]
                                                      + gates[:, c0 + 1:c0 + 2] * o_slc[rows]
                                                      + gates[:, c0 + 2:c0 + 3] * o_win[rows])


def _nsa_sample_slc(page_table, q, small, o_cmp, sel, new_slc, past_win, past_win_base, new_win, cache, layer_base,
                    *, pos0):
    b, n_pages = page_table.shape
    nsp = sel.shape[-1]
    nk = (n_pages + 1) * PAGE_SIZE
    past = past_win.shape[2]
    kern = functools.partial(_nsa_sample_slc_kernel, n_pages=n_pages, pos0=pos0)
    per_b = lambda r, wd: pl.BlockSpec((1, r, wd), lambda bi, pt: (bi, 0, 0))

    def page_spec(i):
        return pl.BlockSpec((1, 256, PAGE_SIZE), lambda bi, pt: (layer_base + pt[bi, i], 0, 0))

    grid_spec = pltpu.PrefetchScalarGridSpec(
        num_scalar_prefetch=1,
        grid=(b,),
        in_specs=[per_b(SUBLANES, 256), per_b(SUBLANES, LANES), per_b(SUBLANES, 256),
                  pl.BlockSpec((1, NSA_KV_HEADS, SUBLANES, nsp), lambda bi, pt: (bi, 0, 0, 0)),
                  per_b(256, PAGE_SIZE),
                  pl.BlockSpec((1, 256, past), lambda bi, pt: (past_win_base + bi, 0, 0)),
                  per_b(256, PAGE_SIZE)] + [page_spec(i) for i in range(n_pages)],
        out_specs=per_b(SUBLANES, 256),
        scratch_shapes=[pltpu.VMEM((2 * SUBLANES, nk), F32)],
    )
    return pl.pallas_call(
        kern,
        grid_spec=grid_spec,
        out_shape=jax.ShapeDtypeStruct((b, SUBLANES, 256), F32),
        compiler_params=_params("parallel"),
    )(page_table, q, small, o_cmp, sel, new_slc, past_win, new_win, *([cache] * n_pages))


def _transpose_cast_kernel(x_ref, o_ref):
    o_ref[...] = x_ref[...].T.astype(BF16)


def _transpose_cast(x, tr=384):
    r, c = x.shape
    return pl.pallas_call(
        _transpose_cast_kernel,
        grid=(r // tr,),
        in_specs=[pl.BlockSpec((tr, c), lambda i: (i, 0))],
        out_specs=pl.BlockSpec((c, tr), lambda i: (0, i)),
        out_shape=jax.ShapeDtypeStruct((c, r), BF16),
        compiler_params=_params("parallel"),
    )(x)


def _prep_layer(l, P):
    w_t = P["w_in_t"][l]
    rows = lambda lo, hi: w_t[lo:hi]
    small = jnp.concatenate([rows(1024, 1032), rows(2568, 2580), jnp.zeros((LANES - 20, D_MODEL), F32)], axis=0)
    w_perm = _transpose_cast(jnp.concatenate([rows(0, 1024), rows(1032, 1544), rows(1544, 1800), rows(2580, 2836),
                                              rows(2836, 3348), rows(1800, 2568), small], axis=0))
    par = jnp.zeros((2, LANES), F32).at[0, 4:8].set(P["gdn_a_log"][l]).at[1, 4:8].set(P["gdn_dt_bias"][l])

    def block_diag(w):
        return jnp.einsum("ncd,nm->ncmd", w, jnp.eye(4, dtype=F32)).reshape(GROUP_WIDTH, GROUP_WIDTH).astype(BF16)

    eye2 = jnp.eye(2, dtype=F32)
    w1 = P["nsa_cmp_w1"][l]
    half = CMP_LEN // 2

    cmp_w = jnp.einsum("khtde,gG->ktgdhGe", w1.reshape(2, 2, half, HEAD_DIM, HEAD_DIM), eye2)
    cmp_w = cmp_w.reshape(2, half * 128, 256).astype(BF16)
    pe = P["nsa_cmp_pe"][l].reshape(2, 2, half, 1, HEAD_DIM)
    cmp_pe = jnp.broadcast_to(pe, (2, 2, half, 2, HEAD_DIM)).reshape(2, 2, half * 128)
    w2 = jnp.einsum("kef,kK,gG->kgeKGf", P["nsa_cmp_w2"][l], eye2, eye2).reshape(256, 256).astype(BF16)
    w_kv_t = jnp.concatenate([rows(2836, 3348), rows(1800, 2568)], axis=0).astype(BF16)
    return dict(
        norm1=P["norm1"][l][None], w_in=w_perm, w_kv_t=w_kv_t,
        gdn_cw=P["gdn_conv_w"][l], gdn_par=par, gdn_norm=P["gdn_norm"][l][None],
        lru_cw=P["lru_conv_w"][l],
        lru_vec=jnp.stack([P["lru_conv_b"][l], P["lru_ba"][l], P["lru_bx"][l], P["lru_lambda"][l]]),
        lru_wa=block_diag(P["lru_wa"][l]), lru_wx=block_diag(P["lru_wx"][l]),
        cmp_pe=cmp_pe, cmp_w=cmp_w, cmp_w2=w2,
        w_out=P["w_out"][l].astype(BF16), norm2=P["norm2"][l][None],
        wg=P["ffn_w_gate"][l].astype(BF16), wu=P["ffn_w_up"][l].astype(BF16),
        ffn_cw=P["ffn_conv_w"][l], wd=P["ffn_w_down"][l].astype(BF16),
    )


def _rope_tables(pos):
    half = HEAD_DIM // 2
    inv = ROPE_THETA ** (-jnp.arange(half, dtype=F32) / half)
    ang = pos.astype(F32)[:, None] * inv[None, :]
    c, s = jnp.cos(ang), jnp.sin(ang)
    return jnp.tile(jnp.concatenate([c, c], axis=-1), (1, 2)), jnp.tile(jnp.concatenate([-s, s], axis=-1), (1, 2))


PROMPT_TM = 256
PROMPT_TQ = 256
SB_TQ = 256
GDN_CHUNK = 64


def _layer_prompt(x2d, W, b, l, cos, sin, final, norm_f, layer, depth, kv_stacks):
    outs = _in_proj_t(x2d, W["norm1"], W["w_in"], W["w_kv_t"], cos, sin, b, l, PROMPT_TM, layer, depth, kv_stacks)
    zg, zl, nq, sq, cmp_r, small = [o.reshape(b, l, o.shape[-1]) for o in outs[:6]]
    kv_stacks = outs[6:]
    skv_t, _, slc_t, win_t = kv_stacks
    o_gdn, gdn_conv, gdn_s = _gdn(zg, small, jnp.zeros((b, SUBLANES, 768), F32), jnp.zeros((b, N_HEADS, 64, 64), F32),
                                  W["gdn_cw"], W["gdn_par"], W["gdn_norm"], tl=PROMPT_TM, chunk=GDN_CHUNK,
                                  l_real=PROMPT_TM)
    o_lru, lru_conv, lru_h = _lru(zl, jnp.zeros((b, SUBLANES, GROUP_WIDTH), F32), jnp.zeros((b, 1, GROUP_WIDTH), F32),
                                  W["lru_cw"], W["lru_vec"], W["lru_wa"], W["lru_wx"], tl=PROMPT_TM, l_real=PROMPT_TM)
    kcvc = _cmp_prompt(cmp_r.reshape(b, l // CMP_STRIDE, CMP_STRIDE * 256), W["cmp_pe"], W["cmp_w"], W["cmp_w2"])
    o_nsa = _nsa_prompt(nq, small, kcvc, slc_t, win_t, tq=min(PROMPT_TQ, l), layer=layer)
    o_sb = _sb_prompt(sq, skv_t, tq=min(SB_TQ, l), layer=layer)
    mixes = [o.reshape(b * l, GROUP_WIDTH) for o in (o_gdn, o_lru, o_nsa, o_sb)]
    res = _out_ffn(x2d, mixes, W["w_out"], W["norm2"], W["wg"], W["wu"], W["ffn_cw"], W["wd"],
                   jnp.zeros((SUBLANES, D_FF), F32), norm_f, tm=PROMPT_TM, stride=1, tiles_per_seq=l // PROMPT_TM,
                   final=final)
    states = (gdn_conv, gdn_s, lru_conv, lru_h[:, 0], res[1][l // PROMPT_TM - 1::l // PROMPT_TM])
    return res[0], kv_stacks, states, (res[2] if final else None)


def _prompt_kv_outputs(kv_stacks, l):
    skv_t, cmp_t, slc_t, win_t = kv_stacks
    kv6 = lambda a, h: jnp.transpose(a.reshape(a.shape[0], a.shape[1], 2, h, HEAD_DIM, a.shape[-1]), (0, 1, 5, 2, 3, 4))
    return kv6(cmp_t, 2), kv6(slc_t, 2), kv6(skv_t, 4), kv6(win_t[..., l - min(WINDOW, l):], 2)


def _layer_sample(x2d, W, layer, S, cos, sin, final, norm_f):
    page_table = S["page_table"]
    b, n_pages = page_table.shape
    t_new = x2d.shape[0] // b
    pos0 = n_pages * PAGE_SIZE
    n_phys = S["n_phys"]
    outs = _in_proj(x2d, W["norm1"], W["w_in"], cos, sin, x2d.shape[0], 1)

    def to_batch_major(a, rows=SUBLANES):
        a = a.reshape(t_new, b, a.shape[-1]).transpose(1, 0, 2)
        return jnp.pad(a, ((0, 0), (0, rows - t_new), (0, 0)))

    zg, zl, nq, sq, skv, cmp_r, slc_r, win_r, small = [to_batch_major(o) for o in outs]
    front = lambda st: jnp.pad(st, ((0, 0), (SUBLANES - st.shape[1], 0), (0, 0)))
    o_gdn, gdn_conv, gdn_s = _gdn(zg, small, front(S["gdn_conv"][layer]), S["gdn"][layer],
                                  W["gdn_cw"], W["gdn_par"], W["gdn_norm"], tl=SUBLANES, chunk=SUBLANES, l_real=t_new)
    o_lru, lru_conv, lru_h = _lru(zl, front(S["lru_conv"][layer]), S["lru"][layer][:, None, :],
                                  W["lru_cw"], W["lru_vec"], W["lru_wa"], W["lru_wx"], tl=SUBLANES, l_real=t_new)
    base = layer * n_phys
    o_cmp, sel = _nsa_sample_cmp(page_table, nq, W["cmp_pe"], W["cmp_w"], W["cmp_w2"],
                                 S["cmp_pages"], base, pos0=pos0)
    pad_page = lambda a: jnp.pad(a.transpose(0, 2, 1), ((0, 0), (0, 0), (0, PAGE_SIZE - a.shape[1])))
    o_nsa = _nsa_sample_slc(page_table, nq, small, o_cmp, sel, pad_page(slc_r), S["win"], layer * b, pad_page(win_r),
                            S["slc_pages"], base, pos0=pos0)
    head_of_col = jnp.arange(256) // HEAD_DIM
    head_of_row = jnp.arange(N_HEADS * SUBLANES) // SUBLANES
    qbd = jnp.where(head_of_row[:, None] == head_of_col[None, :], jnp.tile(sq, (1, N_HEADS, 1)), 0.0)
    o_sb = _sb_sample(page_table, qbd, pad_page(skv), S["sb_pages"], base, pg=min(32, n_pages), pos0=pos0)

    def to_time_major(a):
        return a[:, :t_new].transpose(1, 0, 2).reshape(t_new * b, a.shape[-1])

    mixes = [to_time_major(o) for o in (o_gdn, o_lru, o_nsa, o_sb)]
    buf = S["ffn_conv"][layer].transpose(1, 0, 2).reshape(2 * b, D_FF)
    res = _out_ffn(x2d, mixes, W["w_out"], W["norm2"], W["wg"], W["wu"], W["ffn_cw"], W["wd"], buf, norm_f,
                   tm=x2d.shape[0], stride=b, tiles_per_seq=1, final=final)
    kv5 = lambda a, h: a[:, :t_new].reshape(b, t_new, 2, h, HEAD_DIM)
    ffn_state = res[1].reshape(2, b, D_FF).transpose(1, 0, 2)
    states = (kv5(cmp_r, 2), kv5(slc_r, 2), kv5(skv, 4), kv5(win_r, 2), gdn_conv, gdn_s, lru_conv, lru_h[:, 0], ffn_state)
    return res[0], states, (res[2] if final else None)


def _sample_state_views(cache_cmp_kv, cache_slc_kv, cache_sb_kv, cache_win_kv):
    def view(c):
        d0, d1, tok, kv, h, d = c.shape
        return jnp.transpose(c, (0, 1, 3, 4, 5, 2)).reshape(d0 * d1, kv * h * d, tok)

    return dict(cmp_pages=view(cache_cmp_kv), slc_pages=view(cache_slc_kv), sb_pages=view(cache_sb_kv),
                win=view(cache_win_kv))


def kernel(x_prompt, x_sample, cache_cmp_kv, cache_slc_kv, cache_sb_kv, cache_win_kv, state_gdn_conv, state_gdn, state_lru_conv, state_lru, state_ffn_conv, page_table, norm1, w_in, gdn_conv_w, gdn_a_log, gdn_dt_bias, gdn_norm, lru_conv_w, lru_conv_b, lru_wa, lru_ba, lru_wx, lru_bx, lru_lambda, nsa_cmp_pe, nsa_cmp_w1, nsa_cmp_w2, w_out, norm2, ffn_w_gate, ffn_w_up, ffn_conv_w, ffn_w_down, norm_f):
    P = dict(norm1=norm1, w_in_t=jnp.transpose(w_in, (0, 2, 1)), gdn_conv_w=gdn_conv_w, gdn_a_log=gdn_a_log, gdn_dt_bias=gdn_dt_bias,
             gdn_norm=gdn_norm, lru_conv_w=lru_conv_w, lru_conv_b=lru_conv_b, lru_wa=lru_wa, lru_ba=lru_ba,
             lru_wx=lru_wx, lru_bx=lru_bx, lru_lambda=lru_lambda, nsa_cmp_pe=nsa_cmp_pe, nsa_cmp_w1=nsa_cmp_w1,
             nsa_cmp_w2=nsa_cmp_w2, w_out=w_out, norm2=norm2, ffn_w_gate=ffn_w_gate, ffn_w_up=ffn_w_up,
             ffn_conv_w=ffn_conv_w, ffn_w_down=ffn_w_down)
    depth = w_in.shape[0]
    weights = [_prep_layer(l, P) for l in range(depth)]
    gf = norm_f[None]

    b, l, _ = x_prompt.shape
    cos, sin = _rope_tables(jnp.arange(l))
    h = x_prompt.reshape(b * l, D_MODEL)
    p_states, y_prompt, kv_stacks = [], None, None
    for layer in range(depth):
        h, kv_stacks, st, y = _layer_prompt(h, weights[layer], b, l, cos, sin, layer == depth - 1, gf, layer, depth,
                                            kv_stacks)
        p_states.append(st)
        y_prompt = y
    y_prompt = y_prompt.reshape(b, l, D_MODEL)
    p_kv = _prompt_kv_outputs(kv_stacks, l)

    db, t_new, _ = x_sample.shape
    n_phys = cache_sb_kv.shape[1]
    n_pages = page_table.shape[1]
    pos0 = n_pages * PAGE_SIZE
    past_win = cache_win_kv.shape[2]
    S = _sample_state_views(cache_cmp_kv, cache_slc_kv, cache_sb_kv, cache_win_kv)
    S.update(page_table=page_table, n_phys=n_phys,
             gdn_conv=state_gdn_conv, gdn=state_gdn, lru_conv=state_lru_conv, lru=state_lru, ffn_conv=state_ffn_conv)
    cos_s, sin_s = _rope_tables(pos0 + jnp.repeat(jnp.arange(t_new), db))
    h = x_sample.transpose(1, 0, 2).reshape(t_new * db, D_MODEL)
    s_states, y_sample = [], None
    for layer in range(depth):
        h, st, y = _layer_sample(h, weights[layer], layer, S, cos_s, sin_s, layer == depth - 1, gf)
        s_states.append(st)
        y_sample = y
    y_sample = y_sample.reshape(t_new, db, D_MODEL).transpose(1, 0, 2)

    stk = lambda states, i: jnp.stack([s[i] for s in states])
    return ((y_prompt, y_sample) + p_kv + tuple(stk(p_states, i) for i in range(5))
            + tuple(stk(s_states, i) for i in range(9)))
```

```python
import functools
import math

import jax
import jax.numpy as jnp
from jax import lax
from jax.experimental import pallas as pl
from jax.experimental.pallas import tpu as pltpu

F32, BF16 = jnp.float32, jnp.bfloat16

D_MODEL = 1024
HEAD_DIM = 64
GROUP_WIDTH = 256
N_HEADS = 4
NSA_KV_HEADS = 2
NSA_GROUP = 2
D_FF = 2816
SHORT_CONV = 4
FFN_CONV = 3
LRU_C = 8.0
CMP_STRIDE = 16
CMP_LEN = 32
SLC_BLOCK = 64
SLC_TOPK = 16
N_LOCAL = 2
WINDOW = 512
FORCE_BONUS = 1.0e3
ROPE_THETA = 10000.0
EPS = 1e-6
NEG = -1e30
SCALE = HEAD_DIM ** -0.5
PAGE_SIZE = 128

VMEM_LIMIT_BYTES = 56 * 1024 * 1024
LANES = 128
SUBLANES = 8

IN_WIDTHS = (1024, 512, 256, 256, 512, 256, 256, 256, 128)
D_IN_PAD = sum(IN_WIDTHS)


def _params(*sem):
    return pltpu.CompilerParams(dimension_semantics=sem, vmem_limit_bytes=VMEM_LIMIT_BYTES)


def _bdot(a, b):
    return jnp.dot(a.astype(BF16), b.astype(BF16), preferred_element_type=F32)


def _bdot_nt(a, b):
    return lax.dot_general(a.astype(BF16), b.astype(BF16), (((1,), (1,)), ((), ())), preferred_element_type=F32)


def _bdot_tn(a, b):
    return lax.dot_general(a.astype(BF16), b.astype(BF16), (((0,), (0,)), ((), ())), preferred_element_type=F32)


def _fdot(a, b):
    return jnp.dot(a, b, preferred_element_type=F32, precision=lax.Precision.HIGHEST)


def _dot3(a, b):
    ah, bh = a.astype(BF16), b.astype(BF16)
    al, bl = (a - ah.astype(F32)).astype(BF16), (b - bh.astype(F32)).astype(BF16)
    dot = lambda x, y: jnp.dot(x, y, preferred_element_type=F32)
    return dot(ah, bh) + (dot(ah, bl) + dot(al, bh))


GDN_GROUP_CHUNKS = 2


def _split_dot(a, b_exact):
    hi = a.astype(BF16)
    lo = (a - hi.astype(F32)).astype(BF16)
    return (jnp.dot(hi, b_exact, preferred_element_type=F32) + jnp.dot(lo, b_exact, preferred_element_type=F32))


def _sigmoid(x):
    return jax.nn.sigmoid(x)


def _silu(x):
    return x * jax.nn.sigmoid(x)


def _softplus(x):
    return jnp.maximum(x, 0.0) + jnp.log1p(jnp.exp(-jnp.abs(x)))


def _gelu_tanh(x):
    return 0.5 * x * (1.0 + jnp.tanh(math.sqrt(2.0 / math.pi) * (x + 0.044715 * (x * x * x))))


def _rms(x, g):
    return x * lax.rsqrt(jnp.mean(x * x, axis=-1, keepdims=True) + EPS) * g


def _rope_pairs(v, cos, sin):
    lane = lax.broadcasted_iota(jnp.int32, v.shape, 1)
    first_half = (lane % HEAD_DIM) < (HEAD_DIM // 2)
    partner = jnp.where(first_half, pltpu.roll(v, LANES - HEAD_DIM // 2, 1), pltpu.roll(v, HEAD_DIM // 2, 1))
    return v * cos + partner * sin


def _masked_softmax(s, mask):
    sm = jnp.where(mask, s, NEG)
    m = jnp.max(sm, axis=-1, keepdims=True)
    e = jnp.where(mask, jnp.exp(sm - m), 0.0)
    den = jnp.sum(e, axis=-1, keepdims=True)
    return e * (1.0 / jnp.where(den > 0.0, den, 1.0))


def _in_proj_kernel(x_ref, g_ref, w_ref, cos_ref, sin_ref,
                    zg_ref, zl_ref, nq_ref, sq_ref, skv_ref, cmp_ref, slc_ref, win_ref, sm_ref):
    xb = _rms(x_ref[...], g_ref[...]).astype(BF16)
    cos, sin = cos_ref[...], sin_ref[...]

    def mm(lo, width):
        return jnp.dot(xb, w_ref[:, lo:lo + width], preferred_element_type=F32)

    zg_ref[...] = mm(0, 1024)
    zl_ref[...] = mm(1024, 512)
    q = mm(1536, 256)
    nq_ref[:, 0:128] = _rope_pairs(q[:, 0:128], cos, sin)
    nq_ref[:, 128:256] = _rope_pairs(q[:, 128:256], cos, sin)
    sq_ref[...] = mm(1792, 256)
    skv_ref[...] = mm(2048, 512)
    for ref, lo in ((cmp_ref, 2560), (slc_ref, 2816), (win_ref, 3072)):
        kv = mm(lo, 256)
        ref[:, 0:128] = _rope_pairs(kv[:, 0:128], cos, sin)
        ref[:, 128:256] = kv[:, 128:256]
    sm_ref[...] = mm(3328, 128)


def _in_proj(x2d, g, w, cos, sin, tm, table_tiles):
    n = x2d.shape[0]
    nt = n // tm
    row = lambda wd: pl.BlockSpec((tm, wd), lambda i: (i, 0))
    tab = pl.BlockSpec((tm, LANES), lambda i: (i % table_tiles, 0))
    return pl.pallas_call(
        _in_proj_kernel,
        grid=(nt,),
        in_specs=[row(D_MODEL), pl.BlockSpec((1, D_MODEL), lambda i: (0, 0)),
                  pl.BlockSpec((D_MODEL, D_IN_PAD), lambda i: (0, 0)), tab, tab],
        out_specs=[row(wd) for wd in IN_WIDTHS],
        out_shape=[jax.ShapeDtypeStruct((n, wd), F32) for wd in IN_WIDTHS],
        compiler_params=_params("parallel"),
    )(x2d, g, w, cos, sin)


def _rope_rows(v, cos_t, sin_t):
    q = HEAD_DIM // 2
    partner = jnp.concatenate([v[q:2 * q], v[0:q], v[3 * q:4 * q], v[2 * q:3 * q]], axis=0)
    return v * cos_t + partner * sin_t


KV_T_ROWS = 1280


N_KV_OUT = 4


def _in_proj_t_kernel(x_ref, g_ref, w_ref, wt_ref, cos_ref, sin_ref, cost_ref, sint_ref, *rest):
    (zg_ref, zl_ref, nq_ref, sq_ref, cmp_ref, sm_ref, skvt_ref, cmpt_ref, slct_ref, wint_ref) = rest[-(6 + N_KV_OUT):]
    xb = _rms(x_ref[...], g_ref[...]).astype(BF16)
    cos, sin = cos_ref[...], sin_ref[...]
    cos_t, sin_t = cost_ref[...], sint_ref[...]

    def mm(lo, width):
        return jnp.dot(xb, w_ref[:, lo:lo + width], preferred_element_type=F32)

    def mm_t(lo, rows):
        return lax.dot_general(wt_ref[lo:lo + rows, :], xb, (((1,), (1,)), ((), ())), preferred_element_type=F32)

    zg_ref[...] = mm(0, 1024)
    zl_ref[...] = mm(1024, 512)
    q = mm(1536, 256)
    nq_ref[:, 0:128] = _rope_pairs(q[:, 0:128], cos, sin)
    nq_ref[:, 128:256] = _rope_pairs(q[:, 128:256], cos, sin)
    sq_ref[...] = mm(1792, 256)
    kv = mm(2560, 256)
    cmp_ref[:, 0:128] = _rope_pairs(kv[:, 0:128], cos, sin)
    cmp_ref[:, 128:256] = kv[:, 128:256]
    sm_ref[...] = mm(3328, 128)
    skvt_ref[0] = mm_t(0, 512)
    for ref, lo in ((cmpt_ref, 512), (slct_ref, 768), (wint_ref, 1024)):
        kvt = mm_t(lo, 256)
        ref[0, 0:128, :] = _rope_rows(kvt[0:128], cos_t, sin_t)
        ref[0, 128:256, :] = kvt[128:256]


def _in_proj_t(x2d, g, w, wt, cos, sin, b, l, tm, layer, depth, kv_stacks):
    tps = l // tm
    row = lambda wd: pl.BlockSpec((tm, wd), lambda i: (i, 0))
    tab = pl.BlockSpec((tm, LANES), lambda i: (i % tps, 0))
    tab_t = pl.BlockSpec((LANES, tm), lambda i: (0, i % tps))
    kvt = lambda c: pl.BlockSpec((None, 1, c, tm), lambda i: (layer, i // tps, 0, i % tps))
    row_widths = (1024, 512, 256, 256, 256, 128)
    t_rows = (512, 256, 256, 256)
    in_specs = [row(D_MODEL), pl.BlockSpec((1, D_MODEL), lambda i: (0, 0)),
                pl.BlockSpec((D_MODEL, D_IN_PAD), lambda i: (0, 0)),
                pl.BlockSpec((KV_T_ROWS, D_MODEL), lambda i: (0, 0)), tab, tab, tab_t, tab_t]
    args = [x2d, g, w, wt, cos, sin, cos.T, sin.T]
    aliases = {}
    if kv_stacks is not None:
        aliases = {len(args) + k: len(row_widths) + k for k in range(N_KV_OUT)}
        in_specs += [pl.BlockSpec(memory_space=pl.ANY)] * N_KV_OUT
        args += list(kv_stacks)
    return pl.pallas_call(
        _in_proj_t_kernel,
        grid=(b * tps,),
        in_specs=in_specs,
        out_specs=[row(wd) for wd in row_widths] + [kvt(c) for c in t_rows],
        out_shape=[jax.ShapeDtypeStruct((b * l, wd), F32) for wd in row_widths]
                  + [jax.ShapeDtypeStruct((depth, b, c, l), F32) for c in t_rows],
        input_output_aliases=aliases,
        compiler_params=_params("parallel"),
    )(*args)


FFN_SLABS = 2


def _out_ffn_kernel(*refs, tm, tf, nf, stride, prev_rows, tiles_per_seq, final):
    x_ref, mix_refs, refs = refs[0], refs[1:5], refs[5:]
    if final:
        (wo_ref, g2_ref, wg_ref, wu_ref, cw_ref, wd_ref, buf_ref, gf_ref,
         xo_ref, st_ref, y_ref, carry_s) = refs
    else:
        (wo_ref, g2_ref, wg_ref, wu_ref, cw_ref, wd_ref, buf_ref,
         xo_ref, st_ref, carry_s) = refs
    i = pl.program_id(0)
    mix = jnp.concatenate([m[...] for m in mix_refs], axis=1)
    x1 = x_ref[...] + _bdot(mix, wo_ref[...])
    xn = _rms(x1, g2_ref[...]).astype(BF16)
    seq_start = (i % tiles_per_seq) == 0
    p, s = prev_rows, stride
    x2 = x1
    for j in range(nf):
        cols = slice(j * tf, (j + 1) * tf)
        a = jnp.dot(xn, wg_ref[:, cols], preferred_element_type=F32)
        u = jnp.dot(xn, wu_ref[:, cols], preferred_element_type=F32)
        prev = jnp.where(seq_start, buf_ref[:, cols], carry_s[:, cols])
        ext = jnp.concatenate([prev, a], axis=0)
        cw = cw_ref[:, cols]
        ac = cw[0:1] * ext[p - 2 * s:p - 2 * s + tm] + cw[1:2] * ext[p - s:p - s + tm] + cw[2:3] * ext[p:p + tm]
        carry_s[:, cols] = ext[tm:tm + p]
        st_ref[0, :, cols] = ext[tm + p - 2 * s:tm + p]
        x2 = x2 + _bdot(_silu(ac) * u, wd_ref[cols, :])
    xo_ref[...] = x2
    if final:
        y_ref[...] = _rms(x2, gf_ref[...])


def _out_ffn(x2d, mixes, wo, g2, wg, wu, cw, wd, buf, gf, *, tm, stride, tiles_per_seq, final):
    n = x2d.shape[0]
    nt = n // tm
    nf = FFN_SLABS
    tf = D_FF // nf
    prev_rows = buf.shape[0]
    kern = functools.partial(_out_ffn_kernel, tm=tm, tf=tf, nf=nf, stride=stride, prev_rows=prev_rows,
                             tiles_per_seq=tiles_per_seq, final=final)
    tok = pl.BlockSpec((tm, D_MODEL), lambda i: (i, 0))
    part = pl.BlockSpec((tm, GROUP_WIDTH), lambda i: (i, 0))
    const = lambda shape: pl.BlockSpec(shape, lambda i: (0, 0), pipeline_mode=pl.Buffered(1))
    in_specs = [tok, part, part, part, part,
                const((D_MODEL, D_MODEL)), const((1, D_MODEL)), const((D_MODEL, D_FF)), const((D_MODEL, D_FF)),
                const((FFN_CONV, D_FF)), const((D_FF, D_MODEL)), const((prev_rows, D_FF))]
    args = [x2d, *mixes, wo, g2, wg, wu, cw, wd, buf]
    out_specs = [tok, pl.BlockSpec((1, 2 * stride, D_FF), lambda i: (i, 0, 0))]
    out_shape = [jax.ShapeDtypeStruct((n, D_MODEL), F32),
                 jax.ShapeDtypeStruct((nt, 2 * stride, D_FF), F32)]
    if final:
        in_specs.append(const((1, D_MODEL)))
        args.append(gf)
        out_specs.append(tok)
        out_shape.append(jax.ShapeDtypeStruct((n, D_MODEL), F32))
    return pl.pallas_call(
        kern,
        grid=(nt,),
        in_specs=in_specs,
        out_specs=out_specs,
        out_shape=out_shape,
        scratch_shapes=[pltpu.VMEM((prev_rows, D_FF), F32)],
        compiler_params=_params("arbitrary"),
    )(*args)


def _short_conv(x, carry_ref, cw, tl):
    ext = jnp.concatenate([carry_ref[...], x], axis=0)
    y = cw[0:1] * ext[5:5 + tl]
    for i in range(1, SHORT_CONV):
        y = y + cw[i:i + 1] * ext[5 + i:5 + i + tl]
    carry_ref[...] = ext[tl:tl + SUBLANES]
    return y, ext


def _gdn_kernel(zg_ref, sm_ref, cst_ref, s0_ref, cw_ref, par_ref, nrm_ref,
                o_ref, cout_ref, sout_ref, carry_s, state_s, *, tl, chunk, l_real):
    t = pl.program_id(1)

    @pl.when(t == 0)
    def _():
        carry_s[...] = cst_ref[0]
        state_s[...] = s0_ref[0]

    z = zg_ref[0]
    gate = z[:, 768:1024]
    y, ext = _short_conv(z[:, 0:768], carry_s, cw_ref[...], tl)
    cout_ref[0] = ext[SUBLANES + l_real - 3:SUBLANES + l_real]
    y = _silu(y)

    small = sm_ref[0]
    par = par_ref[...]
    beta_c = _sigmoid(small)
    g_c = -jnp.exp(par[0:1]) * _softplus(small + par[1:2])
    if l_real < tl:
        real = lax.broadcasted_iota(jnp.int32, small.shape, 0) < l_real
        beta_c = jnp.where(real, beta_c, 0.0)
        g_c = jnp.where(real, g_c, 0.0)
    ri = lax.broadcasted_iota(jnp.int32, (tl, tl), 0)
    ci = lax.broadcasted_iota(jnp.int32, (tl, tl), 1)
    same = (ri // chunk) == (ci // chunk)
    gc = _fdot(jnp.where(same & (ci <= ri), 1.0, 0.0), g_c)
    grest = _fdot(jnp.where(same & (ci > ri), 1.0, 0.0), g_c)
    gc_t = gc.T

    ii = lax.broadcasted_iota(jnp.int32, (chunk, chunk), 0)
    jj = lax.broadcasted_iota(jnp.int32, (chunk, chunk), 1)
    causal, strict = ii >= jj, ii > jj
    eye = jnp.where(ii == jj, 1.0, 0.0)
    gnorm = nrm_ref[...]
    n_chunks = tl // chunk
    heads = range(N_HEADS)
    S = [state_s[h] for h in heads]
    out_rows = []
    for c0 in range(0, n_chunks, GDN_GROUP_CHUNKS):
        pairs = [(c, h) for c in range(c0, min(c0 + GDN_GROUP_CHUNKS, n_chunks)) for h in heads]
        rows_of = lambda c: slice(c * chunk, (c + 1) * chunk)
        q = [y[rows_of(c), h * 64:(h + 1) * 64] for c, h in pairs]
        k = [y[rows_of(c), 256 + h * 64:256 + (h + 1) * 64] for c, h in pairs]
        v = [y[rows_of(c), 512 + h * 64:512 + (h + 1) * 64] for c, h in pairs]
        q = [x * lax.rsqrt(jnp.sum(x * x, axis=-1, keepdims=True) + EPS) * SCALE for x in q]
        k = [x * lax.rsqrt(jnp.sum(x * x, axis=-1, keepdims=True) + EPS) for x in k]
        beta = [beta_c[rows_of(c), h:h + 1] for c, h in pairs]
        gcc = [gc[rows_of(c), 4 + h:5 + h] for c, h in pairs]
        gcr = [gc_t[4 + h:5 + h, rows_of(c)] for c, h in pairs]
        decay = [jnp.where(causal, jnp.exp(jnp.where(causal, a - b, 0.0)), 0.0) for a, b in zip(gcc, gcr)]
        kb = [a * b for a, b in zip(k, beta)]
        A = [jnp.where(strict, _bdot_nt(a, b) * d, 0.0) for a, b, d in zip(kb, k, decay)]
        T = [eye - a for a in A]
        P = A
        span = 2
        while span < chunk:
            P = [_dot3(p, p) for p in P]
            T = [t + _dot3(t, p) for t, p in zip(T, P)]
            span *= 2
        u = [_bdot(t, a * b) for t, a, b in zip(T, v, beta)]
        w = [_bdot(t, a * jnp.exp(g)) for t, a, g in zip(T, kb, gcc)]
        attn = [_bdot_nt(a, b) * d for a, b, d in zip(q, k, decay)]
        qe = [a * jnp.exp(g) for a, g in zip(q, gcc)]
        kd = [a * jnp.exp(grest[rows_of(c), 4 + h:5 + h]) for a, (c, h) in zip(k, pairs)]
        for i0 in range(0, len(pairs), N_HEADS):
            c = pairs[i0][0]
            v_new = [u[i0 + h] - _bdot(w[i0 + h], S[h]) for h in heads]
            o = [_bdot(qe[i0 + h], S[h]) + _bdot(attn[i0 + h], v_new[h]) for h in heads]
            last = (c + 1) * chunk - 1
            S = [S[h] * jnp.exp(gc[last:last + 1, 4 + h:5 + h]) + _bdot_tn(kd[i0 + h], v_new[h]) for h in heads]
            o = [_rms(o[h], gnorm) * _silu(gate[rows_of(c), h * 64:(h + 1) * 64]) for h in heads]
            out_rows.append(jnp.concatenate(o, axis=1))
    for h in heads:
        state_s[h] = S[h]
        sout_ref[0, h] = S[h]
    o_ref[0] = jnp.concatenate(out_rows, axis=0)


def _gdn(zg, small, conv_state8, s0, cw, par, gnorm, *, tl, chunk, l_real):
    b, l, _ = zg.shape
    nt = l // tl
    kern = functools.partial(_gdn_kernel, tl=tl, chunk=chunk, l_real=l_real)
    return pl.pallas_call(
        kern,
        grid=(b, nt),
        in_specs=[pl.BlockSpec((1, tl, 1024), lambda i, t: (i, t, 0)),
                  pl.BlockSpec((1, tl, LANES), lambda i, t: (i, t, 0)),
                  pl.BlockSpec((1, SUBLANES, 768), lambda i, t: (i, 0, 0)),
                  pl.BlockSpec((1, N_HEADS, 64, 64), lambda i, t: (i, 0, 0, 0)),
                  pl.BlockSpec((SHORT_CONV, 768), lambda i, t: (0, 0)),
                  pl.BlockSpec((2, LANES), lambda i, t: (0, 0)),
                  pl.BlockSpec((1, HEAD_DIM), lambda i, t: (0, 0))],
        out_specs=[pl.BlockSpec((1, tl, GROUP_WIDTH), lambda i, t: (i, t, 0)),
                   pl.BlockSpec((1, 3, 768), lambda i, t: (i, 0, 0)),
                   pl.BlockSpec((1, N_HEADS, 64, 64), lambda i, t: (i, 0, 0, 0))],
        out_shape=[jax.ShapeDtypeStruct((b, l, GROUP_WIDTH), F32),
                   jax.ShapeDtypeStruct((b, 3, 768), F32),
                   jax.ShapeDtypeStruct((b, N_HEADS, 64, 64), F32)],
        scratch_shapes=[pltpu.VMEM((SUBLANES, 768), F32), pltpu.VMEM((N_HEADS, 64, 64), F32)],
        compiler_params=_params("parallel", "arbitrary"),
    )(zg, small, conv_state8, s0, cw, par, gnorm)


def _lru_kernel(zl_ref, cst_ref, h0_ref, cw_ref, vec_ref, wa_ref, wx_ref,
                y_ref, cout_ref, hout_ref, carry_s, h_s, *, tl, l_real):
    t = pl.program_id(1)

    @pl.when(t == 0)
    def _():
        carry_s[...] = cst_ref[0]
        h_s[...] = h0_ref[0]

    z = zl_ref[0]
    gate = z[:, 256:512]
    vec = vec_ref[...]
    xc, ext = _short_conv(z[:, 0:256], carry_s, cw_ref[...], tl)
    cout_ref[0] = ext[SUBLANES + l_real - 3:SUBLANES + l_real]
    xc = xc + vec[0:1]
    r = _sigmoid(_bdot(xc, wa_ref[...]) + vec[1:2])
    i = _sigmoid(_bdot(xc, wx_ref[...]) + vec[2:3])
    log_a = -LRU_C * r * _softplus(-vec[3:4])
    a = jnp.exp(log_a)
    u = jnp.sqrt(1.0 - a * a) * (i * xc)
    rowid = lax.broadcasted_iota(jnp.int32, u.shape, 0)
    u = u + jnp.where(rowid == 0, a * h_s[...], 0.0)
    s = 1
    while s < tl:
        a_sh = jnp.concatenate([jnp.ones((s, GROUP_WIDTH), F32), a[:tl - s]], axis=0)
        u_sh = jnp.concatenate([jnp.zeros((s, GROUP_WIDTH), F32), u[:tl - s]], axis=0)
        u = a * u_sh + u
        a = a * a_sh
        s *= 2
    h_last = u[l_real - 1:l_real]
    h_s[...] = h_last
    hout_ref[0] = h_last
    y_ref[0] = u * _gelu_tanh(gate)


def _lru(zl, conv_state8, h0, cw, vec, wa, wx, *, tl, l_real):
    b, l, _ = zl.shape
    nt = l // tl
    kern = functools.partial(_lru_kernel, tl=tl, l_real=l_real)
    return pl.pallas_call(
        kern,
        grid=(b, nt),
        in_specs=[pl.BlockSpec((1, tl, 512), lambda i, t: (i, t, 0)),
                  pl.BlockSpec((1, SUBLANES, GROUP_WIDTH), lambda i, t: (i, 0, 0)),
                  pl.BlockSpec((1, 1, GROUP_WIDTH), lambda i, t: (i, 0, 0)),
                  pl.BlockSpec((SHORT_CONV, GROUP_WIDTH), lambda i, t: (0, 0)),
                  pl.BlockSpec((4, GROUP_WIDTH), lambda i, t: (0, 0)),
                  pl.BlockSpec((GROUP_WIDTH, GROUP_WIDTH), lambda i, t: (0, 0)),
                  pl.BlockSpec((GROUP_WIDTH, GROUP_WIDTH), lambda i, t: (0, 0))],
        out_specs=[pl.BlockSpec((1, tl, GROUP_WIDTH), lambda i, t: (i, t, 0)),
                   pl.BlockSpec((1, 3, GROUP_WIDTH), lambda i, t: (i, 0, 0)),
                   pl.BlockSpec((1, 1, GROUP_WIDTH), lambda i, t: (i, 0, 0))],
        out_shape=[jax.ShapeDtypeStruct((b, l, GROUP_WIDTH), F32),
                   jax.ShapeDtypeStruct((b, 3, GROUP_WIDTH), F32),
                   jax.ShapeDtypeStruct((b, 1, GROUP_WIDTH), F32)],
        scratch_shapes=[pltpu.VMEM((SUBLANES, GROUP_WIDTH), F32), pltpu.VMEM((1, GROUP_WIDTH), F32)],
        compiler_params=_params("parallel", "arbitrary"),
    )(zl, conv_state8, h0, cw, vec, wa, wx)


def _compress_project(x_kv, pe2, w_kv):
    rows = x_kv.shape[0]
    first = lax.broadcasted_iota(jnp.int32, (2 * SUBLANES, x_kv.shape[1]), 0) < SUBLANES
    p = _bdot(jnp.concatenate([x_kv, jnp.where(first, pe2[0:1], pe2[1:2])], axis=0), w_kv)
    return (p[:rows, 0:128] + p[rows:rows + 1, 0:128],
            p[:rows, 128:256] + p[rows + SUBLANES:rows + SUBLANES + 1, 128:256])


def _compress_finish(pa, pb, w2):
    pb_next = jnp.concatenate([pb[1:], jnp.zeros((1, pb.shape[1]), F32)], axis=0)
    return _bdot(_silu(pa + pb_next), w2)


def _cmp_prompt_kernel(x_ref, pe_ref, w_ref, w2_ref, o_ref):
    x = x_ref[0]
    halves = []
    for kv in range(2):
        x_kv = jnp.concatenate([x[:, t * 256 + kv * 128:t * 256 + (kv + 1) * 128] for t in range(CMP_STRIDE)], axis=1)
        halves.append(_compress_project(x_kv, pe_ref[kv], w_ref[kv]))
    o_ref[0] = _compress_finish(jnp.concatenate([halves[0][0], halves[1][0]], axis=1),
                                jnp.concatenate([halves[0][1], halves[1][1]], axis=1), w2_ref[...])


def _cmp_prompt(xc, pe, w, w2):
    b, m, f = xc.shape
    full = lambda a: pl.BlockSpec(a.shape, lambda i: (0,) * a.ndim)
    return pl.pallas_call(
        _cmp_prompt_kernel,
        grid=(b,),
        in_specs=[pl.BlockSpec((1, m, f), lambda i: (i, 0, 0)), full(pe), full(w), full(w2)],
        out_specs=pl.BlockSpec((1, m, 256), lambda i: (i, 0, 0)),
        out_shape=jax.ShapeDtypeStruct((b, m, 256), F32),
        compiler_params=_params("parallel"),
    )(xc, pe, w, w2)


def _overlap_matrix(m, nsp, nc, ns, transposed=False):
    shape = (nsp, m) if transposed else (m, nsp)
    nn = lax.broadcasted_iota(jnp.int32, shape, 1 if transposed else 0)
    mm = lax.broadcasted_iota(jnp.int32, shape, 0 if transposed else 1)
    ov = (jnp.minimum(nn * CMP_STRIDE + CMP_LEN - 1, mm * SLC_BLOCK + SLC_BLOCK - 1)
          - jnp.maximum(nn * CMP_STRIDE, mm * SLC_BLOCK) + 1)
    return jnp.where((nn < nc) & (mm < ns), jnp.maximum(ov, 0).astype(F32) * (1.0 / CMP_LEN), 0.0)


def _select_bias(imp, qpos, ns):
    rows, nsp = imp.shape
    j = lax.broadcasted_iota(jnp.int32, (rows, nsp), 1)
    cur = qpos // SLC_BLOCK
    valid = (j <= cur) & (j < ns)
    forced = valid & ((j == 0) | (j > cur - N_LOCAL))
    score = jnp.where(valid, imp + jnp.where(forced, FORCE_BONUS, 0.0), NEG)
    score = jnp.where(j < ns, score, -3.0e38)
    cnt = jnp.zeros((rows, nsp), jnp.int32)
    for i in range(ns):
        si = score[:, i:i + 1]
        beats = (si > score) | ((si == score) & (i < j))
        cnt = cnt + beats.astype(jnp.int32)
    return jnp.where((cnt < min(SLC_TOPK, ns)) & valid, 0.0, NEG)


NSA_KEY_CHUNK = 512


NSA_STRIP = 16
NSA_STRIP_UNROLL = 64
SEL_LANES = 64
DEN_ROWS = 16


def _select_bias_t(imp_t, qpos_row, ns):
    nsr, tq = imp_t.shape
    j = lax.broadcasted_iota(jnp.int32, (nsr, tq), 0)
    cur = qpos_row // SLC_BLOCK
    valid = (j <= cur) & (j < ns)
    forced = valid & ((j == 0) | (j > cur - N_LOCAL))
    score = jnp.where(valid, imp_t + jnp.where(forced, FORCE_BONUS, 0.0), NEG)
    score = jnp.where(j < ns, score, -3.0e38)
    cnt = jnp.zeros((nsr, tq), jnp.int32)
    for i in range(ns):
        si = score[i:i + 1, :]
        beats = (si > score) | ((si == score) & (i < j))
        cnt = cnt + beats.astype(jnp.int32)
    return jnp.where((cnt < min(SLC_TOPK, ns)) & valid, 0.0, NEG)


def _nsa_prompt_kernel(q_ref, sm_ref, kc_ref, slc_ref, win_ref, e_ref, o_ref,
                       krhs_s, vslc_s, kwin_s, vwin_s, s_s, e_s, *, tq, l, nc, ns, nsr, wk, kchunk):
    t = pl.program_id(1)
    q0 = t * tq

    @pl.when(t == 0)
    def _():
        ones = jnp.ones((DEN_ROWS, l), BF16)
        for g in range(NSA_KV_HEADS):
            krow = slice(g * 64, (g + 1) * 64)
            vrow = slice(128 + g * 64, 128 + (g + 1) * 64)
            krhs_s[g, 0:64, :] = slc_ref[0, krow, :].astype(BF16)
            krhs_s[g, 64:64 + SEL_LANES, :] = e_ref[...]
            vslc_s[g, 0:64, :] = slc_ref[0, vrow, :].astype(BF16)
            vslc_s[g, 64:64 + DEN_ROWS, :] = ones
            kwin_s[g] = win_ref[0, krow, :].astype(BF16)
            vwin_s[g, 0:64, :] = win_ref[0, vrow, :].astype(BF16)
            vwin_s[g, 64:64 + DEN_ROWS, :] = ones

    q = q_ref[0] * SCALE
    gates = _sigmoid(sm_ref[0])
    kcvc = kc_ref[0]
    m = kcvc.shape[0]
    qpos = q0 + lax.broadcasted_iota(jnp.int32, (tq, 1), 0)
    qpos2 = jnp.concatenate([qpos, qpos], axis=0)
    qpos_row = q0 + lax.broadcasted_iota(jnp.int32, (1, tq), 1)
    n_id = lax.broadcasted_iota(jnp.int32, (2 * tq, m), 1)
    cmask = (n_id * CMP_STRIDE + CMP_LEN - 1 <= qpos2) & (n_id < nc)
    ovl_t = _overlap_matrix(m, nsr, nc, ns, transposed=True)
    start = pl.multiple_of(jnp.clip(q0 + tq - wk, 0, l - wk), LANES)
    dwin = qpos - (start + lax.broadcasted_iota(jnp.int32, (tq, wk), 1))
    wbias = jnp.where((dwin >= 0) & (dwin < WINDOW), 0.0, NEG)
    tri = jnp.where(lax.broadcasted_iota(jnp.int32, (tq, tq), 1) > lax.broadcasted_iota(jnp.int32, (tq, tq), 0),
                    NEG, 0.0)
    n_need = (q0 + tq + kchunk - 1) // kchunk

    def softmax_pv(nk, v_aug):
        def strip(i, carry):
            r = pl.multiple_of(i * NSA_STRIP, NSA_STRIP)
            s = s_s[pl.ds(r, NSA_STRIP), 0:nk]
            e_s[pl.ds(r, NSA_STRIP), 0:nk] = jnp.exp(s - jnp.max(s, axis=-1, keepdims=True)).astype(BF16)
            return carry

        lax.fori_loop(0, 2 * tq // NSA_STRIP, strip, 0, unroll=NSA_STRIP_UNROLL)
        oa = lax.dot_general(e_s[:, 0:nk], v_aug, (((1,), (1,)), ((), ())), preferred_element_type=F32)
        return oa[:, 0:64] * (1.0 / oa[:, 64:65])

    for g in range(NSA_KV_HEADS):
        h0 = g * NSA_GROUP
        q_heads = [q[:, (h0 + hg) * 64:(h0 + hg + 1) * 64] for hg in range(NSA_GROUP)]
        qs = jnp.concatenate(q_heads, axis=0)
        pc = _masked_softmax(_bdot_nt(qs, kcvc[:, g * 64:(g + 1) * 64]), cmask)
        o_cmp = _bdot(pc, kcvc[:, 128 + g * 64:128 + (g + 1) * 64])
        imp_t = lax.dot_general(ovl_t, pc[:tq] + pc[tq:], (((1,), (1,)), ((), ())), preferred_element_type=F32,
                                precision=lax.Precision.HIGHEST)
        bias_t = _select_bias_t(imp_t, qpos_row, ns)
        bias = jnp.concatenate([bias_t, jnp.zeros((LANES - nsr, tq), F32)], axis=0).T[:, 0:SEL_LANES]
        lhs = jnp.concatenate([jnp.concatenate([qh, bias], axis=1) for qh in q_heads], axis=0).astype(BF16)

        s3 = jnp.dot(qs.astype(BF16), kwin_s[g, :, pl.ds(start, wk)], preferred_element_type=F32)
        s_s[0:tq, 0:wk] = s3[0:tq] + wbias
        s_s[tq:2 * tq, 0:wk] = s3[tq:2 * tq] + wbias
        o_win = softmax_pv(wk, vwin_s[g, :, pl.ds(start, wk)])
        part = [gates[:, 8 + 3 * (h0 + hg):9 + 3 * (h0 + hg)] * o_cmp[hg * tq:(hg + 1) * tq]
                + gates[:, 10 + 3 * (h0 + hg):11 + 3 * (h0 + hg)] * o_win[hg * tq:(hg + 1) * tq]
                for hg in range(NSA_GROUP)]
        for nkc in range(1, l // kchunk + 1):

            @pl.when(n_need == nkc)
            def _(nk=nkc * kchunk, g=g, h0=h0, lhs=lhs, part=part):
                s_s[:, 0:nk] = jnp.dot(lhs, krhs_s[g, :, 0:nk], preferred_element_type=F32)
                s_s[0:tq, pl.ds(pl.multiple_of(q0, LANES), tq)] += tri
                s_s[tq:2 * tq, pl.ds(pl.multiple_of(q0, LANES), tq)] += tri
                o_slc = softmax_pv(nk, vslc_s[g, :, 0:nk])
                for hg in range(NSA_GROUP):
                    head = h0 + hg
                    c1 = 9 + 3 * head
                    o_ref[0, :, head * 64:(head + 1) * 64] = (part[hg]
                                                              + gates[:, c1:c1 + 1] * o_slc[hg * tq:(hg + 1) * tq])


def _block_expander(nsp, nkeys):
    blk = lax.broadcasted_iota(jnp.int32, (nsp, nkeys), 0)
    key = lax.broadcasted_iota(jnp.int32, (nsp, nkeys), 1)
    return jnp.where(key // SLC_BLOCK == blk, 1.0, 0.0).astype(BF16)


def _nsa_prompt(q, small, kcvc, slc, win, *, tq, layer):
    b, l, _ = q.shape
    m = kcvc.shape[1]
    nc = m - 1
    ns = -(-l // SLC_BLOCK)
    nsr = -(-ns // SUBLANES) * SUBLANES
    assert nsr <= SEL_LANES
    wk = min(WINDOW + tq, l)
    kern = functools.partial(_nsa_prompt_kernel, tq=tq, l=l, nc=nc, ns=ns, nsr=nsr, wk=wk, kchunk=min(NSA_KEY_CHUNK, l))
    seq = lambda rows: pl.BlockSpec((None, 1, rows, l), lambda i, t: (layer, i, 0, 0))
    tile = lambda wd: pl.BlockSpec((1, tq, wd), lambda i, t: (i, t, 0))
    return pl.pallas_call(
        kern,
        grid=(b, l // tq),
        in_specs=[tile(256), tile(LANES), pl.BlockSpec((1, m, 256), lambda i, t: (i, 0, 0)), seq(256), seq(256),
                  pl.BlockSpec((SEL_LANES, l), lambda i, t: (0, 0))],
        out_specs=tile(256),
        out_shape=jax.ShapeDtypeStruct((b, l, 256), F32),
        scratch_shapes=[pltpu.VMEM((NSA_KV_HEADS, 64 + SEL_LANES, l), BF16),
                        pltpu.VMEM((NSA_KV_HEADS, 64 + DEN_ROWS, l), BF16),
                        pltpu.VMEM((NSA_KV_HEADS, 64, l), BF16),
                        pltpu.VMEM((NSA_KV_HEADS, 64 + DEN_ROWS, l), BF16),
                        pltpu.VMEM((2 * tq, l), F32),
                        pltpu.VMEM((2 * tq, l), BF16)],
        compiler_params=_params("parallel", "arbitrary"),
    )(q, small, kcvc, slc, win, _block_expander(SEL_LANES, l))


def _later_keys_matrix(n):
    a = lax.broadcasted_iota(jnp.int32, (n, n), 0)
    c = lax.broadcasted_iota(jnp.int32, (n, n), 1)
    return jnp.where(a > c, 1.0, 0.0).astype(BF16)


SB_STRIP = 32


def _sb_prompt_kernel(q_ref, kv_ref, later2_ref, o_ref, acc_s, aft_s, z_s, lf_s, hl_s, loc_s, a_s, *, tq):
    qi = pl.program_id(1)
    q = q_ref[0] * SCALE
    acc_s[...] = jnp.zeros_like(acc_s)
    aft_s[...] = jnp.zeros_like(aft_s)
    strips = [slice(r, r + SB_STRIP) for r in range(0, tq, SB_STRIP)]

    def block(k0, diagonal):
        for h in range(N_HEADS):
            kt = kv_ref[0, h * 64:(h + 1) * 64, pl.ds(k0, tq)]
            z_s[h] = _bdot(q[:, h * 64:(h + 1) * 64], kt)
        for h in range(N_HEADS):
            for rows in strips:
                z = z_s[h, rows, :]
                lf = -(jnp.maximum(z, 0.0) + jnp.log(1.0 + jnp.exp(-jnp.abs(z))))
                if diagonal:
                    keep = (lax.broadcasted_iota(jnp.int32, (SB_STRIP, tq), 1)
                            < rows.start + lax.broadcasted_iota(jnp.int32, (SB_STRIP, tq), 0))
                    lf = jnp.where(keep, lf, 0.0)
                hi = lf.astype(BF16)
                lf_s[h, rows, :] = lf
                hl_s[h, rows, 0:tq] = hi
                hl_s[h, rows, tq:2 * tq] = (lf - hi.astype(F32)).astype(BF16)
        for h in range(N_HEADS):
            loc_s[h] = jnp.dot(hl_s[h], later2_ref[...], preferred_element_type=F32)
        for h in range(N_HEADS):
            for rows in strips:
                a = jnp.exp(z_s[h, rows, :] + lf_s[h, rows, :] + (loc_s[h, rows, :] + aft_s[h, rows, :]))
                if diagonal:
                    keep = (lax.broadcasted_iota(jnp.int32, (SB_STRIP, tq), 1)
                            < rows.start + lax.broadcasted_iota(jnp.int32, (SB_STRIP, tq), 0))
                    a = jnp.where(keep, a, 0.0)
                a_s[h, rows, :] = a.astype(BF16)
        for h in range(N_HEADS):
            vt = kv_ref[0, 256 + h * 64:256 + (h + 1) * 64, pl.ds(k0, tq)]
            acc_s[:, h * 64:(h + 1) * 64] += _bdot_nt(a_s[h], vt)
            aft_s[h] = aft_s[h] + loc_s[h, :, 0:1] + lf_s[h, :, 0:1]

    block(pl.multiple_of(qi * tq, tq), True)

    def body(it, carry):
        block(pl.multiple_of((qi - 1 - it) * tq, tq), False)
        return carry

    lax.fori_loop(0, qi, body, 0)
    o_ref[0] = acc_s[...]


def _sb_prompt(q, kv, *, tq, layer):
    b, l, _ = q.shape
    kern = functools.partial(_sb_prompt_kernel, tq=tq)
    return pl.pallas_call(
        kern,
        grid=(b, l // tq),
        in_specs=[pl.BlockSpec((1, tq, 256), lambda i, t: (i, t, 0)),
                  pl.BlockSpec((None, 1, 512, l), lambda i, t: (layer, i, 0, 0)),
                  pl.BlockSpec((2 * tq, tq), lambda i, t: (0, 0))],
        out_specs=pl.BlockSpec((1, tq, 256), lambda i, t: (i, t, 0)),
        out_shape=jax.ShapeDtypeStruct((b, l, 256), F32),
        scratch_shapes=[pltpu.VMEM((tq, 256), F32), pltpu.VMEM((N_HEADS, tq, 1), F32),
                        pltpu.VMEM((N_HEADS, tq, tq), F32), pltpu.VMEM((N_HEADS, tq, tq), F32),
                        pltpu.VMEM((N_HEADS, tq, 2 * tq), BF16), pltpu.VMEM((N_HEADS, tq, tq), F32),
                        pltpu.VMEM((N_HEADS, tq, tq), BF16)],
        compiler_params=_params("parallel", "arbitrary"),
    )(q, kv, jnp.concatenate([_later_keys_matrix(tq)] * 2, axis=0))


SB_PAGE_GROUP = 8


def _sb_sample_kernel(pt_ref, qbd_ref, new_ref, later_ref, *rest, pg, n_pg, pos0):
    pages = rest[:pg]
    o_ref, acc_s, aft_s = rest[pg:]
    g = pl.program_id(1)
    qbd = (qbd_ref[0] * SCALE).astype(BF16)
    later = later_ref[...]
    rows = qbd.shape[0]

    def blocks(kv_refs, mask):
        n = len(kv_refs)
        z = jnp.concatenate([jnp.dot(qbd, r[0, 0:256, :].astype(BF16), preferred_element_type=F32)
                             for r in kv_refs], axis=0)
        lf = -(jnp.maximum(z, 0.0) + jnp.log(1.0 + jnp.exp(-jnp.abs(z))))
        if mask is not None:
            lf = jnp.where(mask, lf, 0.0)
        loc = _split_dot(lf, later)
        aft = aft_s[...]
        acc = acc_s[...]
        for i, r in enumerate(kv_refs):
            blk = slice(i * rows, (i + 1) * rows)
            a = jnp.exp(z[blk] + lf[blk] + (loc[blk] + aft))
            if mask is not None:
                a = jnp.where(mask, a, 0.0)
            acc = acc + _bdot_nt(a, r[0, 256:512, :])
            aft = aft + (loc[blk, 0:1] + lf[blk, 0:1])
        acc_s[...] = acc
        aft_s[...] = aft

    @pl.when(g == 0)
    def _():
        acc_s[...] = jnp.zeros_like(acc_s)
        aft_s[...] = jnp.zeros_like(aft_s)
        t = lax.broadcasted_iota(jnp.int32, (rows, PAGE_SIZE), 0) % SUBLANES
        key = lax.broadcasted_iota(jnp.int32, (rows, PAGE_SIZE), 1)
        blocks([new_ref], key < t)

    for i0 in range(0, pg, SB_PAGE_GROUP):
        blocks(pages[i0:i0 + SB_PAGE_GROUP], None)

    @pl.when(g == n_pg - 1)
    def _():
        acc = acc_s[...]
        lane_head = lax.broadcasted_iota(jnp.int32, (SUBLANES, 256), 1) // HEAD_DIM
        out = jnp.zeros((SUBLANES, 256), F32)
        for h in range(N_HEADS):
            out = out + jnp.where(lane_head == h, acc[h * SUBLANES:(h + 1) * SUBLANES], 0.0)
        o_ref[0] = out


def _sb_sample(page_table, qbd, new_rows, cache, layer_base, *, pg, pos0):
    b, n_pages = page_table.shape
    n_pg = n_pages // pg
    kern = functools.partial(_sb_sample_kernel, pg=pg, n_pg=n_pg, pos0=pos0)

    def page_spec(i):
        return pl.BlockSpec((1, 512, PAGE_SIZE),
                            lambda bi, g, pt: (layer_base + pt[bi, (n_pg - 1 - g) * pg + (pg - 1 - i)], 0, 0))

    grid_spec = pltpu.PrefetchScalarGridSpec(
        num_scalar_prefetch=1,
        grid=(b, n_pg),
        in_specs=[pl.BlockSpec((1, 32, 256), lambda bi, g, pt: (bi, 0, 0)),
                  pl.BlockSpec((1, 512, PAGE_SIZE), lambda bi, g, pt: (bi, 0, 0)),
                  pl.BlockSpec((PAGE_SIZE, PAGE_SIZE), lambda bi, g, pt: (0, 0))]
                 + [page_spec(i) for i in range(pg)],
        out_specs=pl.BlockSpec((1, SUBLANES, 256), lambda bi, g, pt: (bi, 0, 0)),
        scratch_shapes=[pltpu.VMEM((32, 256), F32), pltpu.VMEM((32, 1), F32)],
    )
    return pl.pallas_call(
        kern,
        grid_spec=grid_spec,
        out_shape=jax.ShapeDtypeStruct((b, SUBLANES, 256), F32),
        compiler_params=_params("parallel", "arbitrary"),
    )(page_table, qbd, new_rows, _later_keys_matrix(PAGE_SIZE), *([cache] * pg))


CMP_PAGE_GROUP = 16


def _nsa_sample_cmp_kernel(pt_ref, q_ref, pe_ref, w_ref, w2_ref, *rest, n_pages, pos0, nsp):
    pages = rest[:n_pages]
    ocmp_ref, sel_ref, xs_ref = rest[n_pages:]
    pas, pbs = [], []
    for c in range(0, n_pages, CMP_PAGE_GROUP):
        npg = min(CMP_PAGE_GROUP, n_pages - c)
        for i in range(npg):
            for kv in range(2):
                xs_ref[kv, i * PAGE_SIZE:(i + 1) * PAGE_SIZE, :] = pages[c + i][0, kv * LANES:(kv + 1) * LANES, :].T
        rows = npg * (PAGE_SIZE // CMP_STRIDE)
        halves = []
        for kv in range(2):
            x_kv = jnp.concatenate([xs_ref[kv, pl.ds(t, rows, stride=CMP_STRIDE), :] for t in range(CMP_STRIDE)], axis=1)
            halves.append(_compress_project(x_kv, pe_ref[kv], w_ref[kv]))
        pas.append(jnp.concatenate([halves[0][0], halves[1][0]], axis=1))
        pbs.append(jnp.concatenate([halves[0][1], halves[1][1]], axis=1))
    kcvc = _compress_finish(jnp.concatenate(pas, axis=0), jnp.concatenate(pbs, axis=0), w2_ref[...])
    m = kcvc.shape[0]
    nc = m - 1
    ns = 2 * n_pages + 1
    q = q_ref[0]
    tq = q.shape[0]
    qpos = pos0 + lax.broadcasted_iota(jnp.int32, (tq, 1), 0)
    qpos2 = jnp.concatenate([qpos, qpos], axis=0)
    n_id = lax.broadcasted_iota(jnp.int32, (2 * tq, m), 1)
    cmask = (n_id * CMP_STRIDE + CMP_LEN - 1 <= qpos2) & (n_id < nc)
    ovl = _overlap_matrix(m, nsp, nc, ns)
    for g in range(NSA_KV_HEADS):
        h0 = g * NSA_GROUP
        qs = jnp.concatenate([q[:, h0 * 64:(h0 + 1) * 64], q[:, (h0 + 1) * 64:(h0 + 2) * 64]], axis=0)
        pc = _masked_softmax(_bdot_nt(qs, kcvc[:, g * 64:(g + 1) * 64]) * SCALE, cmask)
        o_cmp = _bdot(pc, kcvc[:, 128 + g * 64:128 + (g + 1) * 64])
        sel_ref[0, g] = _select_bias(_fdot(pc[:tq] + pc[tq:], ovl), qpos, ns)
        for hg in range(NSA_GROUP):
            ocmp_ref[0, :, (h0 + hg) * 64:(h0 + hg + 1) * 64] = o_cmp[hg * tq:(hg + 1) * tq]


def _nsa_sample_cmp(page_table, q, pe, w, w2, cache, layer_base, *, pos0):
    b, n_pages = page_table.shape
    nsp = -(-(2 * n_pages + 1) // LANES) * LANES
    kern = functools.partial(_nsa_sample_cmp_kernel, n_pages=n_pages, pos0=pos0, nsp=nsp)
    full = lambda a: pl.BlockSpec(a.shape, lambda bi, pt: (0,) * a.ndim)

    def page_spec(i):
        return pl.BlockSpec((1, 256, PAGE_SIZE), lambda bi, pt: (layer_base + pt[bi, i], 0, 0))

    grid_spec = pltpu.PrefetchScalarGridSpec(
        num_scalar_prefetch=1,
        grid=(b,),
        in_specs=[pl.BlockSpec((1, SUBLANES, 256), lambda bi, pt: (bi, 0, 0)),
                  full(pe), full(w), full(w2)] + [page_spec(i) for i in range(n_pages)],
        out_specs=[pl.BlockSpec((1, SUBLANES, 256), lambda bi, pt: (bi, 0, 0)),
                   pl.BlockSpec((1, NSA_KV_HEADS, SUBLANES, nsp), lambda bi, pt: (bi, 0, 0, 0))],
        scratch_shapes=[pltpu.VMEM((2, CMP_PAGE_GROUP * PAGE_SIZE, LANES), F32)],
    )
    return pl.pallas_call(
        kern,
        grid_spec=grid_spec,
        out_shape=[jax.ShapeDtypeStruct((b, SUBLANES, 256), F32),
                   jax.ShapeDtypeStruct((b, NSA_KV_HEADS, SUBLANES, nsp), F32)],
        compiler_params=_params("parallel"),
    )(page_table, q, pe, w, w2, *([cache] * n_pages))


def _softmax_rows(s):
    e = jnp.exp(s - jnp.max(s, axis=-1, keepdims=True))
    return e * (1.0 / jnp.sum(e, axis=-1, keepdims=True))


def _nsa_sample_slc_kernel(pt_ref, q_ref, sm_ref, ocmp_ref, sel_ref, newslc_ref, pastwin_ref, newwin_ref,
                           *rest, n_pages, pos0):
    pages = rest[:n_pages]
    o_ref, z_s = rest[n_pages:]
    q = q_ref[0] * SCALE
    tq = q.shape[0]
    gates = _sigmoid(sm_ref[0])
    nk = (n_pages + 1) * PAGE_SIZE
    past = pastwin_ref.shape[2]
    t2 = lax.broadcasted_iota(jnp.int32, (2 * tq, 1), 0) % tq
    kposw = jnp.concatenate([lax.broadcasted_iota(jnp.int32, (2 * tq, past), 1) - past,
                             lax.broadcasted_iota(jnp.int32, (2 * tq, PAGE_SIZE), 1)], axis=1)
    dwin = t2 - kposw
    wbias = jnp.where((dwin >= 0) & (dwin < WINDOW), 0.0, NEG)
    lane = lax.broadcasted_iota(jnp.int32, (2 * tq, PAGE_SIZE), 1)
    lower_block = lane < SLC_BLOCK
    new_causal = jnp.where(lane <= t2, 0.0, NEG)
    for g in range(NSA_KV_HEADS):
        h0 = g * NSA_GROUP
        krow = slice(g * 64, (g + 1) * 64)
        vrow = slice(128 + g * 64, 128 + (g + 1) * 64)
        qs = jnp.concatenate([q[:, h0 * 64:(h0 + 1) * 64], q[:, (h0 + 1) * 64:(h0 + 2) * 64]], axis=0).astype(BF16)
        bias = jnp.concatenate([sel_ref[0, g], sel_ref[0, g]], axis=0)
        dot = lambda k_t: jnp.dot(qs, k_t.astype(BF16), preferred_element_type=F32)
        for p in range(n_pages):
            z_s[:, p * PAGE_SIZE:(p + 1) * PAGE_SIZE] = dot(pages[p][0, krow, :]) + jnp.where(
                lower_block, bias[:, 2 * p:2 * p + 1], bias[:, 2 * p + 1:2 * p + 2])
        z_s[:, n_pages * PAGE_SIZE:nk] = dot(newslc_ref[0, krow, :]) + bias[:, 2 * n_pages:2 * n_pages + 1] + new_causal
        p2 = _softmax_rows(z_s[...])
        o_slc = _bdot_nt(p2[:, n_pages * PAGE_SIZE:nk], newslc_ref[0, vrow, :])
        for p in range(n_pages):
            o_slc = o_slc + _bdot_nt(p2[:, p * PAGE_SIZE:(p + 1) * PAGE_SIZE], pages[p][0, vrow, :])
        p3 = _softmax_rows(jnp.concatenate([dot(pastwin_ref[0, krow, :]), dot(newwin_ref[0, krow, :])], axis=1) + wbias)
        o_win = _bdot_nt(p3[:, 0:past], pastwin_ref[0, vrow, :]) + _bdot_nt(p3[:, past:], newwin_ref[0, vrow, :])
        for hg in range(NSA_GROUP):
            head = h0 + hg
            rows = slice(hg * tq, (hg + 1) * tq)
            c0 = 8 + 3 * head
            o_ref[0, :, head * 64:(head + 1) * 64] = (gates[:, c0:c0 + 1] * ocmp_ref[0, :, head * 64:(head + 1) * 64]
                                                      + gates[:, c0 + 1:c0 + 2] * o_slc[rows]
                                                      + gates[:, c0 + 2:c0 + 3] * o_win[rows])


def _nsa_sample_slc(page_table, q, small, o_cmp, sel, new_slc, past_win, past_win_base, new_win, cache, layer_base,
                    *, pos0):
    b, n_pages = page_table.shape
    nsp = sel.shape[-1]
    nk = (n_pages + 1) * PAGE_SIZE
    past = past_win.shape[2]
    kern = functools.partial(_nsa_sample_slc_kernel, n_pages=n_pages, pos0=pos0)
    per_b = lambda r, wd: pl.BlockSpec((1, r, wd), lambda bi, pt: (bi, 0, 0))

    def page_spec(i):
        return pl.BlockSpec((1, 256, PAGE_SIZE), lambda bi, pt: (layer_base + pt[bi, i], 0, 0))

    grid_spec = pltpu.PrefetchScalarGridSpec(
        num_scalar_prefetch=1,
        grid=(b,),
        in_specs=[per_b(SUBLANES, 256), per_b(SUBLANES, LANES), per_b(SUBLANES, 256),
                  pl.BlockSpec((1, NSA_KV_HEADS, SUBLANES, nsp), lambda bi, pt: (bi, 0, 0, 0)),
                  per_b(256, PAGE_SIZE),
                  pl.BlockSpec((1, 256, past), lambda bi, pt: (past_win_base + bi, 0, 0)),
                  per_b(256, PAGE_SIZE)] + [page_spec(i) for i in range(n_pages)],
        out_specs=per_b(SUBLANES, 256),
        scratch_shapes=[pltpu.VMEM((2 * SUBLANES, nk), F32)],
    )
    return pl.pallas_call(
        kern,
        grid_spec=grid_spec,
        out_shape=jax.ShapeDtypeStruct((b, SUBLANES, 256), F32),
        compiler_params=_params("parallel"),
    )(page_table, q, small, o_cmp, sel, new_slc, past_win, new_win, *([cache] * n_pages))


def _transpose_cast_kernel(x_ref, o_ref):
    o_ref[...] = x_ref[...].T.astype(BF16)


def _transpose_cast(x, tr=384):
    r, c = x.shape
    return pl.pallas_call(
        _transpose_cast_kernel,
        grid=(r // tr,),
        in_specs=[pl.BlockSpec((tr, c), lambda i: (i, 0))],
        out_specs=pl.BlockSpec((c, tr), lambda i: (0, i)),
        out_shape=jax.ShapeDtypeStruct((c, r), BF16),
        compiler_params=_params("parallel"),
    )(x)


def _prep_layer(l, P):
    w_t = P["w_in_t"][l]
    rows = lambda lo, hi: w_t[lo:hi]
    small = jnp.concatenate([rows(1024, 1032), rows(2568, 2580), jnp.zeros((LANES - 20, D_MODEL), F32)], axis=0)
    w_perm = _transpose_cast(jnp.concatenate([rows(0, 1024), rows(1032, 1544), rows(1544, 1800), rows(2580, 2836),
                                              rows(2836, 3348), rows(1800, 2568), small], axis=0))
    par = jnp.zeros((2, LANES), F32).at[0, 4:8].set(P["gdn_a_log"][l]).at[1, 4:8].set(P["gdn_dt_bias"][l])

    def block_diag(w):
        return jnp.einsum("ncd,nm->ncmd", w, jnp.eye(4, dtype=F32)).reshape(GROUP_WIDTH, GROUP_WIDTH).astype(BF16)

    eye2 = jnp.eye(2, dtype=F32)
    w1 = P["nsa_cmp_w1"][l]
    half = CMP_LEN // 2

    cmp_w = jnp.einsum("khtde,gG->ktgdhGe", w1.reshape(2, 2, half, HEAD_DIM, HEAD_DIM), eye2)
    cmp_w = cmp_w.reshape(2, half * 128, 256).astype(BF16)
    pe = P["nsa_cmp_pe"][l].reshape(2, 2, half, 1, HEAD_DIM)
    cmp_pe = jnp.broadcast_to(pe, (2, 2, half, 2, HEAD_DIM)).reshape(2, 2, half * 128)
    w2 = jnp.einsum("kef,kK,gG->kgeKGf", P["nsa_cmp_w2"][l], eye2, eye2).reshape(256, 256).astype(BF16)
    w_kv_t = jnp.concatenate([rows(2836, 3348), rows(1800, 2568)], axis=0).astype(BF16)
    return dict(
        norm1=P["norm1"][l][None], w_in=w_perm, w_kv_t=w_kv_t,
        gdn_cw=P["gdn_conv_w"][l], gdn_par=par, gdn_norm=P["gdn_norm"][l][None],
        lru_cw=P["lru_conv_w"][l],
        lru_vec=jnp.stack([P["lru_conv_b"][l], P["lru_ba"][l], P["lru_bx"][l], P["lru_lambda"][l]]),
        lru_wa=block_diag(P["lru_wa"][l]), lru_wx=block_diag(P["lru_wx"][l]),
        cmp_pe=cmp_pe, cmp_w=cmp_w, cmp_w2=w2,
        w_out=P["w_out"][l].astype(BF16), norm2=P["norm2"][l][None],
        wg=P["ffn_w_gate"][l].astype(BF16), wu=P["ffn_w_up"][l].astype(BF16),
        ffn_cw=P["ffn_conv_w"][l], wd=P["ffn_w_down"][l].astype(BF16),
    )


def _rope_tables(pos):
    half = HEAD_DIM // 2
    inv = ROPE_THETA ** (-jnp.arange(half, dtype=F32) / half)
    ang = pos.astype(F32)[:, None] * inv[None, :]
    c, s = jnp.cos(ang), jnp.sin(ang)
    return jnp.tile(jnp.concatenate([c, c], axis=-1), (1, 2)), jnp.tile(jnp.concatenate([-s, s], axis=-1), (1, 2))


PROMPT_TM = 256
PROMPT_TQ = 512
SB_TQ = 256
GDN_CHUNK = 64


def _layer_prompt(x2d, W, b, l, cos, sin, final, norm_f, layer, depth, kv_stacks):
    outs = _in_proj_t(x2d, W["norm1"], W["w_in"], W["w_kv_t"], cos, sin, b, l, PROMPT_TM, layer, depth, kv_stacks)
    zg, zl, nq, sq, cmp_r, small = [o.reshape(b, l, o.shape[-1]) for o in outs[:6]]
    kv_stacks = outs[6:]
    skv_t, _, slc_t, win_t = kv_stacks
    o_gdn, gdn_conv, gdn_s = _gdn(zg, small, jnp.zeros((b, SUBLANES, 768), F32), jnp.zeros((b, N_HEADS, 64, 64), F32),
                                  W["gdn_cw"], W["gdn_par"], W["gdn_norm"], tl=PROMPT_TM, chunk=GDN_CHUNK,
                                  l_real=PROMPT_TM)
    o_lru, lru_conv, lru_h = _lru(zl, jnp.zeros((b, SUBLANES, GROUP_WIDTH), F32), jnp.zeros((b, 1, GROUP_WIDTH), F32),
                                  W["lru_cw"], W["lru_vec"], W["lru_wa"], W["lru_wx"], tl=PROMPT_TM, l_real=PROMPT_TM)
    kcvc = _cmp_prompt(cmp_r.reshape(b, l // CMP_STRIDE, CMP_STRIDE * 256), W["cmp_pe"], W["cmp_w"], W["cmp_w2"])
    o_nsa = _nsa_prompt(nq, small, kcvc, slc_t, win_t, tq=min(PROMPT_TQ, l), layer=layer)
    o_sb = _sb_prompt(sq, skv_t, tq=min(SB_TQ, l), layer=layer)
    mixes = [o.reshape(b * l, GROUP_WIDTH) for o in (o_gdn, o_lru, o_nsa, o_sb)]
    res = _out_ffn(x2d, mixes, W["w_out"], W["norm2"], W["wg"], W["wu"], W["ffn_cw"], W["wd"],
                   jnp.zeros((SUBLANES, D_FF), F32), norm_f, tm=PROMPT_TM, stride=1, tiles_per_seq=l // PROMPT_TM,
                   final=final)
    states = (gdn_conv, gdn_s, lru_conv, lru_h[:, 0], res[1][l // PROMPT_TM - 1::l // PROMPT_TM])
    return res[0], kv_stacks, states, (res[2] if final else None)


def _prompt_kv_outputs(kv_stacks, l):
    skv_t, cmp_t, slc_t, win_t = kv_stacks
    kv6 = lambda a, h: jnp.transpose(a.reshape(a.shape[0], a.shape[1], 2, h, HEAD_DIM, a.shape[-1]), (0, 1, 5, 2, 3, 4))
    return kv6(cmp_t, 2), kv6(slc_t, 2), kv6(skv_t, 4), kv6(win_t[..., l - min(WINDOW, l):], 2)


def _layer_sample(x2d, W, layer, S, cos, sin, final, norm_f):
    page_table = S["page_table"]
    b, n_pages = page_table.shape
    t_new = x2d.shape[0] // b
    pos0 = n_pages * PAGE_SIZE
    n_phys = S["n_phys"]
    outs = _in_proj(x2d, W["norm1"], W["w_in"], cos, sin, x2d.shape[0], 1)

    def to_batch_major(a, rows=SUBLANES):
        a = a.reshape(t_new, b, a.shape[-1]).transpose(1, 0, 2)
        return jnp.pad(a, ((0, 0), (0, rows - t_new), (0, 0)))

    zg, zl, nq, sq, skv, cmp_r, slc_r, win_r, small = [to_batch_major(o) for o in outs]
    front = lambda st: jnp.pad(st, ((0, 0), (SUBLANES - st.shape[1], 0), (0, 0)))
    o_gdn, gdn_conv, gdn_s = _gdn(zg, small, front(S["gdn_conv"][layer]), S["gdn"][layer],
                                  W["gdn_cw"], W["gdn_par"], W["gdn_norm"], tl=SUBLANES, chunk=SUBLANES, l_real=t_new)
    o_lru, lru_conv, lru_h = _lru(zl, front(S["lru_conv"][layer]), S["lru"][layer][:, None, :],
                                  W["lru_cw"], W["lru_vec"], W["lru_wa"], W["lru_wx"], tl=SUBLANES, l_real=t_new)
    base = layer * n_phys
    o_cmp, sel = _nsa_sample_cmp(page_table, nq, W["cmp_pe"], W["cmp_w"], W["cmp_w2"],
                                 S["cmp_pages"], base, pos0=pos0)
    pad_page = lambda a: jnp.pad(a.transpose(0, 2, 1), ((0, 0), (0, 0), (0, PAGE_SIZE - a.shape[1])))
    o_nsa = _nsa_sample_slc(page_table, nq, small, o_cmp, sel, pad_page(slc_r), S["win"], layer * b, pad_page(win_r),
                            S["slc_pages"], base, pos0=pos0)
    head_of_col = jnp.arange(256) // HEAD_DIM
    head_of_row = jnp.arange(N_HEADS * SUBLANES) // SUBLANES
    qbd = jnp.where(head_of_row[:, None] == head_of_col[None, :], jnp.tile(sq, (1, N_HEADS, 1)), 0.0)
    o_sb = _sb_sample(page_table, qbd, pad_page(skv), S["sb_pages"], base, pg=min(32, n_pages), pos0=pos0)

    def to_time_major(a):
        return a[:, :t_new].transpose(1, 0, 2).reshape(t_new * b, a.shape[-1])

    mixes = [to_time_major(o) for o in (o_gdn, o_lru, o_nsa, o_sb)]
    buf = S["ffn_conv"][layer].transpose(1, 0, 2).reshape(2 * b, D_FF)
    res = _out_ffn(x2d, mixes, W["w_out"], W["norm2"], W["wg"], W["wu"], W["ffn_cw"], W["wd"], buf, norm_f,
                   tm=x2d.shape[0], stride=b, tiles_per_seq=1, final=final)
    kv5 = lambda a, h: a[:, :t_new].reshape(b, t_new, 2, h, HEAD_DIM)
    ffn_state = res[1].reshape(2, b, D_FF).transpose(1, 0, 2)
    states = (kv5(cmp_r, 2), kv5(slc_r, 2), kv5(skv, 4), kv5(win_r, 2), gdn_conv, gdn_s, lru_conv, lru_h[:, 0], ffn_state)
    return res[0], states, (res[2] if final else None)


def _sample_state_views(cache_cmp_kv, cache_slc_kv, cache_sb_kv, cache_win_kv):
    def view(c):
        d0, d1, tok, kv, h, d = c.shape
        return jnp.transpose(c, (0, 1, 3, 4, 5, 2)).reshape(d0 * d1, kv * h * d, tok)

    return dict(cmp_pages=view(cache_cmp_kv), slc_pages=view(cache_slc_kv), sb_pages=view(cache_sb_kv),
                win=view(cache_win_kv))


def kernel(x_prompt, x_sample, cache_cmp_kv, cache_slc_kv, cache_sb_kv, cache_win_kv, state_gdn_conv, state_gdn, state_lru_conv, state_lru, state_ffn_conv, page_table, norm1, w_in, gdn_conv_w, gdn_a_log, gdn_dt_bias, gdn_norm, lru_conv_w, lru_conv_b, lru_wa, lru_ba, lru_wx, lru_bx, lru_lambda, nsa_cmp_pe, nsa_cmp_w1, nsa_cmp_w2, w_out, norm2, ffn_w_gate, ffn_w_up, ffn_conv_w, ffn_w_down, norm_f):
    P = dict(norm1=norm1, w_in_t=jnp.transpose(w_in, (0, 2, 1)), gdn_conv_w=gdn_conv_w, gdn_a_log=gdn_a_log, gdn_dt_bias=gdn_dt_bias,
             gdn_norm=gdn_norm, lru_conv_w=lru_conv_w, lru_conv_b=lru_conv_b, lru_wa=lru_wa, lru_ba=lru_ba,
             lru_wx=lru_wx, lru_bx=lru_bx, lru_lambda=lru_lambda, nsa_cmp_pe=nsa_cmp_pe, nsa_cmp_w1=nsa_cmp_w1,
             nsa_cmp_w2=nsa_cmp_w2, w_out=w_out, norm2=norm2, ffn_w_gate=ffn_w_gate, ffn_w_up=ffn_w_up,
             ffn_conv_w=ffn_conv_w, ffn_w_down=ffn_w_down)
    depth = w_in.shape[0]
    weights = [_prep_layer(l, P) for l in range(depth)]
    gf = norm_f[None]

    b, l, _ = x_prompt.shape
    cos, sin = _rope_tables(jnp.arange(l))
    h = x_prompt.reshape(b * l, D_MODEL)
    p_states, y_prompt, kv_stacks = [], None, None
    for layer in range(depth):
        h, kv_stacks, st, y = _layer_prompt(h, weights[layer], b, l, cos, sin, layer == depth - 1, gf, layer, depth,
                                            kv_stacks)
        p_states.append(st)
        y_prompt = y
    y_prompt = y_prompt.reshape(b, l, D_MODEL)
    p_kv = _prompt_kv_outputs(kv_stacks, l)

    db, t_new, _ = x_sample.shape
    n_phys = cache_sb_kv.shape[1]
    n_pages = page_table.shape[1]
    pos0 = n_pages * PAGE_SIZE
    past_win = cache_win_kv.shape[2]
    S = _sample_state_views(cache_cmp_kv, cache_slc_kv, cache_sb_kv, cache_win_kv)
    S.update(page_table=page_table, n_phys=n_phys,
             gdn_conv=state_gdn_conv, gdn=state_gdn, lru_conv=state_lru_conv, lru=state_lru, ffn_conv=state_ffn_conv)
    cos_s, sin_s = _rope_tables(pos0 + jnp.repeat(jnp.arange(t_new), db))
    h = x_sample.transpose(1, 0, 2).reshape(t_new * db, D_MODEL)
    s_states, y_sample = [], None
    for layer in range(depth):
        h, st, y = _layer_sample(h, weights[layer], layer, S, cos_s, sin_s, layer == depth - 1, gf)
        s_states.append(st)
        y_sample = y
    y_sample = y_sample.reshape(t_new, db, D_MODEL).transpose(1, 0, 2)

    stk = lambda states, i: jnp.stack([s[i] for s in states])
    return ((y_prompt, y_sample) + p_kv + tuple(stk(p_states, i) for i in range(5))
            + tuple(stk(s_states, i) for i in range(9)))
```
